```python
import jax, jax.numpy as jnp
from jax import lax
import numpy as np

D_MODEL = 1024
BATCH = 2
SEQ = 16384
DEPTH = 4

GRID_W = 64
CTX_LEN = 256
HEAD_DIM = 64
MIXER_WIDTH = 256
N_MIXERS = 4
MIX_WIDTH = N_MIXERS * MIXER_WIDTH
NA_HEADS = 4
NA_WIN_R = 8
NA_WIN_C = 16
GB_HEADS = 4
GB_KV = 2
GB_GROUP = GB_HEADS // GB_KV
SW_HEADS = 4
SW_KV = 2
SW_GROUP = SW_HEADS // SW_KV
WINDOW = 128
Q_BLOCK = 128
SG_GROUPS = 4
SG_GROUP_DIM = MIXER_WIDTH // SG_GROUPS
CHUNK = 128
N_EXPERTS = 16
EXPERT_FF = 2048
EC_CAPACITY_FACTOR = 2
ROPE_THETA = 10000.0
ROPE_FREQS = HEAD_DIM // 4
EPS = 1e-6
NEG_INF = -1e30
PROJ_SIZES = [MIXER_WIDTH] * 3 + [MIXER_WIDTH, GB_KV * HEAD_DIM, GB_KV * HEAD_DIM] \
    + [MIXER_WIDTH, SW_KV * HEAD_DIM, SW_KV * HEAD_DIM] + [MIXER_WIDTH, MIXER_WIDTH]
IN_WIDTH = sum(PROJ_SIZES)

kernel_name = "hybrid_headgroup_diffusion_trunk"


def rms_norm(x, g):
    xf = x.astype(jnp.float32)
    y = xf * lax.rsqrt(jnp.mean(xf * xf, axis=-1, keepdims=True) + EPS)
    return (y * g.astype(jnp.float32)).astype(x.dtype)


def modulate(h, shift, scale):
    return h * (1 + scale) + shift


def adaln(cvec, w, b):
    m = (jax.nn.silu(cvec) @ w + b)[:, None, :]
    return jnp.split(m, 6, axis=-1)


def rope_tables(n):
    t = jnp.arange(n)
    row = (t // GRID_W).astype(jnp.float32)
    col = (t % GRID_W).astype(jnp.float32)
    freqs = ROPE_THETA ** (-jnp.arange(ROPE_FREQS, dtype=jnp.float32) / ROPE_FREQS)
    ang = jnp.stack([row[:, None] * freqs, col[:, None] * freqs], axis=1)
    return jnp.cos(ang), jnp.sin(ang)


def apply_rope(x, cos, sin):
    xf = x.astype(jnp.float32).reshape(x.shape[:-1] + (2, 2, ROPE_FREQS))
    a, b = xf[..., 0, :], xf[..., 1, :]
    shp = (cos.shape[0],) + (1,) * (x.ndim - 3) + (2, ROPE_FREQS)
    cs, sn = cos.reshape(shp), sin.reshape(shp)
    out = jnp.stack([a * cs - b * sn, a * sn + b * cs], axis=-2)
    return out.reshape(x.shape).astype(x.dtype)


def softmax_parts(parts, sink=None):
    s = jnp.concatenate([p.astype(jnp.float32) for p in parts], axis=-1)
    if sink is not None:
        s = jnp.concatenate([s, jnp.broadcast_to(sink.astype(jnp.float32), s.shape[:-1] + (1,))], axis=-1)
    p = jax.nn.softmax(s, axis=-1)
    outs, off = [], 0
    for part in parts:
        n = part.shape[-1]
        outs.append(p[..., off:off + n])
        off += n
    return outs


def ctx_attention(q, k, v, sink=None):
    s = jnp.einsum('bqhgd,bkhd->bhgqk', q, k) * HEAD_DIM ** -0.5
    (p,) = softmax_parts([s], sink)
    o = jnp.einsum('bhgqk,bkhd->bqhgd', p.astype(v.dtype), v)
    return o.reshape(o.shape[0], o.shape[1], -1)


def neighbourhood_attention(q, k, v, kc, vc, rpb):
    B, N, H, dh = q.shape
    rows = N // GRID_W
    win_r = min(NA_WIN_R, rows)
    qg = q.reshape(B, rows, GRID_W, H, dh)
    kg = k.reshape(B, rows, GRID_W, H, dh)
    vg = v.reshape(B, rows, GRID_W, H, dh)
    cols = np.arange(GRID_W)
    c_start = np.clip(cols - NA_WIN_C // 2, 0, GRID_W - NA_WIN_C)
    col_idx = c_start[:, None] + np.arange(NA_WIN_C)[None, :]
    col_off = col_idx - cols[:, None] + (NA_WIN_C - 1)
    scale = dh ** -0.5

    def row_block(r):
        r_start = jnp.clip(r - win_r // 2, 0, rows - win_r)
        k_sel = lax.dynamic_slice_in_dim(kg, r_start, win_r, axis=1)[:, :, col_idx]
        v_sel = lax.dynamic_slice_in_dim(vg, r_start, win_r, axis=1)[:, :, col_idx]
        q_r = lax.dynamic_index_in_dim(qg, r, axis=1, keepdims=False)
        s_loc = jnp.einsum('bqhd,bwqjhd->bhqwj', q_r, k_sel) * scale
        row_off = r_start + jnp.arange(win_r) - r + (NA_WIN_R - 1)
        bias = rpb[:, row_off[:, None, None], col_off[None]]
        s_loc = (s_loc + bias.transpose(0, 2, 1, 3)[None]).reshape(B, H, GRID_W, win_r * NA_WIN_C)
        s_ctx = jnp.einsum('bqhd,bkhd->bhqk', q_r, kc) * scale
        p_loc, p_ctx = softmax_parts([s_loc, s_ctx])
        p_loc = p_loc.reshape(B, H, GRID_W, win_r, NA_WIN_C).astype(v.dtype)
        return (jnp.einsum('bhqwj,bwqjhd->bqhd', p_loc, v_sel)
                + jnp.einsum('bhqk,bkhd->bqhd', p_ctx.astype(v.dtype), vc))

    y = lax.map(row_block, jnp.arange(rows))
    return y.transpose(1, 0, 2, 3, 4).reshape(B, N, H * dh)


def global_attention(q, k, v, kc, vc):
    B, N = q.shape[:2]
    nb = N // Q_BLOCK
    scale = HEAD_DIM ** -0.5
    q_blocks = q.reshape((B, nb, Q_BLOCK) + q.shape[2:]).swapaxes(0, 1)

    def block(qb):
        s_lat = jnp.einsum('bqhgd,bkhd->bhgqk', qb, k) * scale
        s_ctx = jnp.einsum('bqhgd,bkhd->bhgqk', qb, kc) * scale
        p_lat, p_ctx = softmax_parts([s_lat, s_ctx])
        return (jnp.einsum('bhgqk,bkhd->bqhgd', p_lat.astype(v.dtype), v)
                + jnp.einsum('bhgqk,bkhd->bqhgd', p_ctx.astype(v.dtype), vc))

    y = lax.map(block, q_blocks)
    return y.swapaxes(0, 1).reshape(B, N, -1)


def window_attention(q, k, v, kc, vc, sink):
    B, N, HKV, G, dh = q.shape
    nb = N // Q_BLOCK
    scale = dh ** -0.5
    pad = ((0, 0), (Q_BLOCK, Q_BLOCK), (0, 0), (0, 0))
    kp = jnp.pad(k, pad).reshape(B, nb + 2, Q_BLOCK, HKV, dh)
    vp = jnp.pad(v, pad).reshape(B, nb + 2, Q_BLOCK, HKV, dh)
    k_band = jnp.concatenate([kp[:, i:i + nb] for i in range(3)], axis=2)
    v_band = jnp.concatenate([vp[:, i:i + nb] for i in range(3)], axis=2)
    a = np.arange(Q_BLOCK)[:, None]
    j = np.arange(3 * Q_BLOCK)[None, :]
    in_window = np.abs(j - Q_BLOCK - a) <= WINDOW
    kpos = (np.arange(nb)[:, None] - 1) * Q_BLOCK + np.arange(3 * Q_BLOCK)[None, :]
    in_range = (kpos >= 0) & (kpos < N)
    mask = in_window[None] & in_range[:, None, :]
    sink_b = sink.reshape(HKV, G, 1, 1)
    xs = (q.reshape(B, nb, Q_BLOCK, HKV, G, dh).swapaxes(0, 1),
          k_band.swapaxes(0, 1), v_band.swapaxes(0, 1), jnp.asarray(mask))

    def block(args):
        qb, kb, vb, m = args
        s_loc = jnp.einsum('bqhgd,bkhd->bhgqk', qb, kb).astype(jnp.float32) * scale
        s_loc = jnp.where(m, s_loc, NEG_INF)
        s_ctx = jnp.einsum('bqhgd,bkhd->bhgqk', qb, kc) * scale
        p_loc, p_ctx = softmax_parts([s_loc, s_ctx], sink_b)
        return (jnp.einsum('bhgqk,bkhd->bqhgd', p_loc.astype(v.dtype), vb)
                + jnp.einsum('bhgqk,bkhd->bqhgd', p_ctx.astype(v.dtype), vc))

    y = lax.map(block, xs)
    return y.swapaxes(0, 1).reshape(B, N, -1)


def spatial_gating(u, v, g_norm, w_s, b_s):
    B, N, _ = u.shape
    nc = N // CHUNK
    u = jax.nn.gelu(u)
    v = rms_norm(jax.nn.gelu(v), g_norm).reshape(B, nc, CHUNK, SG_GROUPS, SG_GROUP_DIM)
    mixed = jnp.einsum('gts,bcsgd->bctgd', w_s, v) + b_s.T[None, None, :, :, None]
    return u * mixed.reshape(B, N, MIXER_WIDTH)


def split_projection(p):
    return jnp.split(p, [int(s) for s in np.cumsum(PROJ_SIZES)[:-1]], axis=-1)


def merge_groups(parts, out_norm, w_out):
    y = jnp.concatenate(parts, axis=-1)
    yg = rms_norm(y.reshape(y.shape[:-1] + (N_MIXERS, MIXER_WIDTH)), out_norm.reshape(N_MIXERS, MIXER_WIDTH))
    return yg.reshape(y.shape) @ w_out


def token_mixer(h, hc, cos, sin, w_in, rpb, q_norm, k_norm, sink, sgu_norm, w_sgu, b_sgu,
                out_norm, w_out, ctx_out):
    B, N, _ = h.shape
    Nc = hc.shape[1]
    qa, ka, va, qb, kb, vb, qs, ks, vs, su, sv = split_projection(h @ w_in)
    qa_c, ka_c, va_c, qb_c, kb_c, vb_c, qs_c, ks_c, vs_c, su_c, sv_c = split_projection(hc @ w_in)
    qa, ka, va = [t.reshape(B, N, NA_HEADS, HEAD_DIM) for t in (qa, ka, va)]
    qa_c, ka_c, va_c = [t.reshape(B, Nc, NA_HEADS, HEAD_DIM) for t in (qa_c, ka_c, va_c)]
    y_a = neighbourhood_attention(qa, ka, va, ka_c, va_c, rpb)
    qb = apply_rope(rms_norm(qb.reshape(B, N, GB_KV, GB_GROUP, HEAD_DIM), q_norm), cos, sin)
    kb = apply_rope(rms_norm(kb.reshape(B, N, GB_KV, HEAD_DIM), k_norm), cos, sin)
    vb = vb.reshape(B, N, GB_KV, HEAD_DIM)
    qb_c = rms_norm(qb_c.reshape(B, Nc, GB_KV, GB_GROUP, HEAD_DIM), q_norm)
    kb_c = rms_norm(kb_c.reshape(B, Nc, GB_KV, HEAD_DIM), k_norm)
    vb_c = vb_c.reshape(B, Nc, GB_KV, HEAD_DIM)
    y_b = global_attention(qb, kb, vb, kb_c, vb_c)
    qs = apply_rope(qs.reshape(B, N, SW_KV, SW_GROUP, HEAD_DIM), cos, sin)
    ks = apply_rope(ks.reshape(B, N, SW_KV, HEAD_DIM), cos, sin)
    vs = vs.reshape(B, N, SW_KV, HEAD_DIM)
    qs_c = qs_c.reshape(B, Nc, SW_KV, SW_GROUP, HEAD_DIM)
    ks_c = ks_c.reshape(B, Nc, SW_KV, HEAD_DIM)
    vs_c = vs_c.reshape(B, Nc, SW_KV, HEAD_DIM)
    y_c = window_attention(qs, ks, vs, ks_c, vs_c, sink)
    y_d = spatial_gating(su, sv, sgu_norm, w_sgu, b_sgu)
    y = merge_groups([y_a, y_b, y_c, y_d], out_norm, w_out)
    if not ctx_out:
        return y, None
    ya_c = ctx_attention(qa_c[:, :, :, None], ka_c, va_c)
    yb_c = ctx_attention(qb_c, kb_c, vb_c)
    yc_c = ctx_attention(qs_c, ks_c, vs_c, sink.reshape(SW_KV, SW_GROUP, 1, 1))
    yd_c = spatial_gating(su_c, sv_c, sgu_norm, w_sgu, b_sgu)
    y_ctx = merge_groups([ya_c, yb_c, yc_c, yd_c], out_norm, w_out)
    return y, y_ctx


def expert_choice_moe(h, w_router, w_gate, w_up, w_down):
    B, N, D = h.shape
    cap = EC_CAPACITY_FACTOR * N // N_EXPERTS
    logits = jnp.einsum('bnd,de->ben', h, w_router).astype(jnp.float32)
    aff = jax.nn.softmax(logits, axis=1)
    gate, idx = lax.top_k(aff, cap)
    xe = jax.vmap(lambda hb, ib: hb[ib])(h, idx)
    hid = jax.nn.silu(jnp.einsum('becd,edf->becf', xe, w_gate)) * jnp.einsum('becd,edf->becf', xe, w_up)
    ye = jnp.einsum('becf,efd->becd', hid, w_down) * gate[..., None].astype(h.dtype)
    return jax.vmap(lambda yb, ib: jnp.zeros((N, D), yb.dtype).at[ib.reshape(-1)].add(yb.reshape(-1, D)))(ye, idx)


def setup_inputs(seed: int = 0) -> dict:
    key = jax.random.key(seed)
    ks = jax.random.split(key, 24)

    def nrm(k, shape, scale):
        return jax.random.normal(k, shape, jnp.float32) * scale

    return {
        "x": nrm(ks[0], (BATCH, SEQ, D_MODEL), 1.0),
        "c": nrm(ks[1], (BATCH, D_MODEL), 1.0),
        "ctx": nrm(ks[2], (BATCH, CTX_LEN, D_MODEL), 1.0),
        "c_ctx": nrm(ks[3], (D_MODEL,), 1.0),
        "w_mod": nrm(ks[4], (DEPTH, D_MODEL, 6 * D_MODEL), 0.5 * D_MODEL ** -0.5),
        "b_mod": nrm(ks[5], (DEPTH, 6 * D_MODEL), 0.02),
        "norm_mix": 1.0 + nrm(ks[6], (DEPTH, D_MODEL), 0.02),
        "norm_ffn": 1.0 + nrm(ks[7], (DEPTH, D_MODEL), 0.02),
        "w_in": nrm(ks[8], (DEPTH, D_MODEL, IN_WIDTH), D_MODEL ** -0.5),
        "rpb": nrm(ks[9], (DEPTH, NA_HEADS, 2 * NA_WIN_R - 1, 2 * NA_WIN_C - 1), 0.1),
        "q_norm": 1.0 + nrm(ks[10], (DEPTH, HEAD_DIM), 0.02),
        "k_norm": 1.0 + nrm(ks[11], (DEPTH, HEAD_DIM), 0.02),
        "sink": nrm(ks[12], (DEPTH, SW_HEADS), 0.5),
        "sgu_norm": 1.0 + nrm(ks[13], (DEPTH, MIXER_WIDTH), 0.02),
        "w_sgu": nrm(ks[14], (DEPTH, SG_GROUPS, CHUNK, CHUNK), CHUNK ** -0.5),
        "b_sgu": nrm(ks[15], (DEPTH, SG_GROUPS, CHUNK), 0.02),
        "out_norm": 1.0 + nrm(ks[16], (DEPTH, MIX_WIDTH), 0.02),
        "w_out": nrm(ks[17], (DEPTH, MIX_WIDTH, D_MODEL), MIX_WIDTH ** -0.5),
        "w_router": nrm(ks[18], (DEPTH, D_MODEL, N_EXPERTS), D_MODEL ** -0.5),
        "w_gate": nrm(ks[19], (DEPTH, N_EXPERTS, D_MODEL, EXPERT_FF), D_MODEL ** -0.5),
        "w_up": nrm(ks[20], (DEPTH, N_EXPERTS, D_MODEL, EXPERT_FF), D_MODEL ** -0.5),
        "w_down": nrm(ks[21], (DEPTH, N_EXPERTS, EXPERT_FF, D_MODEL), EXPERT_FF ** -0.5),
        "final_norm": 1.0 + nrm(ks[22], (D_MODEL,), 0.02),
    }


def reference(x, c, ctx, c_ctx, w_mod, b_mod, norm_mix, norm_ffn, w_in, rpb, q_norm, k_norm, sink,
              sgu_norm, w_sgu, b_sgu, out_norm, w_out, w_router, w_gate, w_up, w_down, final_norm):
    N = x.shape[1]
    cos, sin = rope_tables(N)
    xc = ctx
    for l in range(DEPTH):
        ctx_needed = l < DEPTH - 1
        sh1, sc1, g1, sh2, sc2, g2 = adaln(c, w_mod[l], b_mod[l])
        csh1, csc1, cg1, csh2, csc2, cg2 = adaln(c_ctx[None], w_mod[l], b_mod[l])
        h = modulate(rms_norm(x, norm_mix[l]), sh1, sc1)
        hc = modulate(rms_norm(xc, norm_mix[l]), csh1, csc1)
        y, y_ctx = token_mixer(h, hc, cos, sin, w_in[l], rpb[l], q_norm[l], k_norm[l], sink[l],
                               sgu_norm[l], w_sgu[l], b_sgu[l], out_norm[l], w_out[l], ctx_needed)
        x = x + g1 * y
        h = modulate(rms_norm(x, norm_ffn[l]), sh2, sc2)
        x = x + g2 * expert_choice_moe(h, w_router[l], w_gate[l], w_up[l], w_down[l])
        if ctx_needed:
            xc = xc + cg1 * y_ctx
            hc = modulate(rms_norm(xc, norm_ffn[l]), csh2, csc2)
            xc = xc + cg2 * expert_choice_moe(hc, w_router[l], w_gate[l], w_up[l], w_down[l])
    return rms_norm(x, final_norm)
```

```python
import functools

import numpy as np
import jax
import jax.numpy as jnp
from jax import lax
from jax.experimental import pallas as pl
from jax.experimental.pallas import tpu as pltpu

HEAD_DIM = 64
GRID_W = 64
MIXER_WIDTH = 256
NA_WIN_R = 8
NA_WIN_C = 16
WINDOW = 128
CHUNK = 128
SG_GROUPS = 4
EC_CAPACITY_FACTOR = 2
ROPE_THETA = 10000.0
ROPE_FREQS = HEAD_DIM // 4
EPS = 1e-6
NEG_INF = -1e30
PROJ_SIZES = (256, 256, 256, 256, 128, 128, 256, 128, 128, 256, 256)
PROJ_OFFS = tuple(int(v) for v in np.cumsum((0,) + PROJ_SIZES))
Q_SCALE = HEAD_DIM ** -0.5

LANES = 128
VMEM_LIMIT = 56 * 2 ** 20
ATT_BLOCK = 256
MXU_DTYPE = jnp.bfloat16
ACT_DTYPE = jnp.bfloat16
F32 = jnp.float32
HI = lax.Precision.HIGHEST
NT_DIMS = (((1,), (1,)), ((), ()))
TN_DIMS = (((0,), (0,)), ((), ()))


def _params(*sem):
    return pltpu.CompilerParams(dimension_semantics=sem, vmem_limit_bytes=VMEM_LIMIT)


def _mm(a, b):
    return jnp.dot(a.astype(MXU_DTYPE), b.astype(MXU_DTYPE), preferred_element_type=F32)


def _rms(x, g):
    return x * lax.rsqrt(jnp.mean(x * x, axis=-1, keepdims=True) + EPS) * g


def _silu(x):
    return x / (1.0 + jnp.exp(-x))


def _mod_body(c_ref, w_ref, b_ref, o_ref):
    s = _silu(c_ref[...])
    o_ref[0] = jnp.dot(s, w_ref[0], precision=HI, preferred_element_type=F32) + b_ref[0]


def _adaln_all(cvecs, w_mod, b_mod):
    depth, d, d6 = w_mod.shape
    tn = 1536
    rows = cvecs.shape[0]
    return pl.pallas_call(
        _mod_body,
        out_shape=jax.ShapeDtypeStruct((depth, rows, d6), F32),
        grid=(depth, d6 // tn),
        in_specs=[pl.BlockSpec((rows, d), lambda l, j: (0, 0)),
                  pl.BlockSpec((1, d, tn), lambda l, j: (l, 0, j)),
                  pl.BlockSpec((1, 1, tn), lambda l, j: (l, 0, j))],
        out_specs=pl.BlockSpec((1, rows, tn), lambda l, j: (l, 0, j)),
        compiler_params=_params("parallel", "parallel"),
        name="adaln",
    )(cvecs, w_mod, b_mod.reshape(depth, 1, d6))


def _head_rms(t, g, gsum_ref):
    w = t.shape[-1]
    sq = t * t
    hi = sq.astype(MXU_DTYPE)
    lo = (sq - hi.astype(F32)).astype(MXU_DTYPE)
    gs = gsum_ref[0:w, 0:w]
    ss = jnp.dot(hi, gs, preferred_element_type=F32) + jnp.dot(lo, gs, preferred_element_type=F32)
    return t * lax.rsqrt(ss * (1.0 / HEAD_DIM) + EPS) * g


def _rope(t, cos, sin_signed):
    w = t.shape[-1]
    rep = w // LANES
    if rep > 1:
        cos = jnp.concatenate([cos] * rep, axis=-1)
        sin_signed = jnp.concatenate([sin_signed] * rep, axis=-1)
    lane = lax.broadcasted_iota(jnp.int32, t.shape, 1)
    first_half = (lane % (2 * ROPE_FREQS)) < ROPE_FREQS
    partner = jnp.where(first_half, pltpu.roll(t, w - ROPE_FREQS, 1), pltpu.roll(t, ROPE_FREQS, 1))
    return t * cos + partner * sin_signed


def _inproj_body(x_ref, nw_ref, sh_ref, sc_ref, w_ref, cos_ref, sin_ref, qn_ref, kn_ref, gn_ref,
                 ws_ref, bs_ref, gsum_ref,
                 qa_ref, ka_ref, va_ref, qb_ref, kb_ref, vb_ref, qs_ref, ks_ref, vs_ref, yd_ref):
    x = x_ref[0]
    h = _rms(x, nw_ref[...]) * (1.0 + sc_ref[0]) + sh_ref[0]
    p = _mm(h, w_ref[...])
    o = PROJ_OFFS
    cos, sin = cos_ref[...], sin_ref[...]
    dt = qa_ref.dtype
    qa_ref[0] = (p[:, o[0]:o[1]] * Q_SCALE).astype(dt)
    ka_ref[0] = p[:, o[1]:o[2]].astype(dt)
    va_ref[0] = p[:, o[2]:o[3]].astype(dt)
    qb = _rope(_head_rms(p[:, o[3]:o[4]], qn_ref[...], gsum_ref), cos, sin)
    qb_ref[0] = (qb * Q_SCALE).astype(dt)
    kb_ref[0] = _rope(_head_rms(p[:, o[4]:o[5]], kn_ref[...], gsum_ref), cos, sin).astype(dt)
    vb_ref[0] = p[:, o[5]:o[6]].astype(dt)
    qs_ref[0] = (_rope(p[:, o[6]:o[7]], cos, sin) * Q_SCALE).astype(dt)
    ks_ref[0] = _rope(p[:, o[7]:o[8]], cos, sin).astype(dt)
    vs_ref[0] = p[:, o[8]:o[9]].astype(dt)
    u = jax.nn.gelu(p[:, o[9]:o[10]])
    v = _rms(jax.nn.gelu(p[:, o[10]:o[11]]), gn_ref[...]).astype(MXU_DTYPE)
    lane_group = lax.broadcasted_iota(jnp.int32, (CHUNK, MIXER_WIDTH), 1) // (MIXER_WIDTH // SG_GROUPS)
    for c in range(x.shape[0] // CHUNK):
        rows = slice(c * CHUNK, (c + 1) * CHUNK)
        mixed = bs_ref[...]
        for g in range(SG_GROUPS):
            mg = jnp.dot(ws_ref[g], v[rows], preferred_element_type=F32)
            mixed = mixed + jnp.where(lane_group == g, mg, 0.0)
        yd_ref[0, rows, :] = (u[rows] * mixed).astype(dt)


def _inproj(x, nw, shift, scale, lw, cos, sin, tm):
    b, t, d = x.shape
    widths = PROJ_SIZES[:9] + (MIXER_WIDTH,)
    row = lambda bi, i: (0, 0)
    per_b = lambda bi, i: (bi, 0, 0)
    tile = lambda bi, i: (bi, i, 0)
    return pl.pallas_call(
        _inproj_body,
        out_shape=[jax.ShapeDtypeStruct((b, t, w), ACT_DTYPE) for w in widths],
        grid=(b, t // tm),
        in_specs=[pl.BlockSpec((1, tm, d), tile),
                  pl.BlockSpec((1, d), row),
                  pl.BlockSpec((1, 1, d), per_b),
                  pl.BlockSpec((1, 1, d), per_b),
                  pl.BlockSpec(lw["w_in"].shape, row),
                  pl.BlockSpec((tm, LANES), lambda bi, i: (i, 0)),
                  pl.BlockSpec((tm, LANES), lambda bi, i: (i, 0)),
                  pl.BlockSpec((1, 256), row),
                  pl.BlockSpec((1, 128), row),
                  pl.BlockSpec((1, 256), row),
                  pl.BlockSpec((SG_GROUPS, CHUNK, CHUNK), lambda bi, i: (0, 0, 0)),
                  pl.BlockSpec((CHUNK, MIXER_WIDTH), row),
                  pl.BlockSpec((256, 256), row)],
        out_specs=[pl.BlockSpec((1, tm, w), tile) for w in widths],
        compiler_params=_params("parallel", "parallel"),
        name="inproj",
    )(x, nw, shift, scale, lw["w_in"], cos, sin, lw["qn"], lw["kn"], lw["gn"], lw["w_sgu"],
      lw["b_sgu"], lw["gsum"])


def _attend(q, chunks, sink=None):
    scores = []
    for k, _, bias in chunks:
        s = lax.dot_general(q, k, NT_DIMS, preferred_element_type=F32)
        scores.append(s if bias is None else s + bias)
    m = functools.reduce(jnp.maximum, [jnp.max(s, axis=-1, keepdims=True) for s in scores])
    if sink is not None:
        m = jnp.maximum(m, sink)
    l = jnp.zeros_like(m)
    o = jnp.zeros((q.shape[0], HEAD_DIM), F32)
    for s, (_, v, _) in zip(scores, chunks):
        p = jnp.exp(s - m)
        l = l + jnp.sum(p, axis=-1, keepdims=True)
        o = o + jnp.dot(p.astype(v.dtype), v, preferred_element_type=F32)
    if sink is not None:
        l = l + jnp.exp(sink - m)
    return o / l


def _head(ref, h):
    return ref[0, :, h * HEAD_DIM:(h + 1) * HEAD_DIM]


def _local_attn_body(*refs, group, has_sink):
    if has_sink:
        sink_ref, refs = refs[0], refs[1:]
    q_ref, k0, k1, k2, v0, v1, v2, kc_ref, vc_ref, bias_ref, o_ref = refs
    outs = []
    for h in range(q_ref.shape[-1] // HEAD_DIM):
        kv = h // group
        hb = h if bias_ref.shape[1] > 1 else 0
        chunks = [(_head(kr, kv), _head(vr, kv), bias_ref[0, hb, j])
                  for j, (kr, vr) in enumerate(((k0, v0), (k1, v1), (k2, v2)))]
        chunks.append((_head(kc_ref, kv), _head(vc_ref, kv), None))
        outs.append(_attend(_head(q_ref, h), chunks, sink_ref[h] if has_sink else None))
    o_ref[0] = jnp.concatenate(outs, axis=-1).astype(o_ref.dtype)


def _local_attn(q, k, v, kc, vc, bias, sink=None):
    b, n, qw = q.shape
    kw = k.shape[-1]
    nc = kc.shape[1]
    tq = ATT_BLOCK
    nblk = n // tq
    assert nblk >= 4
    group = qw // kw
    hb = bias.shape[1]

    def kmap(j):
        return lambda bi, i: (bi, jnp.clip(i - 1, 0, nblk - 3) + j, 0)

    def bmap(bi, i):
        return (jnp.where(i == 0, 0, jnp.where(i == nblk - 1, 2, 1)), 0, 0, 0, 0)

    in_specs = [pl.BlockSpec((1, tq, qw), lambda bi, i: (bi, i, 0))]
    in_specs += [pl.BlockSpec((1, tq, kw), kmap(j)) for j in range(3)] * 2
    in_specs += [pl.BlockSpec((1, nc, kw), lambda bi, i: (bi, 0, 0))] * 2
    in_specs += [pl.BlockSpec((1, hb, 3, tq, tq), bmap)]
    args = [q, k, k, k, v, v, v, kc, vc, bias]
    if sink is not None:
        in_specs = [pl.BlockSpec(memory_space=pltpu.SMEM)] + in_specs
        args = [sink] + args
    return pl.pallas_call(
        functools.partial(_local_attn_body, group=group, has_sink=sink is not None),
        out_shape=jax.ShapeDtypeStruct((b, n, qw), ACT_DTYPE),
        grid=(b, nblk),
        in_specs=in_specs,
        out_specs=pl.BlockSpec((1, tq, qw), lambda bi, i: (bi, i, 0)),
        compiler_params=_params("parallel", "parallel"),
        name="local_attn",
    )(*args)


def _ctx_attn_body(sink_ref, qa, ka, va, qb, kb, vb, qs, ks, vs, oa, ob, oc):
    for q_ref, k_ref, v_ref, o_ref, group, use_sink in (
            (qa, ka, va, oa, 1, False), (qb, kb, vb, ob, 2, False), (qs, ks, vs, oc, 2, True)):
        outs = []
        for h in range(q_ref.shape[-1] // HEAD_DIM):
            kv = h // group
            outs.append(_attend(_head(q_ref, h), [(_head(k_ref, kv), _head(v_ref, kv), None)],
                                sink_ref[h] if use_sink else None))
        o_ref[0] = jnp.concatenate(outs, axis=-1).astype(o_ref.dtype)


def _ctx_attn(sink, qkv):
    b, nc, _ = qkv[0].shape
    spec = lambda a: pl.BlockSpec((1, nc, a.shape[-1]), lambda bi: (bi, 0, 0))
    return pl.pallas_call(
        _ctx_attn_body,
        out_shape=[jax.ShapeDtypeStruct((b, nc, MIXER_WIDTH), ACT_DTYPE)] * 3,
        grid=(b,),
        in_specs=[pl.BlockSpec(memory_space=pltpu.SMEM)] + [spec(a) for a in qkv],
        out_specs=[pl.BlockSpec((1, nc, MIXER_WIDTH), lambda bi: (bi, 0, 0))] * 3,
        compiler_params=_params("parallel"),
        name="ctx_attn",
    )(sink, *qkv)


def _global_attn_body(q_ref, k_ref, v_ref, o_ref, *, tk):
    tq = q_ref.shape[1]
    nk = k_ref.shape[1]
    group_w = 2 * HEAD_DIM
    outs = []
    for kv in range(k_ref.shape[-1] // HEAD_DIM):
        qh = jnp.concatenate([q_ref[0, :, kv * group_w:kv * group_w + HEAD_DIM],
                              q_ref[0, :, kv * group_w + HEAD_DIM:(kv + 1) * group_w]], axis=0)

        def step(i, carry, kv=kv, qh=qh):
            m, l, acc = carry
            ks = pl.multiple_of(i * tk, tk)
            k = k_ref[0, pl.ds(ks, tk), kv * HEAD_DIM:(kv + 1) * HEAD_DIM]
            v = v_ref[0, pl.ds(ks, tk), kv * HEAD_DIM:(kv + 1) * HEAD_DIM]
            s = lax.dot_general(qh, k, NT_DIMS, preferred_element_type=F32)
            m_new = jnp.maximum(m, jnp.max(s, axis=-1, keepdims=True))
            alpha = jnp.exp(m - m_new)
            p = jnp.exp(s - m_new)
            l = alpha * l + jnp.sum(p, axis=-1, keepdims=True)
            acc = alpha * acc + jnp.dot(p.astype(v.dtype), v, preferred_element_type=F32)
            return m_new, l, acc

        init = (jnp.full((2 * tq, 1), NEG_INF, F32), jnp.zeros((2 * tq, 1), F32),
                jnp.zeros((2 * tq, HEAD_DIM), F32))
        _, l, acc = lax.fori_loop(0, nk // tk, step, init)
        o = acc / l
        outs += [o[:tq], o[tq:]]
    o_ref[0] = jnp.concatenate(outs, axis=-1).astype(o_ref.dtype)


def _global_attn(q, k, v):
    b, n, qw = q.shape
    nk, kw = k.shape[1:]
    tq = ATT_BLOCK
    tk = 1280 if nk % 1280 == 0 else 256
    return pl.pallas_call(
        functools.partial(_global_attn_body, tk=tk),
        out_shape=jax.ShapeDtypeStruct((b, n, qw), ACT_DTYPE),
        grid=(b, n // tq),
        in_specs=[pl.BlockSpec((1, tq, qw), lambda bi, i: (bi, i, 0)),
                  pl.BlockSpec((1, nk, kw), lambda bi, i: (bi, 0, 0)),
                  pl.BlockSpec((1, nk, kw), lambda bi, i: (bi, 0, 0))],
        out_specs=pl.BlockSpec((1, tq, qw), lambda bi, i: (bi, i, 0)),
        compiler_params=_params("parallel", "parallel"),
        name="global_attn",
    )(q, k, v)


def _merge_body(ya, yb, yc, yd, x_ref, on_ref, wo_ref, g1_ref, nf_ref, sh_ref, sc_ref, wr_ref,
                xo_ref, h_ref, aff_ref):
    parts = []
    for j, r in enumerate((ya, yb, yc, yd)):
        y = r[0].astype(F32)
        parts.append(_rms(y, on_ref[:, j * MIXER_WIDTH:(j + 1) * MIXER_WIDTH]).astype(MXU_DTYPE))
    xn = x_ref[0] + g1_ref[0] * _mm(jnp.concatenate(parts, axis=-1), wo_ref[...])
    xo_ref[0] = xn
    h = _rms(xn, nf_ref[...]) * (1.0 + sc_ref[0]) + sh_ref[0]
    h_ref[0] = h
    logits = lax.dot_general(wr_ref[...], h, NT_DIMS, precision=HI, preferred_element_type=F32)
    e = jnp.exp(logits - jnp.max(logits, axis=0, keepdims=True))
    aff_ref[0] = e / jnp.sum(e, axis=0, keepdims=True)


def _merge(ys, x, lw, g1, shift, scale, tm):
    b, t, d = x.shape
    ne = lw["w_router_t"].shape[0]
    row = lambda bi, i: (0, 0)
    per_b = lambda bi, i: (bi, 0, 0)
    tile = lambda bi, i: (bi, i, 0)
    return pl.pallas_call(
        _merge_body,
        out_shape=[jax.ShapeDtypeStruct((b, t, d), F32), jax.ShapeDtypeStruct((b, t, d), F32),
                   jax.ShapeDtypeStruct((b, ne, t), F32)],
        grid=(b, t // tm),
        in_specs=[pl.BlockSpec((1, tm, MIXER_WIDTH), tile)] * 4 + [
            pl.BlockSpec((1, tm, d), tile),
            pl.BlockSpec((1, d), row),
            pl.BlockSpec((d, d), row),
            pl.BlockSpec((1, 1, d), per_b),
            pl.BlockSpec((1, d), row),
            pl.BlockSpec((1, 1, d), per_b),
            pl.BlockSpec((1, 1, d), per_b),
            pl.BlockSpec((ne, d), row)],
        out_specs=[pl.BlockSpec((1, tm, d), tile), pl.BlockSpec((1, tm, d), tile),
                   pl.BlockSpec((1, ne, tm), lambda bi, i: (bi, 0, i))],
        compiler_params=_params("parallel", "parallel"),
        name="merge",
    )(*ys, x, lw["out_norm"], lw["w_out"], g1, lw["norm_ffn"], shift, scale, lw["w_router_t"])


def _tri(n, m, mode):
    r = lax.broadcasted_iota(jnp.int32, (n, m), 0)
    c = lax.broadcasted_iota(jnp.int32, (n, m), 1)
    return jnp.where({"lt": r < c, "le": r <= c, "gt": r > c}[mode], 1.0, 0.0).astype(MXU_DTYPE)


def _count(mask, axes):
    out = jnp.where(mask, 1.0, 0.0)
    for ax in sorted(axes, reverse=True):
        out = jnp.sum(out, axis=ax, keepdims=True)
    return out


def _kth_largest_bits(bits, cap, axes):
    shape = tuple(1 if a in axes else s for a, s in enumerate(bits.shape))

    def body(i, t):
        cand = t | lax.shift_left(jnp.int32(1), 30 - i)
        return jnp.where(_count(bits >= cand, axes) >= cap, cand, t)

    return lax.fori_loop(0, 31, body, jnp.zeros(shape, jnp.int32))


def _prefix_tokens(m, exact_rows):
    e, r, l = m.shape
    m2 = m.reshape(e * r, l).astype(MXU_DTYPE)
    local = jnp.dot(m2, _tri(l, l, "lt"), preferred_element_type=F32)
    rowtot = jnp.dot(m2, jnp.ones((l, l), MXU_DTYPE), preferred_element_type=F32).reshape(e, r, l)
    below = _tri(r, r, "gt")
    if exact_rows:
        base = [jnp.dot(below, rowtot[i].astype(MXU_DTYPE), preferred_element_type=F32) for i in range(e)]
    else:
        base = [jnp.dot(below.astype(F32), rowtot[i], precision=HI, preferred_element_type=F32)
                for i in range(e)]
    return local.reshape(e, r, l) + jnp.stack(base, axis=0)


def _select_mask(aff, cap, prefix_fn, axes):
    bits = pltpu.bitcast(aff, jnp.int32)
    thr = _kth_largest_bits(bits, cap, axes)
    gt = bits > thr
    eq = bits == thr
    need = cap - _count(gt, axes)
    eq_rank = prefix_fn(jnp.where(eq, 1.0, 0.0))
    take_eq = jnp.where(eq, jnp.where(eq_rank < need, 1.0, 0.0), 0.0)
    return jnp.where(gt, 1.0, take_eq)


def _select_body(aff_ref, sel_ref, prank_ref, tstart_ref, tend_ref, *, cap):
    aff = aff_ref[0]
    ne = aff.shape[0]
    sel = _select_mask(aff, cap, functools.partial(_prefix_tokens, exact_rows=True), (1, 2))
    sel_ref[0] = sel
    cnt = jnp.sum(sel, axis=0)
    tstart = _prefix_tokens(cnt[None], exact_rows=False)[0]
    tstart_ref[0] = tstart.astype(jnp.int32)
    tend_ref[0] = (tstart + cnt).astype(jnp.int32)
    run = tstart
    for e in range(ne):
        prank_ref[0, e] = run.astype(jnp.int32)
        run = run + sel[e]


def _select(aff4, cap):
    b, ne, r, l = aff4.shape
    blk4 = pl.BlockSpec((1, ne, r, l), lambda bi: (bi, 0, 0, 0))
    blk3 = pl.BlockSpec((1, r, l), lambda bi: (bi, 0, 0))
    return pl.pallas_call(
        functools.partial(_select_body, cap=cap),
        out_shape=[jax.ShapeDtypeStruct((b, ne, r, l), F32), jax.ShapeDtypeStruct((b, ne, r, l), jnp.int32),
                   jax.ShapeDtypeStruct((b, r, l), jnp.int32), jax.ShapeDtypeStruct((b, r, l), jnp.int32)],
        grid=(b,),
        in_specs=[blk4],
        out_specs=[blk4, blk4, blk3, blk3],
        compiler_params=_params("parallel"),
        name="moe_select",
    )(aff4)


def _slots_body(sel_ref, aff_ref, prank_ref, idx_ref, dest_ref, gate_ref):
    m = sel_ref[0, 0]
    r, l = m.shape
    cap = idx_ref.shape[2]
    mb = m.astype(MXU_DTYPE)
    linc = jnp.dot(mb, _tri(l, l, "le"), preferred_element_type=F32)
    rowtot = jnp.dot(mb, jnp.ones((l, l), MXU_DTYPE), preferred_element_type=F32)
    rowtot_lane = lax.dot_general(jnp.ones((8, l), MXU_DTYPE), mb, NT_DIMS, preferred_element_type=F32)
    cumrow = jnp.dot(rowtot_lane.astype(MXU_DTYPE), _tri(r, r, "le"), preferred_element_type=F32)[0:1]
    slot = lax.broadcasted_iota(jnp.int32, (cap, r), 0).astype(F32)
    passed = jnp.where(cumrow <= slot, 1.0, 0.0).astype(MXU_DTYPE)
    row_of = jnp.dot(passed, jnp.ones((r, l), MXU_DTYPE), preferred_element_type=F32)[:, 0:1]
    base_of = jnp.dot(passed, rowtot.astype(MXU_DTYPE), preferred_element_type=F32)[:, 0:1]
    onehot = jnp.where(lax.broadcasted_iota(jnp.int32, (cap, r), 1).astype(F32) == row_of, 1.0, 0.0)
    linc_of = jnp.dot(onehot.astype(MXU_DTYPE), linc.astype(MXU_DTYPE), preferred_element_type=F32)
    k = slot[:, 0:1] - base_of
    col_of = jnp.sum(jnp.where(linc_of <= k, 1.0, 0.0), axis=-1, keepdims=True)
    idx_ref[0, 0] = (row_of * l + col_of).astype(jnp.int32)
    at_col = lax.broadcasted_iota(jnp.int32, (cap, l), 1).astype(F32) == col_of
    aff_rows = jnp.dot(onehot, aff_ref[0, 0], precision=HI, preferred_element_type=F32)
    gate_ref[0, 0] = jnp.sum(jnp.where(at_col, aff_rows, 0.0), axis=-1, keepdims=True)
    prank_rows = jnp.dot(onehot, prank_ref[0, 0].astype(F32), precision=HI, preferred_element_type=F32)
    dest_ref[0, 0] = jnp.sum(jnp.where(at_col, prank_rows, 0.0), axis=-1, keepdims=True).astype(jnp.int32)


def _slots(sel, aff4, prank, cap):
    b, ne, r, l = sel.shape
    blk = pl.BlockSpec((1, 1, r, l), lambda bi, e: (bi, e, 0, 0))
    oblk = pl.BlockSpec((1, 1, cap, 1), lambda bi, e: (bi, e, 0, 0))
    return pl.pallas_call(
        _slots_body,
        out_shape=[jax.ShapeDtypeStruct((b, ne, cap, 1), jnp.int32), jax.ShapeDtypeStruct((b, ne, cap, 1), jnp.int32),
                   jax.ShapeDtypeStruct((b, ne, cap, 1), F32)],
        grid=(b, ne),
        in_specs=[blk, blk, blk],
        out_specs=[oblk, oblk, oblk],
        compiler_params=_params("parallel", "parallel"),
        name="moe_slots",
    )(sel, aff4, prank)


def _ffn_body(idx_ref, dest_ref, h_hbm, gate_ref, wg_ref, wu_ref, wd_ref, z_hbm, xbuf, ybuf, sems,
              *, n_tok, n_pair):
    e, b, t = pl.program_id(0), pl.program_id(1), pl.program_id(2)
    ts = xbuf.shape[0]
    base = ((b * pl.num_programs(0) + e) * pl.num_programs(2) + t) * ts

    def gather(s):
        tok = idx_ref[base + s]
        return pltpu.make_async_copy(h_hbm.at[pl.ds(b * n_tok + tok, 1), :], xbuf.at[pl.ds(s, 1), :], sems.at[0])

    def scatter(s):
        pair = dest_ref[base + s]
        return pltpu.make_async_copy(ybuf.at[pl.ds(s, 1), :], z_hbm.at[pl.ds(b * n_pair + pair, 1), :], sems.at[1])

    def run(copy, op):
        def body(s, c):
            getattr(copy(s), op)()
            return c
        lax.fori_loop(0, ts, body, 0)

    run(gather, "start")
    run(gather, "wait")
    x = xbuf[...]
    g = _mm(x, wg_ref[0])
    u = _mm(x, wu_ref[0])
    ybuf[...] = _mm(_silu(g) * u, wd_ref[0]) * gate_ref[0, 0]
    run(scatter, "start")
    run(scatter, "wait")


def _expert_ffn(idx, dest, gate, h, lw, ts):
    b, ne, cap, _ = gate.shape
    n, d = h.shape[1:]
    f = lw["w_gate"].shape[-1]
    n_pair = ne * cap
    grid_spec = pltpu.PrefetchScalarGridSpec(
        num_scalar_prefetch=2,
        grid=(ne, b, cap // ts),
        in_specs=[pl.BlockSpec(memory_space=pl.ANY),
                  pl.BlockSpec((1, 1, ts, 1), lambda e, bi, t, *_: (bi, e, t, 0)),
                  pl.BlockSpec((1, d, f), lambda e, bi, t, *_: (e, 0, 0)),
                  pl.BlockSpec((1, d, f), lambda e, bi, t, *_: (e, 0, 0)),
                  pl.BlockSpec((1, f, d), lambda e, bi, t, *_: (e, 0, 0))],
        out_specs=pl.BlockSpec(memory_space=pl.ANY),
        scratch_shapes=[pltpu.VMEM((ts, d), F32), pltpu.VMEM((ts, d), F32), pltpu.SemaphoreType.DMA((2,))])
    return pl.pallas_call(
        functools.partial(_ffn_body, n_tok=n, n_pair=n_pair),
        out_shape=jax.ShapeDtypeStruct((b * n_pair, d), F32),
        grid_spec=grid_spec,
        compiler_params=_params("arbitrary", "arbitrary", "arbitrary"),
        name="moe_ffn",
    )(idx.reshape(-1), dest.reshape(-1), h.reshape(b * n, d), gate, lw["w_gate"], lw["w_up"], lw["w_down"])


def _combine_body(tb_ref, x_ref, g2_ref, ts_ref, te_ref, z_hbm, o_ref, zbuf, acc_ref, sem, *, n_pair):
    b, i = pl.program_id(0), pl.program_id(1)
    pc = zbuf.shape[0]
    tt = x_ref.shape[1]
    lo = tb_ref[b * (pl.num_programs(1) + 1) + i]
    hi = tb_ref[b * (pl.num_programs(1) + 1) + i + 1]
    p0 = (lo // 8) * 8
    start_row, end_row = ts_ref[0, 0], te_ref[0, 0]
    acc_ref[...] = jnp.zeros_like(acc_ref)

    def chunk(k, c):
        want = p0 + k * pc
        cs = pl.multiple_of(jnp.minimum(want, n_pair - pc), 8)
        cp = pltpu.make_async_copy(z_hbm.at[pl.ds(b * n_pair + cs, pc), :], zbuf, sem.at[0])
        cp.start()
        cp.wait()
        pair = cs + lax.broadcasted_iota(jnp.int32, (pc, tt), 0)
        own = jnp.where(pair >= jnp.maximum(start_row, want), jnp.where(pair < end_row, 1.0, 0.0), 0.0)
        own = own.astype(MXU_DTYPE)
        z = zbuf[...]
        zh = z.astype(MXU_DTYPE)
        zl = (z - zh.astype(F32)).astype(MXU_DTYPE)
        acc_ref[...] += (lax.dot_general(own, zh, TN_DIMS, preferred_element_type=F32)
                         + lax.dot_general(own, zl, TN_DIMS, preferred_element_type=F32))
        return c

    lax.fori_loop(0, (hi - p0 + pc - 1) // pc, chunk, 0)
    o_ref[0] = x_ref[0] + g2_ref[0] * acc_ref[...]


def _combine(x, g2, tstart, tend, z, n_pair, tt, pc):
    b, n, d = x.shape
    nt = n // tt
    ts4 = tstart.reshape(b, nt, 1, tt)
    te4 = tend.reshape(b, nt, 1, tt)
    bounds = jnp.concatenate([ts4[:, :, 0, 0], jnp.full((b, 1), n_pair, jnp.int32)], axis=1).reshape(-1)
    grid_spec = pltpu.PrefetchScalarGridSpec(
        num_scalar_prefetch=1,
        grid=(b, nt),
        in_specs=[pl.BlockSpec((1, tt, d), lambda bi, i, *_: (bi, i, 0)),
                  pl.BlockSpec((1, 1, d), lambda bi, i, *_: (bi, 0, 0)),
                  pl.BlockSpec((1, 1, 1, tt), lambda bi, i, *_: (bi, i, 0, 0)),
                  pl.BlockSpec((1, 1, 1, tt), lambda bi, i, *_: (bi, i, 0, 0)),
                  pl.BlockSpec(memory_space=pl.ANY)],
        out_specs=pl.BlockSpec((1, tt, d), lambda bi, i, *_: (bi, i, 0)),
        scratch_shapes=[pltpu.VMEM((pc, d), F32), pltpu.VMEM((tt, d), F32), pltpu.SemaphoreType.DMA((1,))])
    return pl.pallas_call(
        functools.partial(_combine_body, n_pair=n_pair),
        out_shape=jax.ShapeDtypeStruct((b, n, d), F32),
        grid_spec=grid_spec,
        compiler_params=_params("parallel", "parallel"),
        name="moe_combine",
    )(bounds, x, g2, ts4, te4, z)


def _moe_latent(x, h, aff, g2, lw):
    b, n, d = x.shape
    ne = aff.shape[1]
    cap = EC_CAPACITY_FACTOR * n // ne
    aff4 = aff.reshape(b, ne, n // LANES, LANES)
    sel, prank, tstart, tend = _select(aff4, cap)
    idx, dest, gate = _slots(sel, aff4, prank, cap)
    z = _expert_ffn(idx, dest, gate, h, lw, min(cap, 256))
    return _combine(x, g2, tstart, tend, z, ne * cap, 256, 256)


def _ctx_coef_body(aff_ref, coef_ref, *, cap):
    nb, _, nc = aff_ref.shape
    excl = _tri(nc, nc, "lt")
    outs = []
    for b in range(nb):
        aff = aff_ref[b]
        sel = _select_mask(aff, cap, lambda mm: jnp.dot(mm.astype(MXU_DTYPE), excl, preferred_element_type=F32), (1,))
        outs.append(sel * aff)
    coef_ref[:, 0, :] = jnp.concatenate(outs, axis=-1)


def _ctx_coef(aff, cap):
    b, ne, nc = aff.shape
    return pl.pallas_call(
        functools.partial(_ctx_coef_body, cap=cap),
        out_shape=jax.ShapeDtypeStruct((ne, 1, b * nc), F32),
        name="ctx_moe_select",
    )(aff)


def _ctx_ffn_body(h_ref, coef_ref, wg_ref, wu_ref, wd_ref, x_ref, g2_ref, o_ref):
    e = pl.program_id(0)
    rows = h_ref.shape[0]
    h = h_ref[...]
    y = _mm(_silu(_mm(h, wg_ref[0])) * _mm(h, wu_ref[0]), wd_ref[0])
    diag = (lax.broadcasted_iota(jnp.int32, (rows, rows), 0) == lax.broadcasted_iota(jnp.int32, (rows, rows), 1))
    coef = jnp.sum(jnp.where(diag, coef_ref[0], 0.0), axis=-1, keepdims=True)

    @pl.when(e == 0)
    def _():
        o_ref[...] = x_ref[...]

    o_ref[...] += g2_ref[...] * (coef * y)


def _ctx_moe(xc, hc, aff, g2, lw):
    b, nc, d = xc.shape
    ne = aff.shape[1]
    f = lw["w_gate"].shape[-1]
    coef = _ctx_coef(aff, EC_CAPACITY_FACTOR * nc // ne)
    rows = b * nc
    out = pl.pallas_call(
        _ctx_ffn_body,
        out_shape=jax.ShapeDtypeStruct((rows, d), F32),
        grid=(ne,),
        in_specs=[pl.BlockSpec((rows, d), lambda e: (0, 0)),
                  pl.BlockSpec((1, 1, rows), lambda e: (e, 0, 0)),
                  pl.BlockSpec((1, d, f), lambda e: (e, 0, 0)),
                  pl.BlockSpec((1, d, f), lambda e: (e, 0, 0)),
                  pl.BlockSpec((1, f, d), lambda e: (e, 0, 0)),
                  pl.BlockSpec((rows, d), lambda e: (0, 0)),
                  pl.BlockSpec((1, d), lambda e: (0, 0))],
        out_specs=pl.BlockSpec((rows, d), lambda e: (0, 0)),
        compiler_params=_params("arbitrary"),
        name="ctx_moe_ffn",
    )(hc.reshape(rows, d), coef, lw["w_gate"], lw["w_up"], lw["w_down"], xc.reshape(rows, d), g2)
    return out.reshape(b, nc, d)


def _final_body(x_ref, g_ref, o_ref):
    o_ref[0] = _rms(x_ref[0], g_ref[...])


def _final_norm(x, g, tm):
    b, n, d = x.shape
    return pl.pallas_call(
        _final_body,
        out_shape=jax.ShapeDtypeStruct((b, n, d), F32),
        grid=(b, n // tm),
        in_specs=[pl.BlockSpec((1, tm, d), lambda bi, i: (bi, i, 0)), pl.BlockSpec((1, d), lambda bi, i: (0, 0))],
        out_specs=pl.BlockSpec((1, tm, d), lambda bi, i: (bi, i, 0)),
        compiler_params=_params("parallel", "parallel"),
        name="final_norm",
    )(x, g)


def _rope_tables(n):
    t = np.arange(n)
    freqs = ROPE_THETA ** (-np.arange(ROPE_FREQS, dtype=np.float32) / ROPE_FREQS)
    ang_r = (t // GRID_W).astype(np.float32)[:, None] * freqs
    ang_c = (t % GRID_W).astype(np.float32)[:, None] * freqs
    cos = np.concatenate([np.cos(ang_r)] * 2 + [np.cos(ang_c)] * 2, axis=1)
    sin = np.concatenate([-np.sin(ang_r), np.sin(ang_r), -np.sin(ang_c), np.sin(ang_c)], axis=1)
    reps = LANES // HEAD_DIM
    return (jnp.asarray(np.tile(cos, (1, reps)), F32), jnp.asarray(np.tile(sin, (1, reps)), F32))


def _block_geometry(n):
    nblk = n // ATT_BLOCK
    a = np.arange(ATT_BLOCK)
    out = []
    for i in (0, 1, nblk - 1):
        j0 = int(np.clip(i - 1, 0, nblk - 3))
        out.append((i * ATT_BLOCK + a, [(j0 + j) * ATT_BLOCK + a for j in range(3)]))
    return out


def _neighbourhood_bias(rpb, n):
    rows = n // GRID_W
    win_r = min(NA_WIN_R, rows)
    kinds = []
    for q_tok, k_chunks in _block_geometry(n):
        qr, qc = q_tok // GRID_W, q_tok % GRID_W
        r0 = np.clip(qr - win_r // 2, 0, rows - win_r)
        c0 = np.clip(qc - NA_WIN_C // 2, 0, GRID_W - NA_WIN_C)
        chunks = []
        for k_tok in k_chunks:
            kr, kc = k_tok // GRID_W, k_tok % GRID_W
            ok = ((kr[None] >= r0[:, None]) & (kr[None] < r0[:, None] + win_r)
                  & (kc[None] >= c0[:, None]) & (kc[None] < c0[:, None] + NA_WIN_C))
            ro = np.clip(kr[None] - qr[:, None] + NA_WIN_R - 1, 0, 2 * NA_WIN_R - 2)
            co = np.clip(kc[None] - qc[:, None] + NA_WIN_C - 1, 0, 2 * NA_WIN_C - 2)
            chunks.append(jnp.where(ok[None], rpb[:, ro, co], NEG_INF))
        kinds.append(jnp.stack(chunks, axis=1))
    return jnp.stack(kinds, axis=0)


def _window_mask(n):
    kinds = []
    for q_tok, k_chunks in _block_geometry(n):
        kinds.append(np.stack([np.where(np.abs(k_tok[None] - q_tok[:, None]) <= WINDOW, 0.0, NEG_INF)
                               for k_tok in k_chunks])[None])
    return jnp.asarray(np.stack(kinds), F32)


def _token_mixer(h_parts, c_parts):
    qa, ka, va, qb, kb, vb, qs, ks, vs, yd = h_parts
    _, ka_c, va_c, _, kb_c, vb_c, _, ks_c, vs_c, _ = c_parts
    return qa, ka, va, qb, kb, vb, qs, ks, vs, yd, ka_c, va_c, kb_c, vb_c, ks_c, vs_c


def kernel(x, c, ctx, c_ctx, w_mod, b_mod, norm_mix, norm_ffn, w_in, rpb, q_norm, k_norm, sink, sgu_norm, w_sgu,
           b_sgu, out_norm, w_out, w_router, w_gate, w_up, w_down, final_norm):
    depth = w_mod.shape[0]
    b, n, d = x.shape
    nc = ctx.shape[1]
    group_w = MIXER_WIDTH // SG_GROUPS

    cvecs = jnp.concatenate([c, c_ctx[None], jnp.zeros((8 - b - 1, d), F32)], axis=0)
    mods = _adaln_all(cvecs, w_mod, b_mod).reshape(depth, 8, 6, d)

    cos, sin = _rope_tables(n)
    cos_c, sin_c = jnp.ones((nc, LANES), F32), jnp.zeros((nc, LANES), F32)
    win_mask = _window_mask(n)
    blk = np.arange(256) // HEAD_DIM
    gsum = jnp.asarray(blk[:, None] == blk[None, :], MXU_DTYPE)

    xc = ctx
    for l in range(depth):
        ctx_needed = l < depth - 1
        lat = [mods[l, :b, j][:, None, :] for j in range(6)]
        cm = [jnp.broadcast_to(mods[l, b, j][None, None, :], (b, 1, d)) for j in range(6)]
        lw = {
            "w_in": w_in[l].astype(MXU_DTYPE),
            "qn": jnp.tile(q_norm[l], 4)[None], "kn": jnp.tile(k_norm[l], 2)[None], "gn": sgu_norm[l][None],
            "w_sgu": w_sgu[l].astype(MXU_DTYPE),
            "b_sgu": jnp.repeat(b_sgu[l].T, group_w, axis=1),
            "gsum": gsum,
            "out_norm": out_norm[l][None], "w_out": w_out[l].astype(MXU_DTYPE), "norm_ffn": norm_ffn[l][None],
            "w_router_t": w_router[l].T,
            "w_gate": w_gate[l].astype(MXU_DTYPE), "w_up": w_up[l].astype(MXU_DTYPE),
            "w_down": w_down[l].astype(MXU_DTYPE),
        }
        nm = norm_mix[l][None]
        hp = _inproj(x, nm, lat[0], lat[1], lw, cos, sin, 512)
        cp = _inproj(xc, nm, cm[0], cm[1], lw, cos_c, sin_c, nc)
        qa, ka, va, qb, kb, vb, qs, ks, vs, yd, ka_c, va_c, kb_c, vb_c, ks_c, vs_c = _token_mixer(hp, cp)

        ya = _local_attn(qa, ka, va, ka_c, va_c, _neighbourhood_bias(rpb[l], n))
        yb = _global_attn(qb, jnp.concatenate([kb, kb_c], axis=1), jnp.concatenate([vb, vb_c], axis=1))
        yc = _local_attn(qs, ks, vs, ks_c, vs_c, win_mask, sink[l])
        x_mid, h2, aff = _merge((ya, yb, yc, yd), x, lw, lat[2], lat[3], lat[4], 512)
        x = _moe_latent(x_mid, h2, aff, lat[5], lw)

        if ctx_needed:
            ys_c = _ctx_attn(sink[l], cp[:9])
            xc_mid, hc2, aff_c = _merge((*ys_c, cp[9]), xc, lw, cm[2], cm[3], cm[4], nc)
            xc = _ctx_moe(xc_mid, hc2, aff_c, mods[l, b, 5][None], lw)
    return _final_norm(x, final_norm[None], 512)
```

```python
import functools

import numpy as np
import jax
import jax.numpy as jnp
from jax import lax
from jax.experimental import pallas as pl
from jax.experimental.pallas import tpu as pltpu

HEAD_DIM = 64
GRID_W = 64
MIXER_WIDTH = 256
NA_WIN_R = 8
NA_WIN_C = 16
WINDOW = 128
CHUNK = 128
SG_GROUPS = 4
EC_CAPACITY_FACTOR = 2
ROPE_THETA = 10000.0
ROPE_FREQS = HEAD_DIM // 4
EPS = 1e-6
NEG_INF = -1e30
PROJ_SIZES = (256, 256, 256, 256, 128, 128, 256, 128, 128, 256, 256)
PROJ_OFFS = tuple(int(v) for v in np.cumsum((0,) + PROJ_SIZES))
LOG2E = 1.4426950408889634
Q_SCALE = HEAD_DIM ** -0.5 * LOG2E

LANES = 128
VMEM_LIMIT = 56 * 2 ** 20
ATT_BLOCK = 256
MXU_DTYPE = jnp.bfloat16
ACT_DTYPE = jnp.bfloat16
F32 = jnp.float32
HI = lax.Precision.HIGHEST
NT_DIMS = (((1,), (1,)), ((), ()))
TN_DIMS = (((0,), (0,)), ((), ()))


def _params(*sem):
    return pltpu.CompilerParams(dimension_semantics=sem, vmem_limit_bytes=VMEM_LIMIT)


def _mm(a, b):
    return jnp.dot(a.astype(MXU_DTYPE), b.astype(MXU_DTYPE), preferred_element_type=F32)


def _rms(x, g):
    return x * lax.rsqrt(jnp.mean(x * x, axis=-1, keepdims=True) + EPS) * g


def _silu(x):
    return x / (1.0 + jnp.exp(-x))


def _mod_body(c_ref, w_ref, b_ref, o_ref):
    s = _silu(c_ref[...])
    o_ref[0] = jnp.dot(s, w_ref[0], precision=HI, preferred_element_type=F32) + b_ref[0]


def _adaln_all(cvecs, w_mod, b_mod):
    depth, d, d6 = w_mod.shape
    tn = 1536
    rows = cvecs.shape[0]
    return pl.pallas_call(
        _mod_body,
        out_shape=jax.ShapeDtypeStruct((depth, rows, d6), F32),
        grid=(depth, d6 // tn),
        in_specs=[pl.BlockSpec((rows, d), lambda l, j: (0, 0)),
                  pl.BlockSpec((1, d, tn), lambda l, j: (l, 0, j)),
                  pl.BlockSpec((1, 1, tn), lambda l, j: (l, 0, j))],
        out_specs=pl.BlockSpec((1, rows, tn), lambda l, j: (l, 0, j)),
        compiler_params=_params("parallel", "parallel"),
        name="adaln",
    )(cvecs, w_mod, b_mod.reshape(depth, 1, d6))


def _head_rms(t, g, gsum_ref):
    w = t.shape[-1]
    sq = t * t
    hi = sq.astype(MXU_DTYPE)
    lo = (sq - hi.astype(F32)).astype(MXU_DTYPE)
    gs = gsum_ref[0:w, 0:w]
    ss = jnp.dot(hi, gs, preferred_element_type=F32) + jnp.dot(lo, gs, preferred_element_type=F32)
    return t * lax.rsqrt(ss * (1.0 / HEAD_DIM) + EPS) * g


def _rope(t, cos, sin_signed):
    w = t.shape[-1]
    rep = w // LANES
    if rep > 1:
        cos = jnp.concatenate([cos] * rep, axis=-1)
        sin_signed = jnp.concatenate([sin_signed] * rep, axis=-1)
    lane = lax.broadcasted_iota(jnp.int32, t.shape, 1)
    first_half = (lane % (2 * ROPE_FREQS)) < ROPE_FREQS
    partner = jnp.where(first_half, pltpu.roll(t, w - ROPE_FREQS, 1), pltpu.roll(t, ROPE_FREQS, 1))
    return t * cos + partner * sin_signed


def _inproj_body(x_ref, nw_ref, sh_ref, sc_ref, w_ref, cos_ref, sin_ref, qn_ref, kn_ref, gn_ref,
                 ws_ref, bs_ref, gsum_ref,
                 qa_ref, ka_ref, va_ref, qb_ref, kb_ref, vb_ref, qs_ref, ks_ref, vs_ref, yd_ref):
    x = x_ref[0]
    h = _rms(x, nw_ref[...]) * (1.0 + sc_ref[0]) + sh_ref[0]
    p = _mm(h, w_ref[...])
    o = PROJ_OFFS
    cos, sin = cos_ref[...], sin_ref[...]
    dt = qa_ref.dtype
    qa_ref[0] = (p[:, o[0]:o[1]] * Q_SCALE).astype(dt)
    ka_ref[0] = p[:, o[1]:o[2]].astype(dt)
    va_ref[0] = p[:, o[2]:o[3]].astype(dt)
    qb = _rope(_head_rms(p[:, o[3]:o[4]], qn_ref[...], gsum_ref), cos, sin)
    qb_ref[0] = (qb * Q_SCALE).astype(dt)
    kb_ref[0] = _rope(_head_rms(p[:, o[4]:o[5]], kn_ref[...], gsum_ref), cos, sin).astype(dt)
    vb_ref[0] = p[:, o[5]:o[6]].astype(dt)
    qs_ref[0] = (_rope(p[:, o[6]:o[7]], cos, sin) * Q_SCALE).astype(dt)
    ks_ref[0] = _rope(p[:, o[7]:o[8]], cos, sin).astype(dt)
    vs_ref[0] = p[:, o[8]:o[9]].astype(dt)
    u = jax.nn.gelu(p[:, o[9]:o[10]])
    v = _rms(jax.nn.gelu(p[:, o[10]:o[11]]), gn_ref[...]).astype(MXU_DTYPE)
    lane_group = lax.broadcasted_iota(jnp.int32, (CHUNK, MIXER_WIDTH), 1) // (MIXER_WIDTH // SG_GROUPS)
    for c in range(x.shape[0] // CHUNK):
        rows = slice(c * CHUNK, (c + 1) * CHUNK)
        mixed = bs_ref[...]
        for g in range(SG_GROUPS):
            mg = jnp.dot(ws_ref[g], v[rows], preferred_element_type=F32)
            mixed = mixed + jnp.where(lane_group == g, mg, 0.0)
        yd_ref[0, rows, :] = (u[rows] * mixed).astype(dt)


def _inproj(x, nw, shift, scale, lw, cos, sin, tm):
    b, t, d = x.shape
    widths = PROJ_SIZES[:9] + (MIXER_WIDTH,)
    row = lambda bi, i: (0, 0)
    per_b = lambda bi, i: (bi, 0, 0)
    tile = lambda bi, i: (bi, i, 0)
    return pl.pallas_call(
        _inproj_body,
        out_shape=[jax.ShapeDtypeStruct((b, t, w), ACT_DTYPE) for w in widths],
        grid=(b, t // tm),
        in_specs=[pl.BlockSpec((1, tm, d), tile),
                  pl.BlockSpec((1, d), row),
                  pl.BlockSpec((1, 1, d), per_b),
                  pl.BlockSpec((1, 1, d), per_b),
                  pl.BlockSpec(lw["w_in"].shape, row),
                  pl.BlockSpec((tm, LANES), lambda bi, i: (i, 0)),
                  pl.BlockSpec((tm, LANES), lambda bi, i: (i, 0)),
                  pl.BlockSpec((1, 256), row),
                  pl.BlockSpec((1, 128), row),
                  pl.BlockSpec((1, 256), row),
                  pl.BlockSpec((SG_GROUPS, CHUNK, CHUNK), lambda bi, i: (0, 0, 0)),
                  pl.BlockSpec((CHUNK, MIXER_WIDTH), row),
                  pl.BlockSpec((256, 256), row)],
        out_specs=[pl.BlockSpec((1, tm, w), tile) for w in widths],
        compiler_params=_params("parallel", "parallel"),
        name="inproj",
    )(x, nw, shift, scale, lw["w_in"], cos, sin, lw["qn"], lw["kn"], lw["gn"], lw["w_sgu"],
      lw["b_sgu"], lw["gsum"])


def _attend(q, chunks, sink=None):
    scores = []
    for k, _, bias in chunks:
        s = lax.dot_general(q, k, NT_DIMS, preferred_element_type=F32)
        scores.append(s if bias is None else s + bias)
    m = functools.reduce(jnp.maximum, [jnp.max(s, axis=-1, keepdims=True) for s in scores])
    if sink is not None:
        m = jnp.maximum(m, sink)
    l = jnp.zeros_like(m)
    o = jnp.zeros((q.shape[0], HEAD_DIM), F32)
    for s, (_, v, _) in zip(scores, chunks):
        p = jnp.exp2(s - m)
        l = l + jnp.sum(p, axis=-1, keepdims=True)
        o = o + jnp.dot(p.astype(v.dtype), v, preferred_element_type=F32)
    if sink is not None:
        l = l + jnp.exp2(sink - m)
    return o / l


def _head(ref, h):
    return ref[0, :, h * HEAD_DIM:(h + 1) * HEAD_DIM]


def _local_attn_body(*refs, group, has_sink):
    if has_sink:
        sink_ref, refs = refs[0], refs[1:]
    q_ref, k0, k1, k2, v0, v1, v2, kc_ref, vc_ref, bias_ref, o_ref = refs
    outs = []
    for h in range(q_ref.shape[-1] // HEAD_DIM):
        kv = h // group
        hb = h if bias_ref.shape[1] > 1 else 0
        chunks = [(_head(kr, kv), _head(vr, kv), bias_ref[0, hb, j])
                  for j, (kr, vr) in enumerate(((k0, v0), (k1, v1), (k2, v2)))]
        chunks.append((_head(kc_ref, kv), _head(vc_ref, kv), None))
        outs.append(_attend(_head(q_ref, h), chunks, sink_ref[h] if has_sink else None))
    o_ref[0] = jnp.concatenate(outs, axis=-1).astype(o_ref.dtype)


def _local_attn(q, k, v, kc, vc, bias, sink=None):
    b, n, qw = q.shape
    kw = k.shape[-1]
    nc = kc.shape[1]
    tq = ATT_BLOCK
    nblk = n // tq
    assert nblk >= 4
    group = qw // kw
    hb = bias.shape[1]

    def kmap(j):
        return lambda bi, i: (bi, jnp.clip(i - 1, 0, nblk - 3) + j, 0)

    def bmap(bi, i):
        return (jnp.where(i == 0, 0, jnp.where(i == nblk - 1, 2, 1)), 0, 0, 0, 0)

    in_specs = [pl.BlockSpec((1, tq, qw), lambda bi, i: (bi, i, 0))]
    in_specs += [pl.BlockSpec((1, tq, kw), kmap(j)) for j in range(3)] * 2
    in_specs += [pl.BlockSpec((1, nc, kw), lambda bi, i: (bi, 0, 0))] * 2
    in_specs += [pl.BlockSpec((1, hb, 3, tq, tq), bmap)]
    args = [q, k, k, k, v, v, v, kc, vc, bias]
    if sink is not None:
        in_specs = [pl.BlockSpec(memory_space=pltpu.SMEM)] + in_specs
        args = [sink] + args
    return pl.pallas_call(
        functools.partial(_local_attn_body, group=group, has_sink=sink is not None),
        out_shape=jax.ShapeDtypeStruct((b, n, qw), ACT_DTYPE),
        grid=(b, nblk),
        in_specs=in_specs,
        out_specs=pl.BlockSpec((1, tq, qw), lambda bi, i: (bi, i, 0)),
        compiler_params=_params("parallel", "parallel"),
        name="local_attn",
    )(*args)


def _ctx_attn_body(sink_ref, qa, ka, va, qb, kb, vb, qs, ks, vs, oa, ob, oc):
    for q_ref, k_ref, v_ref, o_ref, group, use_sink in (
            (qa, ka, va, oa, 1, False), (qb, kb, vb, ob, 2, False), (qs, ks, vs, oc, 2, True)):
        outs = []
        for h in range(q_ref.shape[-1] // HEAD_DIM):
            kv = h // group
            outs.append(_attend(_head(q_ref, h), [(_head(k_ref, kv), _head(v_ref, kv), None)],
                                sink_ref[h] if use_sink else None))
        o_ref[0] = jnp.concatenate(outs, axis=-1).astype(o_ref.dtype)


def _ctx_attn(sink, qkv):
    b, nc, _ = qkv[0].shape
    spec = lambda a: pl.BlockSpec((1, nc, a.shape[-1]), lambda bi: (bi, 0, 0))
    return pl.pallas_call(
        _ctx_attn_body,
        out_shape=[jax.ShapeDtypeStruct((b, nc, MIXER_WIDTH), ACT_DTYPE)] * 3,
        grid=(b,),
        in_specs=[pl.BlockSpec(memory_space=pltpu.SMEM)] + [spec(a) for a in qkv],
        out_specs=[pl.BlockSpec((1, nc, MIXER_WIDTH), lambda bi: (bi, 0, 0))] * 3,
        compiler_params=_params("parallel"),
        name="ctx_attn",
    )(sink, *qkv)


def _global_attn_body(q_ref, kt_ref, v_ref, o_ref, s_a, s_b, *, tk):
    tq = q_ref.shape[1]
    n_kv = kt_ref.shape[1] // HEAD_DIM
    n_chunks = kt_ref.shape[2] // tk
    group_w = 2 * HEAD_DIM
    qs = [jnp.concatenate([q_ref[0, :, kv * group_w:kv * group_w + HEAD_DIM],
                           q_ref[0, :, kv * group_w + HEAD_DIM:(kv + 1) * group_w]], axis=0)
          for kv in range(n_kv)]

    def scores(i, s_ref):
        ks = pl.multiple_of(i * tk, tk)
        for kv in range(n_kv):
            s_ref[kv] = jnp.dot(qs[kv], kt_ref[0, kv * HEAD_DIM:(kv + 1) * HEAD_DIM, pl.ds(ks, tk)],
                                preferred_element_type=F32)

    def update(i, s_ref, carry):
        ks = pl.multiple_of(i * tk, tk)
        out = []
        for kv in range(n_kv):
            m, acc = carry[kv]
            s = s_ref[kv]
            m_new = jnp.maximum(m, jnp.max(s, axis=-1, keepdims=True))
            p = jnp.exp2(s - m_new).astype(v_ref.dtype)
            pv = jnp.dot(p, v_ref[0, pl.ds(ks, tk), kv * LANES:(kv + 1) * LANES], preferred_element_type=F32)
            out.append((m_new, jnp.exp2(m - m_new) * acc + pv))
        return tuple(out)

    carry = tuple((jnp.full((2 * tq, 1), NEG_INF, F32), jnp.zeros((2 * tq, LANES), F32)) for _ in range(n_kv))
    scores(0, s_a)

    def pair(j, carry):
        scores(2 * j + 1, s_b)
        carry = update(2 * j, s_a, carry)
        scores(2 * j + 2, s_a)
        return update(2 * j + 1, s_b, carry)

    carry = lax.fori_loop(0, (n_chunks - 1) // 2, pair, carry)
    if n_chunks % 2 == 0:
        scores(n_chunks - 1, s_b)
        carry = update(n_chunks - 2, s_a, carry)
        carry = update(n_chunks - 1, s_b, carry)
    else:
        carry = update(n_chunks - 1, s_a, carry)
    outs = []
    for _, acc in carry:
        o = acc[:, :HEAD_DIM] / acc[:, HEAD_DIM:HEAD_DIM + 1]
        outs += [o[:tq], o[tq:]]
    o_ref[0] = jnp.concatenate(outs, axis=-1).astype(o_ref.dtype)


def _global_attn(q, k, v):
    b, n, qw = q.shape
    nk, kw = k.shape[1:]
    n_kv = kw // HEAD_DIM
    tq = ATT_BLOCK
    tk = 1280 if (nk % 1280 == 0 and nk > 1280) else 256
    kt = jnp.swapaxes(k, 1, 2)
    ones = jnp.ones((b, nk, 1), v.dtype)
    zeros = jnp.zeros((b, nk, LANES - HEAD_DIM - 1), v.dtype)
    v_aug = jnp.concatenate(
        [part for kv in range(n_kv) for part in (v[:, :, kv * HEAD_DIM:(kv + 1) * HEAD_DIM], ones, zeros)], axis=-1)
    return pl.pallas_call(
        functools.partial(_global_attn_body, tk=tk),
        out_shape=jax.ShapeDtypeStruct((b, n, qw), ACT_DTYPE),
        grid=(b, n // tq),
        in_specs=[pl.BlockSpec((1, tq, qw), lambda bi, i: (bi, i, 0)),
                  pl.BlockSpec((1, kw, nk), lambda bi, i: (bi, 0, 0)),
                  pl.BlockSpec((1, nk, n_kv * LANES), lambda bi, i: (bi, 0, 0))],
        out_specs=pl.BlockSpec((1, tq, qw), lambda bi, i: (bi, i, 0)),
        scratch_shapes=[pltpu.VMEM((n_kv, 2 * tq, tk), F32), pltpu.VMEM((n_kv, 2 * tq, tk), F32)],
        compiler_params=_params("parallel", "parallel"),
        name="global_attn",
    )(q, kt, v_aug)


def _merge_body(ya, yb, yc, yd, x_ref, on_ref, wo_ref, g1_ref, nf_ref, sh_ref, sc_ref, wr_ref,
                xo_ref, h_ref, aff_ref):
    parts = []
    for j, r in enumerate((ya, yb, yc, yd)):
        y = r[0].astype(F32)
        parts.append(_rms(y, on_ref[:, j * MIXER_WIDTH:(j + 1) * MIXER_WIDTH]).astype(MXU_DTYPE))
    xn = x_ref[0] + g1_ref[0] * _mm(jnp.concatenate(parts, axis=-1), wo_ref[...])
    xo_ref[0] = xn
    h = _rms(xn, nf_ref[...]) * (1.0 + sc_ref[0]) + sh_ref[0]
    h_ref[0] = h
    logits = lax.dot_general(wr_ref[...], h, NT_DIMS, precision=HI, preferred_element_type=F32)
    e = jnp.exp(logits - jnp.max(logits, axis=0, keepdims=True))
    aff_ref[0] = e / jnp.sum(e, axis=0, keepdims=True)


def _merge(ys, x, lw, g1, shift, scale, tm):
    b, t, d = x.shape
    ne = lw["w_router_t"].shape[0]
    row = lambda bi, i: (0, 0)
    per_b = lambda bi, i: (bi, 0, 0)
    tile = lambda bi, i: (bi, i, 0)
    return pl.pallas_call(
        _merge_body,
        out_shape=[jax.ShapeDtypeStruct((b, t, d), F32), jax.ShapeDtypeStruct((b, t, d), F32),
                   jax.ShapeDtypeStruct((b, ne, t), F32)],
        grid=(b, t // tm),
        in_specs=[pl.BlockSpec((1, tm, MIXER_WIDTH), tile)] * 4 + [
            pl.BlockSpec((1, tm, d), tile),
            pl.BlockSpec((1, d), row),
            pl.BlockSpec((d, d), row),
            pl.BlockSpec((1, 1, d), per_b),
            pl.BlockSpec((1, d), row),
            pl.BlockSpec((1, 1, d), per_b),
            pl.BlockSpec((1, 1, d), per_b),
            pl.BlockSpec((ne, d), row)],
        out_specs=[pl.BlockSpec((1, tm, d), tile), pl.BlockSpec((1, tm, d), tile),
                   pl.BlockSpec((1, ne, tm), lambda bi, i: (bi, 0, i))],
        compiler_params=_params("parallel", "parallel"),
        name="merge",
    )(*ys, x, lw["out_norm"], lw["w_out"], g1, lw["norm_ffn"], shift, scale, lw["w_router_t"])


def _tri(n, m, mode):
    r = lax.broadcasted_iota(jnp.int32, (n, m), 0)
    c = lax.broadcasted_iota(jnp.int32, (n, m), 1)
    return jnp.where({"lt": r < c, "le": r <= c, "gt": r > c}[mode], 1.0, 0.0).astype(MXU_DTYPE)


def _count(mask, axes):
    out = jnp.where(mask, 1.0, 0.0)
    for ax in sorted(axes, reverse=True):
        out = jnp.sum(out, axis=ax, keepdims=True)
    return out


def _kth_largest_bits(bits, cap, axes):
    shape = tuple(1 if a in axes else s for a, s in enumerate(bits.shape))

    def body(i, t):
        cand = t | lax.shift_left(jnp.int32(1), 30 - i)
        return jnp.where(_count(bits >= cand, axes) >= cap, cand, t)

    return lax.fori_loop(0, 31, body, jnp.zeros(shape, jnp.int32))


def _prefix_tokens(m, exact_rows):
    e, r, l = m.shape
    m2 = m.reshape(e * r, l).astype(MXU_DTYPE)
    local = jnp.dot(m2, _tri(l, l, "lt"), preferred_element_type=F32)
    rowtot = jnp.dot(m2, jnp.ones((l, l), MXU_DTYPE), preferred_element_type=F32).reshape(e, r, l)
    below = _tri(r, r, "gt")
    if exact_rows:
        base = [jnp.dot(below, rowtot[i].astype(MXU_DTYPE), preferred_element_type=F32) for i in range(e)]
    else:
        base = [jnp.dot(below.astype(F32), rowtot[i], precision=HI, preferred_element_type=F32)
                for i in range(e)]
    return local.reshape(e, r, l) + jnp.stack(base, axis=0)


def _select_mask(aff, cap, prefix_fn, axes):
    bits = pltpu.bitcast(aff, jnp.int32)
    thr = _kth_largest_bits(bits, cap, axes)
    gt = bits > thr
    eq = bits == thr
    need = cap - _count(gt, axes)
    eq_rank = prefix_fn(jnp.where(eq, 1.0, 0.0))
    take_eq = jnp.where(eq, jnp.where(eq_rank < need, 1.0, 0.0), 0.0)
    return jnp.where(gt, 1.0, take_eq)


def _select_body(aff_ref, sel_ref, prank_ref, tstart_ref, tend_ref, *, cap):
    aff = aff_ref[0]
    ne = aff.shape[0]
    sel = _select_mask(aff, cap, functools.partial(_prefix_tokens, exact_rows=True), (1, 2))
    sel_ref[0] = sel
    cnt = jnp.sum(sel, axis=0)
    tstart = _prefix_tokens(cnt[None], exact_rows=False)[0]
    tstart_ref[0] = tstart.astype(jnp.int32)
    tend_ref[0] = (tstart + cnt).astype(jnp.int32)
    run = tstart
    for e in range(ne):
        prank_ref[0, e] = run.astype(jnp.int32)
        run = run + sel[e]


def _select(aff4, cap):
    b, ne, r, l = aff4.shape
    blk4 = pl.BlockSpec((1, ne, r, l), lambda bi: (bi, 0, 0, 0))
    blk3 = pl.BlockSpec((1, r, l), lambda bi: (bi, 0, 0))
    return pl.pallas_call(
        functools.partial(_select_body, cap=cap),
        out_shape=[jax.ShapeDtypeStruct((b, ne, r, l), F32), jax.ShapeDtypeStruct((b, ne, r, l), jnp.int32),
                   jax.ShapeDtypeStruct((b, r, l), jnp.int32), jax.ShapeDtypeStruct((b, r, l), jnp.int32)],
        grid=(b,),
        in_specs=[blk4],
        out_specs=[blk4, blk4, blk3, blk3],
        compiler_params=_params("parallel"),
        name="moe_select",
    )(aff4)


def _slots_body(sel_ref, aff_ref, prank_ref, idx_ref, dest_ref, gate_ref):
    m = sel_ref[0, 0]
    r, l = m.shape
    cap = idx_ref.shape[2]
    mb = m.astype(MXU_DTYPE)
    linc = jnp.dot(mb, _tri(l, l, "le"), preferred_element_type=F32)
    rowtot = jnp.dot(mb, jnp.ones((l, l), MXU_DTYPE), preferred_element_type=F32)
    rowtot_lane = lax.dot_general(jnp.ones((8, l), MXU_DTYPE), mb, NT_DIMS, preferred_element_type=F32)
    cumrow = jnp.dot(rowtot_lane.astype(MXU_DTYPE), _tri(r, r, "le"), preferred_element_type=F32)[0:1]
    slot = lax.broadcasted_iota(jnp.int32, (cap, r), 0).astype(F32)
    passed = jnp.where(cumrow <= slot, 1.0, 0.0).astype(MXU_DTYPE)
    row_of = jnp.dot(passed, jnp.ones((r, l), MXU_DTYPE), preferred_element_type=F32)[:, 0:1]
    base_of = jnp.dot(passed, rowtot.astype(MXU_DTYPE), preferred_element_type=F32)[:, 0:1]
    onehot = jnp.where(lax.broadcasted_iota(jnp.int32, (cap, r), 1).astype(F32) == row_of, 1.0, 0.0)
    linc_of = jnp.dot(onehot.astype(MXU_DTYPE), linc.astype(MXU_DTYPE), preferred_element_type=F32)
    k = slot[:, 0:1] - base_of
    col_of = jnp.sum(jnp.where(linc_of <= k, 1.0, 0.0), axis=-1, keepdims=True)
    idx_ref[0, 0] = (row_of * l + col_of).astype(jnp.int32)
    at_col = lax.broadcasted_iota(jnp.int32, (cap, l), 1).astype(F32) == col_of
    aff_rows = jnp.dot(onehot, aff_ref[0, 0], precision=HI, preferred_element_type=F32)
    gate_ref[0, 0] = jnp.sum(jnp.where(at_col, aff_rows, 0.0), axis=-1, keepdims=True)
    prank_rows = jnp.dot(onehot, prank_ref[0, 0].astype(F32), precision=HI, preferred_element_type=F32)
    dest_ref[0, 0] = jnp.sum(jnp.where(at_col, prank_rows, 0.0), axis=-1, keepdims=True).astype(jnp.int32)


def _slots(sel, aff4, prank, cap):
    b, ne, r, l = sel.shape
    blk = pl.BlockSpec((1, 1, r, l), lambda bi, e: (bi, e, 0, 0))
    oblk = pl.BlockSpec((1, 1, cap, 1), lambda bi, e: (bi, e, 0, 0))
    return pl.pallas_call(
        _slots_body,
        out_shape=[jax.ShapeDtypeStruct((b, ne, cap, 1), jnp.int32), jax.ShapeDtypeStruct((b, ne, cap, 1), jnp.int32),
                   jax.ShapeDtypeStruct((b, ne, cap, 1), F32)],
        grid=(b, ne),
        in_specs=[blk, blk, blk],
        out_specs=[oblk, oblk, oblk],
        compiler_params=_params("parallel", "parallel"),
        name="moe_slots",
    )(sel, aff4, prank)


FFN_CHUNK = 256


def _ffn_body(idx_ref, dest_ref, h_hbm, gate_ref, wg_ref, wu_ref, wd_ref, z_hbm, xa, xb, ya, yb, sems,
              *, n_tok, n_pair):
    n_b, tiles = pl.num_programs(1), pl.num_programs(2)
    step = (pl.program_id(0) * n_b + pl.program_id(1)) * tiles + pl.program_id(2)
    last = pl.num_programs(0) * n_b * tiles - 1
    ts = xa.shape[0]
    n_groups = wg_ref.shape[2] // FFN_CHUNK
    per_group = ts // n_groups

    def sample_of(k):
        return (k // tiles) % n_b

    def gather(k, half, buf, sem):
        base, rows = (2 * k + half) * ts, sample_of(k) * n_tok
        return lambda s: pltpu.make_async_copy(
            h_hbm.at[pl.ds(rows + idx_ref[base + s], 1), :], buf.at[pl.ds(s, 1), :], sems.at[sem])

    def scatter(k, half, buf, sem):
        base, rows = (2 * k + half) * ts, sample_of(k) * n_pair
        return lambda s: pltpu.make_async_copy(
            buf.at[pl.ds(s, 1), :], z_hbm.at[pl.ds(rows + dest_ref[base + s], 1), :], sems.at[sem])

    def start_all(copy):
        def body(s, c):
            copy(s).start()
            return c
        lax.fori_loop(0, ts, body, 0, unroll=8)

    def wait_rows(buf, sem, from_hbm):
        if from_hbm:
            pltpu.make_async_copy(h_hbm.at[pl.ds(0, ts), :], buf, sems.at[sem]).wait()
        else:
            pltpu.make_async_copy(buf, z_hbm.at[pl.ds(0, ts), :], sems.at[sem]).wait()

    def ffn(xbuf, gate, copies):
        x = xbuf[...].astype(MXU_DTYPE)
        y = None
        for j in range(n_groups):
            for copy in copies:
                for s in range(j * per_group, (j + 1) * per_group):
                    copy(s).start()
            cols = slice(j * FFN_CHUNK, (j + 1) * FFN_CHUNK)
            hid = _silu(_mm(x, wg_ref[0, :, cols])) * _mm(x, wu_ref[0, :, cols])
            part = _mm(hid, wd_ref[0, cols, :])
            y = part if y is None else y + part
        return y * gate

    @pl.when(step == 0)
    def _():
        start_all(gather(step, 0, xa, 0))

    wait_rows(xa, 0, True)
    y_a = ffn(xa, gate_ref[0, 0, 0:ts], [gather(step, 1, xb, 1)])

    @pl.when(step > 0)
    def _():
        wait_rows(ya, 2, False)
        wait_rows(yb, 3, False)

    ya[...] = y_a
    wait_rows(xb, 1, True)
    nxt = jnp.minimum(step + 1, last)
    y_b = ffn(xb, gate_ref[0, 0, ts:2 * ts], [scatter(step, 0, ya, 2), gather(nxt, 0, xa, 0)])
    yb[...] = y_b
    start_all(scatter(step, 1, yb, 3))

    @pl.when(step == last)
    def _():
        wait_rows(xa, 0, True)
        wait_rows(ya, 2, False)
        wait_rows(yb, 3, False)


def _expert_ffn(idx, dest, gate, h, lw, ts):
    b, ne, cap, _ = gate.shape
    n, d = h.shape[1:]
    f = lw["w_gate"].shape[-1]
    n_pair = ne * cap
    by_expert = lambda a: jnp.swapaxes(a, 0, 1)
    grid_spec = pltpu.PrefetchScalarGridSpec(
        num_scalar_prefetch=2,
        grid=(ne, b, cap // (2 * ts)),
        in_specs=[pl.BlockSpec(memory_space=pl.ANY),
                  pl.BlockSpec((1, 1, 2 * ts, 1), lambda e, bi, t, *_: (e, bi, t, 0)),
                  pl.BlockSpec((1, d, f), lambda e, bi, t, *_: (e, 0, 0)),
                  pl.BlockSpec((1, d, f), lambda e, bi, t, *_: (e, 0, 0)),
                  pl.BlockSpec((1, f, d), lambda e, bi, t, *_: (e, 0, 0))],
        out_specs=pl.BlockSpec(memory_space=pl.ANY),
        scratch_shapes=[pltpu.VMEM((ts, d), F32)] * 4 + [pltpu.SemaphoreType.DMA((4,))])
    return pl.pallas_call(
        functools.partial(_ffn_body, n_tok=n, n_pair=n_pair),
        out_shape=jax.ShapeDtypeStruct((b * n_pair, d), F32),
        grid_spec=grid_spec,
        compiler_params=_params("arbitrary", "arbitrary", "arbitrary"),
        name="moe_ffn",
    )(by_expert(idx).reshape(-1), by_expert(dest).reshape(-1), h.reshape(b * n, d), by_expert(gate),
      lw["w_gate"], lw["w_up"], lw["w_down"])


def _combine_body(tb_ref, x_ref, g2_ref, ts_ref, te_ref, z_hbm, o_ref, zbuf, acc_ref, sem, *, n_pair):
    n_tiles = pl.num_programs(1)
    tile = pl.program_id(0) * n_tiles + pl.program_id(1)
    pc = zbuf.shape[1]
    tt = x_ref.shape[1]

    def plan(g):
        b, i = g // n_tiles, g % n_tiles
        lo, hi = tb_ref[b * (n_tiles + 1) + i], tb_ref[b * (n_tiles + 1) + i + 1]
        p0 = (lo // 8) * 8
        return b, p0, (hi - p0 + pc - 1) // pc

    def fetch(b, p0, k, slot):
        cs = pl.multiple_of(jnp.minimum(p0 + k * pc, n_pair - pc), 8)
        return cs, pltpu.make_async_copy(z_hbm.at[pl.ds(b * n_pair + cs, pc), :], zbuf.at[slot], sem.at[slot])

    b, p0, n_chunks = plan(tile)
    start_row, end_row = ts_ref[0, 0], te_ref[0, 0]
    acc_ref[...] = jnp.zeros_like(acc_ref)

    @pl.when((tile == 0) & (n_chunks > 0))
    def _():
        fetch(b, p0, 0, 0)[1].start()

    def chunk(k, c):
        slot = k % 2

        @pl.when(k + 1 < n_chunks)
        def _():
            fetch(b, p0, k + 1, 1 - slot)[1].start()

        cs, cp = fetch(b, p0, k, slot)
        cp.wait()
        pair = cs + lax.broadcasted_iota(jnp.int32, (pc, tt), 0)
        own = jnp.where(pair >= jnp.maximum(start_row, p0 + k * pc), jnp.where(pair < end_row, 1.0, 0.0), 0.0)
        own = own.astype(MXU_DTYPE)
        z = zbuf[slot]
        zh = z.astype(MXU_DTYPE)
        zl = (z - zh.astype(F32)).astype(MXU_DTYPE)
        acc_ref[...] += (lax.dot_general(own, zh, TN_DIMS, preferred_element_type=F32)
                         + lax.dot_general(own, zl, TN_DIMS, preferred_element_type=F32))
        return c

    lax.fori_loop(0, n_chunks, chunk, 0)

    @pl.when(tile + 1 < pl.num_programs(0) * n_tiles)
    def _():
        nb, np0, nn = plan(tile + 1)

        @pl.when(nn > 0)
        def _():
            fetch(nb, np0, 0, 0)[1].start()

    o_ref[0] = x_ref[0] + g2_ref[0] * acc_ref[...]


def _combine(x, g2, tstart, tend, z, n_pair, tt, pc):
    b, n, d = x.shape
    nt = n // tt
    ts4 = tstart.reshape(b, nt, 1, tt)
    te4 = tend.reshape(b, nt, 1, tt)
    bounds = jnp.concatenate([ts4[:, :, 0, 0], jnp.full((b, 1), n_pair, jnp.int32)], axis=1).reshape(-1)
    grid_spec = pltpu.PrefetchScalarGridSpec(
        num_scalar_prefetch=1,
        grid=(b, nt),
        in_specs=[pl.BlockSpec((1, tt, d), lambda bi, i, *_: (bi, i, 0)),
                  pl.BlockSpec((1, 1, d), lambda bi, i, *_: (bi, 0, 0)),
                  pl.BlockSpec((1, 1, 1, tt), lambda bi, i, *_: (bi, i, 0, 0)),
                  pl.BlockSpec((1, 1, 1, tt), lambda bi, i, *_: (bi, i, 0, 0)),
                  pl.BlockSpec(memory_space=pl.ANY)],
        out_specs=pl.BlockSpec((1, tt, d), lambda bi, i, *_: (bi, i, 0)),
        scratch_shapes=[pltpu.VMEM((2, pc, d), F32), pltpu.VMEM((tt, d), F32), pltpu.SemaphoreType.DMA((2,))])
    return pl.pallas_call(
        functools.partial(_combine_body, n_pair=n_pair),
        out_shape=jax.ShapeDtypeStruct((b, n, d), F32),
        grid_spec=grid_spec,
        compiler_params=_params("arbitrary", "arbitrary"),
        name="moe_combine",
    )(bounds, x, g2, ts4, te4, z)


def _moe_latent(x, h, aff, g2, lw):
    b, n, d = x.shape
    ne = aff.shape[1]
    cap = EC_CAPACITY_FACTOR * n // ne
    aff4 = aff.reshape(b, ne, n // LANES, LANES)
    sel, prank, tstart, tend = _select(aff4, cap)
    idx, dest, gate = _slots(sel, aff4, prank, cap)
    z = _expert_ffn(idx, dest, gate, h, lw, min(cap // 2, 256))
    return _combine(x, g2, tstart, tend, z, ne * cap, 256, 256)


def _ctx_coef_body(aff_ref, coef_ref, *, cap):
    nb, _, nc = aff_ref.shape
    excl = _tri(nc, nc, "lt")
    outs = []
    for b in range(nb):
        aff = aff_ref[b]
        sel = _select_mask(aff, cap, lambda mm: jnp.dot(mm.astype(MXU_DTYPE), excl, preferred_element_type=F32), (1,))
        outs.append(sel * aff)
    coef_ref[:, 0, :] = jnp.concatenate(outs, axis=-1)


def _ctx_coef(aff, cap):
    b, ne, nc = aff.shape
    return pl.pallas_call(
        functools.partial(_ctx_coef_body, cap=cap),
        out_shape=jax.ShapeDtypeStruct((ne, 1, b * nc), F32),
        name="ctx_moe_select",
    )(aff)


def _ctx_ffn_body(h_ref, coef_ref, wg_ref, wu_ref, wd_ref, x_ref, g2_ref, o_ref):
    e = pl.program_id(0)
    rows = h_ref.shape[0]
    h = h_ref[...]
    y = _mm(_silu(_mm(h, wg_ref[0])) * _mm(h, wu_ref[0]), wd_ref[0])
    diag = (lax.broadcasted_iota(jnp.int32, (rows, rows), 0) == lax.broadcasted_iota(jnp.int32, (rows, rows), 1))
    coef = jnp.sum(jnp.where(diag, coef_ref[0], 0.0), axis=-1, keepdims=True)

    @pl.when(e == 0)
    def _():
        o_ref[...] = x_ref[...]

    o_ref[...] += g2_ref[...] * (coef * y)


def _ctx_moe(xc, hc, aff, g2, lw):
    b, nc, d = xc.shape
    ne = aff.shape[1]
    f = lw["w_gate"].shape[-1]
    coef = _ctx_coef(aff, EC_CAPACITY_FACTOR * nc // ne)
    rows = b * nc
    out = pl.pallas_call(
        _ctx_ffn_body,
        out_shape=jax.ShapeDtypeStruct((rows, d), F32),
        grid=(ne,),
        in_specs=[pl.BlockSpec((rows, d), lambda e: (0, 0)),
                  pl.BlockSpec((1, 1, rows), lambda e: (e, 0, 0)),
                  pl.BlockSpec((1, d, f), lambda e: (e, 0, 0)),
                  pl.BlockSpec((1, d, f), lambda e: (e, 0, 0)),
                  pl.BlockSpec((1, f, d), lambda e: (e, 0, 0)),
                  pl.BlockSpec((rows, d), lambda e: (0, 0)),
                  pl.BlockSpec((1, d), lambda e: (0, 0))],
        out_specs=pl.BlockSpec((rows, d), lambda e: (0, 0)),
        compiler_params=_params("arbitrary"),
        name="ctx_moe_ffn",
    )(hc.reshape(rows, d), coef, lw["w_gate"], lw["w_up"], lw["w_down"], xc.reshape(rows, d), g2)
    return out.reshape(b, nc, d)


def _final_body(x_ref, g_ref, o_ref):
    o_ref[0] = _rms(x_ref[0], g_ref[...])


def _final_norm(x, g, tm):
    b, n, d = x.shape
    return pl.pallas_call(
        _final_body,
        out_shape=jax.ShapeDtypeStruct((b, n, d), F32),
        grid=(b, n // tm),
        in_specs=[pl.BlockSpec((1, tm, d), lambda bi, i: (bi, i, 0)), pl.BlockSpec((1, d), lambda bi, i: (0, 0))],
        out_specs=pl.BlockSpec((1, tm, d), lambda bi, i: (bi, i, 0)),
        compiler_params=_params("parallel", "parallel"),
        name="final_norm",
    )(x, g)


def _rope_tables(n):
    t = np.arange(n)
    freqs = ROPE_THETA ** (-np.arange(ROPE_FREQS, dtype=np.float32) / ROPE_FREQS)
    ang_r = (t // GRID_W).astype(np.float32)[:, None] * freqs
    ang_c = (t % GRID_W).astype(np.float32)[:, None] * freqs
    cos = np.concatenate([np.cos(ang_r)] * 2 + [np.cos(ang_c)] * 2, axis=1)
    sin = np.concatenate([-np.sin(ang_r), np.sin(ang_r), -np.sin(ang_c), np.sin(ang_c)], axis=1)
    reps = LANES // HEAD_DIM
    return (jnp.asarray(np.tile(cos, (1, reps)), F32), jnp.asarray(np.tile(sin, (1, reps)), F32))


def _block_geometry(n):
    nblk = n // ATT_BLOCK
    a = np.arange(ATT_BLOCK)
    out = []
    for i in (0, 1, nblk - 1):
        j0 = int(np.clip(i - 1, 0, nblk - 3))
        out.append((i * ATT_BLOCK + a, [(j0 + j) * ATT_BLOCK + a for j in range(3)]))
    return out


def _neighbourhood_bias(rpb, n):
    rows = n // GRID_W
    win_r = min(NA_WIN_R, rows)
    blk_rows = ATT_BLOCK // GRID_W
    cols = np.arange(GRID_W)
    col_off = cols[None, :] - cols[:, None] + NA_WIN_C - 1
    col_hot = jnp.asarray(col_off[:, :, None] == np.arange(2 * NA_WIN_C - 1), F32)
    c0 = np.clip(cols - NA_WIN_C // 2, 0, GRID_W - NA_WIN_C)
    col_ok = (cols[None, :] >= c0[:, None]) & (cols[None, :] < c0[:, None] + NA_WIN_C)
    kinds = []
    for q_tok, k_chunks in _block_geometry(n):
        qr = q_tok[::GRID_W] // GRID_W
        r0 = np.clip(qr - win_r // 2, 0, rows - win_r)
        chunks = []
        for k_tok in k_chunks:
            kr = k_tok[::GRID_W] // GRID_W
            row_off = kr[None, :] - qr[:, None] + NA_WIN_R - 1
            row_hot = jnp.asarray(row_off[:, :, None] == np.arange(2 * NA_WIN_R - 1), F32)
            row_ok = (kr[None, :] >= r0[:, None]) & (kr[None, :] < r0[:, None] + win_r)
            ok = (row_ok[:, None, :, None] & col_ok[None, :, None, :]).reshape(ATT_BLOCK, ATT_BLOCK)
            vals = jnp.einsum("qkr,hrc,xyc->hqxky", row_hot, rpb, col_hot, precision=HI)
            vals = vals.reshape(-1, blk_rows * GRID_W, blk_rows * GRID_W) * LOG2E
            chunks.append(jnp.where(ok[None], vals, NEG_INF))
        kinds.append(jnp.stack(chunks, axis=1))
    return jnp.stack(kinds, axis=0)


def _window_mask(n):
    kinds = []
    for q_tok, k_chunks in _block_geometry(n):
        kinds.append(np.stack([np.where(np.abs(k_tok[None] - q_tok[:, None]) <= WINDOW, 0.0, NEG_INF)
                               for k_tok in k_chunks])[None])
    return jnp.asarray(np.stack(kinds), F32)


def kernel(x, c, ctx, c_ctx, w_mod, b_mod, norm_mix, norm_ffn, w_in, rpb, q_norm, k_norm, sink, sgu_norm, w_sgu,
           b_sgu, out_norm, w_out, w_router, w_gate, w_up, w_down, final_norm):
    depth = w_mod.shape[0]
    b, n, d = x.shape
    nc = ctx.shape[1]
    group_w = MIXER_WIDTH // SG_GROUPS

    cvecs = jnp.concatenate([c, c_ctx[None], jnp.zeros((8 - b - 1, d), F32)], axis=0)
    mods = _adaln_all(cvecs, w_mod, b_mod).reshape(depth, 8, 6, d)

    cos, sin = _rope_tables(n)
    cos_c, sin_c = jnp.ones((nc, LANES), F32), jnp.zeros((nc, LANES), F32)
    win_mask = _window_mask(n)
    blk = np.arange(256) // HEAD_DIM
    gsum = jnp.asarray(blk[:, None] == blk[None, :], MXU_DTYPE)

    xc = ctx
    for l in range(depth):
        ctx_needed = l < depth - 1
        lat = [mods[l, :b, j][:, None, :] for j in range(6)]
        cm = [jnp.broadcast_to(mods[l, b, j][None, None, :], (b, 1, d)) for j in range(6)]
        lw = {
            "w_in": w_in[l].astype(MXU_DTYPE),
            "qn": jnp.tile(q_norm[l], 4)[None], "kn": jnp.tile(k_norm[l], 2)[None], "gn": sgu_norm[l][None],
            "w_sgu": w_sgu[l].astype(MXU_DTYPE),
            "b_sgu": jnp.repeat(b_sgu[l].T, group_w, axis=1),
            "gsum": gsum,
            "out_norm": out_norm[l][None], "w_out": w_out[l].astype(MXU_DTYPE), "norm_ffn": norm_ffn[l][None],
            "w_router_t": w_router[l].T,
            "w_gate": w_gate[l].astype(MXU_DTYPE), "w_up": w_up[l].astype(MXU_DTYPE),
            "w_down": w_down[l].astype(MXU_DTYPE),
        }
        nm = norm_mix[l][None]
        hp = _inproj(x, nm, lat[0], lat[1], lw, cos, sin, 512)
        cp = _inproj(xc, nm, cm[0], cm[1], lw, cos_c, sin_c, nc)
        qa, ka, va, qb, kb, vb, qs, ks, vs, yd = hp
        _, ka_c, va_c, _, kb_c, vb_c, _, ks_c, vs_c, _ = cp
        sink_l = sink[l] * LOG2E

        ya = _local_attn(qa, ka, va, ka_c, va_c, _neighbourhood_bias(rpb[l], n))
        yb = _global_attn(qb, jnp.concatenate([kb, kb_c], axis=1), jnp.concatenate([vb, vb_c], axis=1))
        yc = _local_attn(qs, ks, vs, ks_c, vs_c, win_mask, sink_l)
        x_mid, h2, aff = _merge((ya, yb, yc, yd), x, lw, lat[2], lat[3], lat[4], 512)
        x = _moe_latent(x_mid, h2, aff, lat[5], lw)

        if ctx_needed:
            ys_c = _ctx_attn(sink_l, cp[:9])
            xc_mid, hc2, aff_c = _merge((*ys_c, cp[9]), xc, lw, cm[2], cm[3], cm[4], nc)
            xc = _ctx_moe(xc_mid, hc2, aff_c, mods[l, b, 5][None], lw)
    return _final_norm(x, final_norm[None], 512)
```

```python
import functools

import numpy as np
import jax
import jax.numpy as jnp
from jax import lax
from jax.experimental import pallas as pl
from jax.experimental.pallas import tpu as pltpu

HEAD_DIM = 64
GRID_W = 64
MIXER_WIDTH = 256
NA_WIN_R = 8
NA_WIN_C = 16
WINDOW = 128
CHUNK = 128
SG_GROUPS = 4
EC_CAPACITY_FACTOR = 2
ROPE_THETA = 10000.0
ROPE_FREQS = HEAD_DIM // 4
EPS = 1e-6
NEG_INF = -1e30
PROJ_SIZES = (256, 256, 256, 256, 128, 128, 256, 128, 128, 256, 256)
PROJ_OFFS = tuple(int(v) for v in np.cumsum((0,) + PROJ_SIZES))
LOG2E = 1.4426950408889634
Q_SCALE = HEAD_DIM ** -0.5 * LOG2E

LANES = 128
VMEM_LIMIT = 56 * 2 ** 20
ATT_BLOCK = 256
MXU_DTYPE = jnp.bfloat16
ACT_DTYPE = jnp.bfloat16
F32 = jnp.float32
HI = lax.Precision.HIGHEST
NT_DIMS = (((1,), (1,)), ((), ()))
TN_DIMS = (((0,), (0,)), ((), ()))


def _params(*sem):
    return pltpu.CompilerParams(dimension_semantics=sem, vmem_limit_bytes=VMEM_LIMIT)


def _mm(a, b):
    return jnp.dot(a.astype(MXU_DTYPE), b.astype(MXU_DTYPE), preferred_element_type=F32)


def _rms(x, g):
    return x * lax.rsqrt(jnp.mean(x * x, axis=-1, keepdims=True) + EPS) * g


def _silu(x):
    return x / (1.0 + jnp.exp(-x))


ROW_TILE = 8


def _read_row_tiles(ref, lead, n_rows):
    return jnp.concatenate([ref[lead + (pl.ds(a, n_rows, stride=ROW_TILE), slice(None))]
                            for a in range(ROW_TILE)], axis=-1)


def _write_row_tiles(ref, lead, val):
    for a in range(ROW_TILE):
        ref[lead + (pl.ds(a, val.shape[0], stride=ROW_TILE), slice(None))] = val[:, a * LANES:(a + 1) * LANES]


def _mod_body(c_ref, w_ref, b_ref, o_ref):
    s = _silu(c_ref[...])
    o_ref[0] = jnp.dot(s, w_ref[0], precision=HI, preferred_element_type=F32) + b_ref[0]


def _adaln_all(cvecs, w_mod, b_mod):
    depth, d, d6 = w_mod.shape
    tn = 1536
    rows = cvecs.shape[0]
    return pl.pallas_call(
        _mod_body,
        out_shape=jax.ShapeDtypeStruct((depth, rows, d6), F32),
        grid=(depth, d6 // tn),
        in_specs=[pl.BlockSpec((rows, d), lambda l, j: (0, 0)),
                  pl.BlockSpec((1, d, tn), lambda l, j: (l, 0, j)),
                  pl.BlockSpec((1, 1, tn), lambda l, j: (l, 0, j))],
        out_specs=pl.BlockSpec((1, rows, tn), lambda l, j: (l, 0, j)),
        compiler_params=_params("parallel", "parallel"),
        name="adaln",
    )(cvecs, w_mod, b_mod.reshape(depth, 1, d6))


def _head_rms(t, g, gsum_ref):
    w = t.shape[-1]
    sq = t * t
    hi = sq.astype(MXU_DTYPE)
    lo = (sq - hi.astype(F32)).astype(MXU_DTYPE)
    gs = gsum_ref[0:w, 0:w]
    ss = jnp.dot(hi, gs, preferred_element_type=F32) + jnp.dot(lo, gs, preferred_element_type=F32)
    return t * lax.rsqrt(ss * (1.0 / HEAD_DIM) + EPS) * g


def _rope(t, cos, sin_signed):
    w = t.shape[-1]
    rep = w // LANES
    if rep > 1:
        cos = jnp.concatenate([cos] * rep, axis=-1)
        sin_signed = jnp.concatenate([sin_signed] * rep, axis=-1)
    lane = lax.broadcasted_iota(jnp.int32, t.shape, 1)
    first_half = (lane % (2 * ROPE_FREQS)) < ROPE_FREQS
    partner = jnp.where(first_half, pltpu.roll(t, w - ROPE_FREQS, 1), pltpu.roll(t, ROPE_FREQS, 1))
    return t * cos + partner * sin_signed


def _with_ones_lane(v):
    lane = lax.broadcasted_iota(jnp.int32, (v.shape[0], LANES - HEAD_DIM), 1)
    pad = jnp.where(lane == 0, 1.0, 0.0).astype(v.dtype)
    parts = []
    for h in range(v.shape[1] // HEAD_DIM):
        parts += [v[:, h * HEAD_DIM:(h + 1) * HEAD_DIM], pad]
    return jnp.concatenate(parts, axis=-1)


def _inproj_body(x_ref, nw_ref, sh_ref, sc_ref, w_ref, cos_ref, sin_ref, qn_ref, kn_ref, gn_ref,
                 ws_ref, bs_ref, gsum_ref,
                 qa_ref, ka_ref, va_ref, qb_ref, kb_ref, vb_ref, qs_ref, ks_ref, vs_ref, yd_ref):
    x = x_ref[0]
    h = _rms(x, nw_ref[...]) * (1.0 + sc_ref[0]) + sh_ref[0]
    p = _mm(h, w_ref[...])
    o = PROJ_OFFS
    cos, sin = cos_ref[...], sin_ref[...]
    dt = qa_ref.dtype
    qa_ref[0] = (p[:, o[0]:o[1]] * Q_SCALE).astype(dt)
    ka_ref[0] = p[:, o[1]:o[2]].astype(dt)
    va_ref[0] = _with_ones_lane(p[:, o[2]:o[3]].astype(dt))
    qb = _rope(_head_rms(p[:, o[3]:o[4]], qn_ref[...], gsum_ref), cos, sin)
    qb_ref[0] = (qb * Q_SCALE).astype(dt)
    kb_ref[0] = _rope(_head_rms(p[:, o[4]:o[5]], kn_ref[...], gsum_ref), cos, sin).astype(dt)
    vb_ref[0] = _with_ones_lane(p[:, o[5]:o[6]].astype(dt))
    qs_ref[0] = (_rope(p[:, o[6]:o[7]], cos, sin) * Q_SCALE).astype(dt)
    ks_ref[0] = _rope(p[:, o[7]:o[8]], cos, sin).astype(dt)
    vs_ref[0] = _with_ones_lane(p[:, o[8]:o[9]].astype(dt))
    u = jax.nn.gelu(p[:, o[9]:o[10]])
    v = _rms(jax.nn.gelu(p[:, o[10]:o[11]]), gn_ref[...]).astype(MXU_DTYPE)
    lane_group = lax.broadcasted_iota(jnp.int32, (CHUNK, MIXER_WIDTH), 1) // (MIXER_WIDTH // SG_GROUPS)
    for c in range(x.shape[0] // CHUNK):
        rows = slice(c * CHUNK, (c + 1) * CHUNK)
        mixed = bs_ref[...]
        for g in range(SG_GROUPS):
            mg = jnp.dot(ws_ref[g], v[rows], preferred_element_type=F32)
            mixed = mixed + jnp.where(lane_group == g, mg, 0.0)
        yd_ref[0, rows, :] = (u[rows] * mixed).astype(dt)


def _inproj(x, nw, shift, scale, lw, cos, sin, tm):
    b, t, d = x.shape
    widths = tuple(w * (LANES // HEAD_DIM if j % 3 == 2 else 1) for j, w in enumerate(PROJ_SIZES[:9]))
    widths += (MIXER_WIDTH,)
    row = lambda bi, i: (0, 0)
    per_b = lambda bi, i: (bi, 0, 0)
    tile = lambda bi, i: (bi, i, 0)
    return pl.pallas_call(
        _inproj_body,
        out_shape=[jax.ShapeDtypeStruct((b, t, w), ACT_DTYPE) for w in widths],
        grid=(b, t // tm),
        in_specs=[pl.BlockSpec((1, tm, d), tile),
                  pl.BlockSpec((1, d), row),
                  pl.BlockSpec((1, 1, d), per_b),
                  pl.BlockSpec((1, 1, d), per_b),
                  pl.BlockSpec(lw["w_in"].shape, row),
                  pl.BlockSpec((tm, LANES), lambda bi, i: (i, 0)),
                  pl.BlockSpec((tm, LANES), lambda bi, i: (i, 0)),
                  pl.BlockSpec((1, 256), row),
                  pl.BlockSpec((1, 128), row),
                  pl.BlockSpec((1, 256), row),
                  pl.BlockSpec((SG_GROUPS, CHUNK, CHUNK), lambda bi, i: (0, 0, 0)),
                  pl.BlockSpec((CHUNK, MIXER_WIDTH), row),
                  pl.BlockSpec((256, 256), row)],
        out_specs=[pl.BlockSpec((1, tm, w), tile) for w in widths],
        compiler_params=_params("parallel", "parallel"),
        name="inproj",
    )(x, nw, shift, scale, lw["w_in"], cos, sin, lw["qn"], lw["kn"], lw["gn"], lw["w_sgu"],
      lw["b_sgu"], lw["gsum"])


def _attend(q, chunks, sink=None):
    scores = []
    for k, _, bias in chunks:
        s = lax.dot_general(q, k, NT_DIMS, preferred_element_type=F32)
        scores.append(s if bias is None else s + bias)
    m = jnp.max(functools.reduce(jnp.maximum, scores), axis=-1, keepdims=True)
    if sink is not None:
        m = jnp.maximum(m, sink)
    acc = jnp.zeros((q.shape[0], LANES), F32)
    for s, (_, v, _) in zip(scores, chunks):
        acc = acc + jnp.dot(jnp.exp2(s - m).astype(v.dtype), v, preferred_element_type=F32)
    l = acc[:, HEAD_DIM:HEAD_DIM + 1]
    if sink is not None:
        l = l + jnp.exp2(sink - m)
    return acc[:, :HEAD_DIM] / l


def _head(ref, h, width=HEAD_DIM):
    return ref[0, :, h * width:(h + 1) * width]


def _local_attn_body(*refs, group, has_sink):
    if has_sink:
        sink_ref, refs = refs[0], refs[1:]
    q_ref, k0, k1, k2, v0, v1, v2, kc_ref, vc_ref, bias_ref, o_ref = refs
    outs = []
    for h in range(q_ref.shape[-1] // HEAD_DIM):
        kv = h // group
        hb = h if bias_ref.shape[1] > 1 else 0
        chunks = [(_head(kr, kv), _head(vr, kv, LANES), bias_ref[0, hb, j])
                  for j, (kr, vr) in enumerate(((k0, v0), (k1, v1), (k2, v2)))]
        chunks.append((_head(kc_ref, kv), _head(vc_ref, kv, LANES), None))
        outs.append(_attend(_head(q_ref, h), chunks, sink_ref[h] if has_sink else None))
    o_ref[0] = jnp.concatenate(outs, axis=-1).astype(o_ref.dtype)


def _local_attn(q, k, v, kc, vc, bias, sink=None):
    b, n, qw = q.shape
    kw = k.shape[-1]
    nc = kc.shape[1]
    tq = ATT_BLOCK
    nblk = n // tq
    assert nblk >= 4
    group = qw // kw
    hb = bias.shape[1]

    def kmap(j):
        return lambda bi, i: (bi, jnp.clip(i - 1, 0, nblk - 3) + j, 0)

    def bmap(bi, i):
        return (jnp.where(i == 0, 0, jnp.where(i == nblk - 1, 2, 1)), 0, 0, 0, 0)

    in_specs = [pl.BlockSpec((1, tq, qw), lambda bi, i: (bi, i, 0))]
    vw = v.shape[-1]
    assert nc == tq
    in_specs += [pl.BlockSpec((1, tq, kw), kmap(j)) for j in range(3)]
    in_specs += [pl.BlockSpec((1, tq, vw), kmap(j)) for j in range(3)]
    in_specs += [pl.BlockSpec((1, nc, kw), lambda bi, i: (bi, 0, 0)),
                 pl.BlockSpec((1, nc, vw), lambda bi, i: (bi, 0, 0))]
    in_specs += [pl.BlockSpec((1, hb, 3, tq, tq), bmap)]
    args = [q, k, k, k, v, v, v, kc, vc, bias]
    if sink is not None:
        in_specs = [pl.BlockSpec(memory_space=pltpu.SMEM)] + in_specs
        args = [sink] + args
    return pl.pallas_call(
        functools.partial(_local_attn_body, group=group, has_sink=sink is not None),
        out_shape=jax.ShapeDtypeStruct((b, n, qw), ACT_DTYPE),
        grid=(b, nblk),
        in_specs=in_specs,
        out_specs=pl.BlockSpec((1, tq, qw), lambda bi, i: (bi, i, 0)),
        compiler_params=_params("parallel", "parallel"),
        name="local_attn",
    )(*args)


def _ctx_attn_body(sink_ref, qa, ka, va, qb, kb, vb, qs, ks, vs, oa, ob, oc):
    for q_ref, k_ref, v_ref, o_ref, group, use_sink in (
            (qa, ka, va, oa, 1, False), (qb, kb, vb, ob, 2, False), (qs, ks, vs, oc, 2, True)):
        outs = []
        for h in range(q_ref.shape[-1] // HEAD_DIM):
            kv = h // group
            outs.append(_attend(_head(q_ref, h), [(_head(k_ref, kv), _head(v_ref, kv, LANES), None)],
                                sink_ref[h] if use_sink else None))
        o_ref[0] = jnp.concatenate(outs, axis=-1).astype(o_ref.dtype)


def _ctx_attn(sink, qkv):
    b, nc, _ = qkv[0].shape
    spec = lambda a: pl.BlockSpec((1, nc, a.shape[-1]), lambda bi: (bi, 0, 0))
    return pl.pallas_call(
        _ctx_attn_body,
        out_shape=[jax.ShapeDtypeStruct((b, nc, MIXER_WIDTH), ACT_DTYPE)] * 3,
        grid=(b,),
        in_specs=[pl.BlockSpec(memory_space=pltpu.SMEM)] + [spec(a) for a in qkv],
        out_specs=[pl.BlockSpec((1, nc, MIXER_WIDTH), lambda bi: (bi, 0, 0))] * 3,
        compiler_params=_params("parallel"),
        name="ctx_attn",
    )(sink, *qkv)


def _global_attn_body(q_ref, kt_ref, v_ref, o_ref, s_a, s_b, *, tk):
    tq = q_ref.shape[1]
    n_kv = kt_ref.shape[1] // HEAD_DIM
    n_chunks = kt_ref.shape[2] // tk
    group_w = 2 * HEAD_DIM
    qs = [jnp.concatenate([q_ref[0, :, kv * group_w:kv * group_w + HEAD_DIM],
                           q_ref[0, :, kv * group_w + HEAD_DIM:(kv + 1) * group_w]], axis=0)
          for kv in range(n_kv)]

    def scores(i, s_ref):
        ks = pl.multiple_of(i * tk, tk)
        for kv in range(n_kv):
            s_ref[kv] = jnp.dot(qs[kv], kt_ref[0, kv * HEAD_DIM:(kv + 1) * HEAD_DIM, pl.ds(ks, tk)],
                                preferred_element_type=F32)

    def update(i, s_ref, carry):
        ks = pl.multiple_of(i * tk, tk)
        out = []
        for kv in range(n_kv):
            m, acc = carry[kv]
            s = s_ref[kv]
            m_new = jnp.maximum(m, jnp.max(s, axis=-1, keepdims=True))
            p = jnp.exp2(s - m_new).astype(v_ref.dtype)
            pv = jnp.dot(p, v_ref[0, pl.ds(ks, tk), kv * LANES:(kv + 1) * LANES], preferred_element_type=F32)
            out.append((m_new, jnp.exp2(m - m_new) * acc + pv))
        return tuple(out)

    carry = tuple((jnp.full((2 * tq, 1), NEG_INF, F32), jnp.zeros((2 * tq, LANES), F32)) for _ in range(n_kv))
    scores(0, s_a)

    def pair(j, carry):
        scores(2 * j + 1, s_b)
        carry = update(2 * j, s_a, carry)
        scores(2 * j + 2, s_a)
        return update(2 * j + 1, s_b, carry)

    carry = lax.fori_loop(0, (n_chunks - 1) // 2, pair, carry)
    if n_chunks % 2 == 0:
        scores(n_chunks - 1, s_b)
        carry = update(n_chunks - 2, s_a, carry)
        carry = update(n_chunks - 1, s_b, carry)
    else:
        carry = update(n_chunks - 1, s_a, carry)
    outs = []
    for _, acc in carry:
        o = acc[:, :HEAD_DIM] / acc[:, HEAD_DIM:HEAD_DIM + 1]
        outs += [o[:tq], o[tq:]]
    o_ref[0] = jnp.concatenate(outs, axis=-1).astype(o_ref.dtype)


def _global_attn(q, k, v):
    b, n, qw = q.shape
    nk, kw = k.shape[1:]
    n_kv = kw // HEAD_DIM
    tq = ATT_BLOCK
    tk = 1280 if (nk % 1280 == 0 and nk > 1280) else 256
    kt = jnp.swapaxes(k, 1, 2)
    return pl.pallas_call(
        functools.partial(_global_attn_body, tk=tk),
        out_shape=jax.ShapeDtypeStruct((b, n, qw), ACT_DTYPE),
        grid=(b, n // tq),
        in_specs=[pl.BlockSpec((1, tq, qw), lambda bi, i: (bi, i, 0)),
                  pl.BlockSpec((1, kw, nk), lambda bi, i: (bi, 0, 0)),
                  pl.BlockSpec((1, nk, n_kv * LANES), lambda bi, i: (bi, 0, 0))],
        out_specs=pl.BlockSpec((1, tq, qw), lambda bi, i: (bi, i, 0)),
        scratch_shapes=[pltpu.VMEM((n_kv, 2 * tq, tk), F32), pltpu.VMEM((n_kv, 2 * tq, tk), F32)],
        compiler_params=_params("parallel", "parallel"),
        name="global_attn",
    )(q, kt, v)


def _merge_body(ya, yb, yc, yd, x_ref, on_ref, wo_ref, g1_ref, nf_ref, sh_ref, sc_ref, wr_ref,
                xo_ref, h_ref, aff_ref):
    parts = []
    for j, r in enumerate((ya, yb, yc, yd)):
        y = r[0].astype(F32)
        parts.append(_rms(y, on_ref[:, j * MIXER_WIDTH:(j + 1) * MIXER_WIDTH]).astype(MXU_DTYPE))
    xn = x_ref[0] + g1_ref[0] * _mm(jnp.concatenate(parts, axis=-1), wo_ref[...])
    xo_ref[0] = xn
    h = _rms(xn, nf_ref[...]) * (1.0 + sc_ref[0]) + sh_ref[0]
    _write_row_tiles(h_ref, (0,), h)
    logits = lax.dot_general(wr_ref[...], h, NT_DIMS, precision=HI, preferred_element_type=F32)
    e = jnp.exp(logits - jnp.max(logits, axis=0, keepdims=True))
    aff_ref[0] = e / jnp.sum(e, axis=0, keepdims=True)


def _merge(ys, x, lw, g1, shift, scale, tm):
    b, t, d = x.shape
    ne = lw["w_router_t"].shape[0]
    row = lambda bi, i: (0, 0)
    per_b = lambda bi, i: (bi, 0, 0)
    tile = lambda bi, i: (bi, i, 0)
    return pl.pallas_call(
        _merge_body,
        out_shape=[jax.ShapeDtypeStruct((b, t, d), F32), jax.ShapeDtypeStruct((b, t * ROW_TILE, LANES), F32),
                   jax.ShapeDtypeStruct((b, ne, t), F32)],
        grid=(b, t // tm),
        in_specs=[pl.BlockSpec((1, tm, MIXER_WIDTH), tile)] * 4 + [
            pl.BlockSpec((1, tm, d), tile),
            pl.BlockSpec((1, d), row),
            pl.BlockSpec((d, d), row),
            pl.BlockSpec((1, 1, d), per_b),
            pl.BlockSpec((1, d), row),
            pl.BlockSpec((1, 1, d), per_b),
            pl.BlockSpec((1, 1, d), per_b),
            pl.BlockSpec((ne, d), row)],
        out_specs=[pl.BlockSpec((1, tm, d), tile), pl.BlockSpec((1, tm * ROW_TILE, LANES), tile),
                   pl.BlockSpec((1, ne, tm), lambda bi, i: (bi, 0, i))],
        compiler_params=_params("parallel", "parallel"),
        name="merge",
    )(*ys, x, lw["out_norm"], lw["w_out"], g1, lw["norm_ffn"], shift, scale, lw["w_router_t"])


def _tri(n, m, mode):
    r = lax.broadcasted_iota(jnp.int32, (n, m), 0)
    c = lax.broadcasted_iota(jnp.int32, (n, m), 1)
    return jnp.where({"lt": r < c, "le": r <= c, "gt": r > c}[mode], 1.0, 0.0).astype(MXU_DTYPE)


def _count(mask, axes):
    out = jnp.where(mask, 1.0, 0.0)
    for ax in sorted(axes, reverse=True):
        out = jnp.sum(out, axis=ax, keepdims=True)
    return out


def _kth_largest_bits(bits, cap, axes):
    shape = tuple(1 if a in axes else s for a, s in enumerate(bits.shape))

    def body(i, t):
        cand = t | lax.shift_left(jnp.int32(1), 30 - i)
        return jnp.where(_count(bits >= cand, axes) >= cap, cand, t)

    return lax.fori_loop(0, 31, body, jnp.zeros(shape, jnp.int32))


def _prefix_tokens(m, exact_rows):
    e, r, l = m.shape
    m2 = m.reshape(e * r, l).astype(MXU_DTYPE)
    local = jnp.dot(m2, _tri(l, l, "lt"), preferred_element_type=F32)
    rowtot = jnp.dot(m2, jnp.ones((l, l), MXU_DTYPE), preferred_element_type=F32).reshape(e, r, l)
    below = _tri(r, r, "gt")
    if exact_rows:
        base = [jnp.dot(below, rowtot[i].astype(MXU_DTYPE), preferred_element_type=F32) for i in range(e)]
    else:
        base = [jnp.dot(below.astype(F32), rowtot[i], precision=HI, preferred_element_type=F32)
                for i in range(e)]
    return local.reshape(e, r, l) + jnp.stack(base, axis=0)


def _select_mask(aff, cap, prefix_fn, axes):
    bits = pltpu.bitcast(aff, jnp.int32)
    thr = _kth_largest_bits(bits, cap, axes)
    gt = bits > thr
    eq = bits == thr
    need = cap - _count(gt, axes)
    eq_rank = prefix_fn(jnp.where(eq, 1.0, 0.0))
    take_eq = jnp.where(eq, jnp.where(eq_rank < need, 1.0, 0.0), 0.0)
    return jnp.where(gt, 1.0, take_eq)


def _select_body(aff_ref, sel_ref, prank_ref, tstart_ref, tend_ref, *, cap):
    aff = aff_ref[0]
    ne = aff.shape[0]
    sel = _select_mask(aff, cap, functools.partial(_prefix_tokens, exact_rows=True), (1, 2))
    sel_ref[0] = sel
    cnt = jnp.sum(sel, axis=0)
    tstart = _prefix_tokens(cnt[None], exact_rows=False)[0]
    tstart_ref[0] = tstart.astype(jnp.int32)
    tend_ref[0] = (tstart + cnt).astype(jnp.int32)
    run = tstart
    for e in range(ne):
        prank_ref[0, e] = run.astype(jnp.int32)
        run = run + sel[e]


def _select(aff4, cap):
    b, ne, r, l = aff4.shape
    blk4 = pl.BlockSpec((1, ne, r, l), lambda bi: (bi, 0, 0, 0))
    blk3 = pl.BlockSpec((1, r, l), lambda bi: (bi, 0, 0))
    return pl.pallas_call(
        functools.partial(_select_body, cap=cap),
        out_shape=[jax.ShapeDtypeStruct((b, ne, r, l), F32), jax.ShapeDtypeStruct((b, ne, r, l), jnp.int32),
                   jax.ShapeDtypeStruct((b, r, l), jnp.int32), jax.ShapeDtypeStruct((b, r, l), jnp.int32)],
        grid=(b,),
        in_specs=[blk4],
        out_specs=[blk4, blk4, blk3, blk3],
        compiler_params=_params("parallel"),
        name="moe_select",
    )(aff4)


def _slots_body(sel_ref, aff_ref, prank_ref, idx_ref, dest_ref, gate_ref):
    m = sel_ref[0, 0]
    r, l = m.shape
    cap = idx_ref.shape[2]
    mb = m.astype(MXU_DTYPE)
    linc = jnp.dot(mb, _tri(l, l, "le"), preferred_element_type=F32)
    rowtot = jnp.dot(mb, jnp.ones((l, l), MXU_DTYPE), preferred_element_type=F32)
    rowtot_lane = lax.dot_general(jnp.ones((8, l), MXU_DTYPE), mb, NT_DIMS, preferred_element_type=F32)
    cumrow = jnp.dot(rowtot_lane.astype(MXU_DTYPE), _tri(r, r, "le"), preferred_element_type=F32)[0:1]
    slot = lax.broadcasted_iota(jnp.int32, (cap, r), 0).astype(F32)
    passed = jnp.where(cumrow <= slot, 1.0, 0.0).astype(MXU_DTYPE)
    row_of = jnp.dot(passed, jnp.ones((r, l), MXU_DTYPE), preferred_element_type=F32)[:, 0:1]
    base_of = jnp.dot(passed, rowtot.astype(MXU_DTYPE), preferred_element_type=F32)[:, 0:1]
    onehot = jnp.where(lax.broadcasted_iota(jnp.int32, (cap, r), 1).astype(F32) == row_of, 1.0, 0.0)
    linc_of = jnp.dot(onehot.astype(MXU_DTYPE), linc.astype(MXU_DTYPE), preferred_element_type=F32)
    k = slot[:, 0:1] - base_of
    col_of = jnp.sum(jnp.where(linc_of <= k, 1.0, 0.0), axis=-1, keepdims=True)
    idx_ref[0, 0] = (row_of * l + col_of).astype(jnp.int32)
    at_col = lax.broadcasted_iota(jnp.int32, (cap, l), 1).astype(F32) == col_of
    aff_rows = jnp.dot(onehot, aff_ref[0, 0], precision=HI, preferred_element_type=F32)
    gate_ref[0, 0] = jnp.sum(jnp.where(at_col, aff_rows, 0.0), axis=-1, keepdims=True)
    prank_rows = jnp.dot(onehot, prank_ref[0, 0].astype(F32), precision=HI, preferred_element_type=F32)
    dest_ref[0, 0] = jnp.sum(jnp.where(at_col, prank_rows, 0.0), axis=-1, keepdims=True).astype(jnp.int32)


def _slots(sel, aff4, prank, cap):
    b, ne, r, l = sel.shape
    blk = pl.BlockSpec((1, 1, r, l), lambda bi, e: (bi, e, 0, 0))
    oblk = pl.BlockSpec((1, 1, cap, 1), lambda bi, e: (bi, e, 0, 0))
    return pl.pallas_call(
        _slots_body,
        out_shape=[jax.ShapeDtypeStruct((b, ne, cap, 1), jnp.int32), jax.ShapeDtypeStruct((b, ne, cap, 1), jnp.int32),
                   jax.ShapeDtypeStruct((b, ne, cap, 1), F32)],
        grid=(b, ne),
        in_specs=[blk, blk, blk],
        out_specs=[oblk, oblk, oblk],
        compiler_params=_params("parallel", "parallel"),
        name="moe_slots",
    )(sel, aff4, prank)


FFN_CHUNK = 256


def _ffn_body(idx_ref, dest_ref, h_hbm, gate_ref, wg_ref, wu_ref, wd_ref, z_hbm, xa, xb, ya, yb, sems,
              *, n_tok, n_pair):
    n_b, tiles = pl.num_programs(1), pl.num_programs(2)
    step = (pl.program_id(0) * n_b + pl.program_id(1)) * tiles + pl.program_id(2)
    last = pl.num_programs(0) * n_b * tiles - 1
    ts = xa.shape[0] // ROW_TILE
    n_groups = wg_ref.shape[2] // FFN_CHUNK
    per_group = ts // n_groups

    def sample_of(k):
        return (k // tiles) % n_b

    def tile_of(ref, r):
        return ref.at[pl.ds(pl.multiple_of(r * ROW_TILE, ROW_TILE), ROW_TILE), :]

    def gather(k, half, buf, sem):
        base, rows = (2 * k + half) * ts, sample_of(k) * n_tok
        return lambda s: pltpu.make_async_copy(
            tile_of(h_hbm, rows + idx_ref[base + s]), tile_of(buf, s), sems.at[sem])

    def scatter(k, half, buf, sem):
        base, rows = (2 * k + half) * ts, sample_of(k) * n_pair
        return lambda s: pltpu.make_async_copy(
            tile_of(buf, s), tile_of(z_hbm, rows + dest_ref[base + s]), sems.at[sem])

    def start_all(copy):
        def body(s, c):
            copy(s).start()
            return c
        lax.fori_loop(0, ts, body, 0, unroll=8)

    def wait_rows(buf, sem):
        pltpu.make_async_copy(buf, buf, sems.at[sem]).wait()

    def ffn(xbuf, gate, copies):
        x = _read_row_tiles(xbuf, (), ts).astype(MXU_DTYPE)
        y = None
        for j in range(n_groups):
            for copy in copies:
                for s in range(j * per_group, (j + 1) * per_group):
                    copy(s).start()
            cols = slice(j * FFN_CHUNK, (j + 1) * FFN_CHUNK)
            hid = _silu(_mm(x, wg_ref[0, :, cols])) * _mm(x, wu_ref[0, :, cols])
            part = _mm(hid, wd_ref[0, cols, :])
            y = part if y is None else y + part
        return y * gate

    @pl.when(step == 0)
    def _():
        start_all(gather(step, 0, xa, 0))

    wait_rows(xa, 0)
    y_a = ffn(xa, gate_ref[0, 0, 0:ts], [gather(step, 1, xb, 1)])

    @pl.when(step > 0)
    def _():
        wait_rows(ya, 2)
        wait_rows(yb, 3)

    _write_row_tiles(ya, (), y_a)
    wait_rows(xb, 1)
    nxt = jnp.minimum(step + 1, last)
    y_b = ffn(xb, gate_ref[0, 0, ts:2 * ts], [scatter(step, 0, ya, 2), gather(nxt, 0, xa, 0)])
    _write_row_tiles(yb, (), y_b)
    start_all(scatter(step, 1, yb, 3))

    @pl.when(step == last)
    def _():
        wait_rows(xa, 0)
        wait_rows(ya, 2)
        wait_rows(yb, 3)


def _expert_ffn(idx, dest, gate, h, lw, ts):
    b, ne, cap, _ = gate.shape
    n, d = h.shape[1] // ROW_TILE, ROW_TILE * LANES
    f = lw["w_gate"].shape[-1]
    layer = lw["layer"]
    n_pair = ne * cap
    by_expert = lambda a: jnp.swapaxes(a, 0, 1)
    grid_spec = pltpu.PrefetchScalarGridSpec(
        num_scalar_prefetch=2,
        grid=(ne, b, cap // (2 * ts)),
        in_specs=[pl.BlockSpec(memory_space=pl.ANY),
                  pl.BlockSpec((1, 1, 2 * ts, 1), lambda e, bi, t, *_: (e, bi, t, 0)),
                  pl.BlockSpec((None, 1, d, f), lambda e, bi, t, *_: (layer, e, 0, 0)),
                  pl.BlockSpec((None, 1, d, f), lambda e, bi, t, *_: (layer, e, 0, 0)),
                  pl.BlockSpec((None, 1, f, d), lambda e, bi, t, *_: (layer, e, 0, 0))],
        out_specs=pl.BlockSpec(memory_space=pl.ANY),
        scratch_shapes=[pltpu.VMEM((ts * ROW_TILE, LANES), F32)] * 4 + [pltpu.SemaphoreType.DMA((4,))])
    return pl.pallas_call(
        functools.partial(_ffn_body, n_tok=n, n_pair=n_pair),
        out_shape=jax.ShapeDtypeStruct((b * n_pair * ROW_TILE, LANES), F32),
        grid_spec=grid_spec,
        compiler_params=_params("arbitrary", "arbitrary", "arbitrary"),
        name="moe_ffn",
    )(by_expert(idx).reshape(-1), by_expert(dest).reshape(-1), h.reshape(b * n * ROW_TILE, LANES), by_expert(gate),
      lw["w_gate"], lw["w_up"], lw["w_down"])


def _combine_body(tb_ref, x_ref, g2_ref, ts_ref, te_ref, z_hbm, o_ref, zbuf, acc_ref, sem, *, n_pair):
    n_tiles = pl.num_programs(1)
    tile = pl.program_id(0) * n_tiles + pl.program_id(1)
    pc = zbuf.shape[1] // ROW_TILE
    tt = x_ref.shape[1]

    def plan(g):
        b, i = g // n_tiles, g % n_tiles
        lo, hi = tb_ref[b * (n_tiles + 1) + i], tb_ref[b * (n_tiles + 1) + i + 1]
        p0 = (lo // 8) * 8
        return b, p0, (hi - p0 + pc - 1) // pc

    def fetch(b, p0, k, slot):
        cs = pl.multiple_of(jnp.minimum(p0 + k * pc, n_pair - pc), 8)
        rows = pl.ds(pl.multiple_of((b * n_pair + cs) * ROW_TILE, 8 * ROW_TILE), pc * ROW_TILE)
        return cs, pltpu.make_async_copy(z_hbm.at[rows, :], zbuf.at[slot], sem.at[slot])

    b, p0, n_chunks = plan(tile)
    start_row, end_row = ts_ref[0, 0], te_ref[0, 0]
    acc_ref[...] = jnp.zeros_like(acc_ref)

    @pl.when((tile == 0) & (n_chunks > 0))
    def _():
        fetch(b, p0, 0, 0)[1].start()

    def chunk(k, c):
        slot = k % 2

        @pl.when(k + 1 < n_chunks)
        def _():
            fetch(b, p0, k + 1, 1 - slot)[1].start()

        cs, cp = fetch(b, p0, k, slot)
        cp.wait()
        pair = cs + lax.broadcasted_iota(jnp.int32, (pc, tt), 0)
        own = jnp.where(pair >= jnp.maximum(start_row, p0 + k * pc), jnp.where(pair < end_row, 1.0, 0.0), 0.0)
        own = own.astype(MXU_DTYPE)
        z = _read_row_tiles(zbuf, (slot,), pc).astype(MXU_DTYPE)
        acc_ref[...] += lax.dot_general(own, z, TN_DIMS, preferred_element_type=F32)
        return c

    lax.fori_loop(0, n_chunks, chunk, 0)

    @pl.when(tile + 1 < pl.num_programs(0) * n_tiles)
    def _():
        nb, np0, nn = plan(tile + 1)

        @pl.when(nn > 0)
        def _():
            fetch(nb, np0, 0, 0)[1].start()

    o_ref[0] = x_ref[0] + g2_ref[0] * acc_ref[...]


def _combine(x, g2, tstart, tend, z, n_pair, tt, pc):
    b, n, d = x.shape
    nt = n // tt
    ts4 = tstart.reshape(b, nt, 1, tt)
    te4 = tend.reshape(b, nt, 1, tt)
    bounds = jnp.concatenate([ts4[:, :, 0, 0], jnp.full((b, 1), n_pair, jnp.int32)], axis=1).reshape(-1)
    grid_spec = pltpu.PrefetchScalarGridSpec(
        num_scalar_prefetch=1,
        grid=(b, nt),
        in_specs=[pl.BlockSpec((1, tt, d), lambda bi, i, *_: (bi, i, 0)),
                  pl.BlockSpec((1, 1, d), lambda bi, i, *_: (bi, 0, 0)),
                  pl.BlockSpec((1, 1, 1, tt), lambda bi, i, *_: (bi, i, 0, 0)),
                  pl.BlockSpec((1, 1, 1, tt), lambda bi, i, *_: (bi, i, 0, 0)),
                  pl.BlockSpec(memory_space=pl.ANY)],
        out_specs=pl.BlockSpec((1, tt, d), lambda bi, i, *_: (bi, i, 0)),
        scratch_shapes=[pltpu.VMEM((2, pc * ROW_TILE, LANES), F32), pltpu.VMEM((tt, d), F32),
                        pltpu.SemaphoreType.DMA((2,))])
    return pl.pallas_call(
        functools.partial(_combine_body, n_pair=n_pair),
        out_shape=jax.ShapeDtypeStruct((b, n, d), F32),
        grid_spec=grid_spec,
        compiler_params=_params("arbitrary", "arbitrary"),
        name="moe_combine",
    )(bounds, x, g2, ts4, te4, z)


def _moe_latent(x, h, aff, g2, lw):
    b, n, d = x.shape
    ne = aff.shape[1]
    cap = EC_CAPACITY_FACTOR * n // ne
    aff4 = aff.reshape(b, ne, n // LANES, LANES)
    sel, prank, tstart, tend = _select(aff4, cap)
    idx, dest, gate = _slots(sel, aff4, prank, cap)
    z = _expert_ffn(idx, dest, gate, h, lw, min(cap // 2, 256))
    return _combine(x, g2, tstart, tend, z, ne * cap, 256, 256)


def _ctx_coef_body(aff_ref, coef_ref, *, cap):
    nb, _, nc = aff_ref.shape
    excl = _tri(nc, nc, "lt")
    outs = []
    for b in range(nb):
        aff = aff_ref[b]
        sel = _select_mask(aff, cap, lambda mm: jnp.dot(mm.astype(MXU_DTYPE), excl, preferred_element_type=F32), (1,))
        outs.append(sel * aff)
    coef_ref[:, 0, :] = jnp.concatenate(outs, axis=-1)


def _ctx_coef(aff, cap):
    b, ne, nc = aff.shape
    return pl.pallas_call(
        functools.partial(_ctx_coef_body, cap=cap),
        out_shape=jax.ShapeDtypeStruct((ne, 1, b * nc), F32),
        name="ctx_moe_select",
    )(aff)


def _ctx_ffn_body(h_ref, coef_ref, wg_ref, wu_ref, wd_ref, x_ref, g2_ref, o_ref):
    e = pl.program_id(0)
    rows = h_ref.shape[0] // ROW_TILE
    h = _read_row_tiles(h_ref, (), rows)
    y = _mm(_silu(_mm(h, wg_ref[0])) * _mm(h, wu_ref[0]), wd_ref[0])
    diag = (lax.broadcasted_iota(jnp.int32, (rows, rows), 0) == lax.broadcasted_iota(jnp.int32, (rows, rows), 1))
    coef = jnp.sum(jnp.where(diag, coef_ref[0], 0.0), axis=-1, keepdims=True)

    @pl.when(e == 0)
    def _():
        o_ref[...] = x_ref[...]

    o_ref[...] += g2_ref[...] * (coef * y)


def _ctx_moe(xc, hc, aff, g2, lw):
    b, nc, d = xc.shape
    ne = aff.shape[1]
    f = lw["w_gate"].shape[-1]
    layer = lw["layer"]
    coef = _ctx_coef(aff, EC_CAPACITY_FACTOR * nc // ne)
    rows = b * nc
    out = pl.pallas_call(
        _ctx_ffn_body,
        out_shape=jax.ShapeDtypeStruct((rows, d), F32),
        grid=(ne,),
        in_specs=[pl.BlockSpec((rows * ROW_TILE, LANES), lambda e: (0, 0)),
                  pl.BlockSpec((1, 1, rows), lambda e: (e, 0, 0)),
                  pl.BlockSpec((None, 1, d, f), lambda e: (layer, e, 0, 0)),
                  pl.BlockSpec((None, 1, d, f), lambda e: (layer, e, 0, 0)),
                  pl.BlockSpec((None, 1, f, d), lambda e: (layer, e, 0, 0)),
                  pl.BlockSpec((rows, d), lambda e: (0, 0)),
                  pl.BlockSpec((1, d), lambda e: (0, 0))],
        out_specs=pl.BlockSpec((rows, d), lambda e: (0, 0)),
        compiler_params=_params("arbitrary"),
        name="ctx_moe_ffn",
    )(hc.reshape(rows * ROW_TILE, LANES), coef, lw["w_gate"], lw["w_up"], lw["w_down"], xc.reshape(rows, d), g2)
    return out.reshape(b, nc, d)


def _final_body(x_ref, g_ref, o_ref):
    o_ref[0] = _rms(x_ref[0], g_ref[...])


def _final_norm(x, g, tm):
    b, n, d = x.shape
    return pl.pallas_call(
        _final_body,
        out_shape=jax.ShapeDtypeStruct((b, n, d), F32),
        grid=(b, n // tm),
        in_specs=[pl.BlockSpec((1, tm, d), lambda bi, i: (bi, i, 0)), pl.BlockSpec((1, d), lambda bi, i: (0, 0))],
        out_specs=pl.BlockSpec((1, tm, d), lambda bi, i: (bi, i, 0)),
        compiler_params=_params("parallel", "parallel"),
        name="final_norm",
    )(x, g)


def _rope_tables(n):
    t = np.arange(n)
    freqs = ROPE_THETA ** (-np.arange(ROPE_FREQS, dtype=np.float32) / ROPE_FREQS)
    ang_r = (t // GRID_W).astype(np.float32)[:, None] * freqs
    ang_c = (t % GRID_W).astype(np.float32)[:, None] * freqs
    cos = np.concatenate([np.cos(ang_r)] * 2 + [np.cos(ang_c)] * 2, axis=1)
    sin = np.concatenate([-np.sin(ang_r), np.sin(ang_r), -np.sin(ang_c), np.sin(ang_c)], axis=1)
    reps = LANES // HEAD_DIM
    return (jnp.asarray(np.tile(cos, (1, reps)), F32), jnp.asarray(np.tile(sin, (1, reps)), F32))


def _block_geometry(n):
    nblk = n // ATT_BLOCK
    a = np.arange(ATT_BLOCK)
    out = []
    for i in (0, 1, nblk - 1):
        j0 = int(np.clip(i - 1, 0, nblk - 3))
        out.append((i * ATT_BLOCK + a, [(j0 + j) * ATT_BLOCK + a for j in range(3)]))
    return out


def _neighbourhood_bias(rpb, n):
    rows = n // GRID_W
    win_r = min(NA_WIN_R, rows)
    blk_rows = ATT_BLOCK // GRID_W
    cols = np.arange(GRID_W)
    col_off = cols[None, :] - cols[:, None] + NA_WIN_C - 1
    col_hot = jnp.asarray(col_off[:, :, None] == np.arange(2 * NA_WIN_C - 1), F32)
    c0 = np.clip(cols - NA_WIN_C // 2, 0, GRID_W - NA_WIN_C)
    col_ok = (cols[None, :] >= c0[:, None]) & (cols[None, :] < c0[:, None] + NA_WIN_C)
    kinds = []
    for q_tok, k_chunks in _block_geometry(n):
        qr = q_tok[::GRID_W] // GRID_W
        r0 = np.clip(qr - win_r // 2, 0, rows - win_r)
        chunks = []
        for k_tok in k_chunks:
            kr = k_tok[::GRID_W] // GRID_W
            row_off = kr[None, :] - qr[:, None] + NA_WIN_R - 1
            row_hot = jnp.asarray(row_off[:, :, None] == np.arange(2 * NA_WIN_R - 1), F32)
            row_ok = (kr[None, :] >= r0[:, None]) & (kr[None, :] < r0[:, None] + win_r)
            ok = (row_ok[:, None, :, None] & col_ok[None, :, None, :]).reshape(ATT_BLOCK, ATT_BLOCK)
            vals = jnp.einsum("qkr,hrc,xyc->hqxky", row_hot, rpb, col_hot, precision=HI)
            vals = vals.reshape(-1, blk_rows * GRID_W, blk_rows * GRID_W) * LOG2E
            chunks.append(jnp.where(ok[None], vals, NEG_INF))
        kinds.append(jnp.stack(chunks, axis=1))
    return jnp.stack(kinds, axis=0)


def _window_mask(n):
    kinds = []
    for q_tok, k_chunks in _block_geometry(n):
        kinds.append(np.stack([np.where(np.abs(k_tok[None] - q_tok[:, None]) <= WINDOW, 0.0, NEG_INF)
                               for k_tok in k_chunks])[None])
    return jnp.asarray(np.stack(kinds), F32)


def kernel(x, c, ctx, c_ctx, w_mod, b_mod, norm_mix, norm_ffn, w_in, rpb, q_norm, k_norm, sink, sgu_norm, w_sgu,
           b_sgu, out_norm, w_out, w_router, w_gate, w_up, w_down, final_norm):
    depth = w_mod.shape[0]
    b, n, d = x.shape
    nc = ctx.shape[1]
    group_w = MIXER_WIDTH // SG_GROUPS

    cvecs = jnp.concatenate([c, c_ctx[None], jnp.zeros((8 - b - 1, d), F32)], axis=0)
    mods = _adaln_all(cvecs, w_mod, b_mod).reshape(depth, 8, 6, d)

    cos, sin = _rope_tables(n)
    cos_c, sin_c = jnp.ones((nc, LANES), F32), jnp.zeros((nc, LANES), F32)
    win_mask = _window_mask(n)
    blk = np.arange(256) // HEAD_DIM
    gsum = jnp.asarray(blk[:, None] == blk[None, :], MXU_DTYPE)
    wg_all, wu_all, wd_all = (w.astype(MXU_DTYPE) for w in (w_gate, w_up, w_down))

    xc = ctx
    for l in range(depth):
        ctx_needed = l < depth - 1
        lat = [mods[l, :b, j][:, None, :] for j in range(6)]
        cm = [jnp.broadcast_to(mods[l, b, j][None, None, :], (b, 1, d)) for j in range(6)]
        lw = {
            "w_in": w_in[l].astype(MXU_DTYPE),
            "qn": jnp.tile(q_norm[l], 4)[None], "kn": jnp.tile(k_norm[l], 2)[None], "gn": sgu_norm[l][None],
            "w_sgu": w_sgu[l].astype(MXU_DTYPE),
            "b_sgu": jnp.repeat(b_sgu[l].T, group_w, axis=1),
            "gsum": gsum,
            "out_norm": out_norm[l][None], "w_out": w_out[l].astype(MXU_DTYPE), "norm_ffn": norm_ffn[l][None],
            "w_router_t": w_router[l].T,
            "layer": l, "w_gate": wg_all, "w_up": wu_all, "w_down": wd_all,
        }
        nm = norm_mix[l][None]
        hp = _inproj(x, nm, lat[0], lat[1], lw, cos, sin, 512)
        cp = _inproj(xc, nm, cm[0], cm[1], lw, cos_c, sin_c, nc)
        qa, ka, va, qb, kb, vb, qs, ks, vs, yd = hp
        _, ka_c, va_c, _, kb_c, vb_c, _, ks_c, vs_c, _ = cp
        sink_l = sink[l] * LOG2E

        ya = _local_attn(qa, ka, va, ka_c, va_c, _neighbourhood_bias(rpb[l], n))
        yb = _global_attn(qb, jnp.concatenate([kb, kb_c], axis=1), jnp.concatenate([vb, vb_c], axis=1))
        yc = _local_attn(qs, ks, vs, ks_c, vs_c, win_mask, sink_l)
        x_mid, h2, aff = _merge((ya, yb, yc, yd), x, lw, lat[2], lat[3], lat[4], 512)
        x = _moe_latent(x_mid, h2, aff, lat[5], lw)

        if ctx_needed:
            ys_c = _ctx_attn(sink_l, cp[:9])
            xc_mid, hc2, aff_c = _merge((*ys_c, cp[9]), xc, lw, cm[2], cm[3], cm[4], nc)
            xc = _ctx_moe(xc_mid, hc2, aff_c, mods[l, b, 5][None], lw)
    return _final_norm(x, final_norm[None], 512)
```

```python
import functools

import numpy as np
import jax
import jax.numpy as jnp
from jax import lax
from jax.experimental import pallas as pl
from jax.experimental.pallas import tpu as pltpu

HEAD_DIM = 64
GRID_W = 64
MIXER_WIDTH = 256
NA_WIN_R = 8
NA_WIN_C = 16
WINDOW = 128
CHUNK = 128
SG_GROUPS = 4
EC_CAPACITY_FACTOR = 2
ROPE_THETA = 10000.0
ROPE_FREQS = HEAD_DIM // 4
EPS = 1e-6
NEG_INF = -1e30
PROJ_SIZES = (256, 256, 256, 256, 128, 128, 256, 128, 128, 256, 256)
PROJ_OFFS = tuple(int(v) for v in np.cumsum((0,) + PROJ_SIZES))
LOG2E = 1.4426950408889634
Q_SCALE = HEAD_DIM ** -0.5 * LOG2E

LANES = 128
VMEM_LIMIT = 56 * 2 ** 20
ATT_BLOCK = 256
MXU_DTYPE = jnp.bfloat16
ACT_DTYPE = jnp.bfloat16
F32 = jnp.float32
HI = lax.Precision.HIGHEST
NT_DIMS = (((1,), (1,)), ((), ()))
TN_DIMS = (((0,), (0,)), ((), ()))


def _params(*sem):
    return pltpu.CompilerParams(dimension_semantics=sem, vmem_limit_bytes=VMEM_LIMIT)


def _mm(a, b):
    return jnp.dot(a.astype(MXU_DTYPE), b.astype(MXU_DTYPE), preferred_element_type=F32)


def _rms(x, g):
    return x * lax.rsqrt(jnp.mean(x * x, axis=-1, keepdims=True) + EPS) * g


def _silu(x):
    return x / (1.0 + jnp.exp(-x))


ROW_TILE = 8


def _read_row_tiles(ref, lead, n_rows):
    return jnp.concatenate([ref[lead + (pl.ds(a, n_rows, stride=ROW_TILE), slice(None))]
                            for a in range(ROW_TILE)], axis=-1)


def _write_row_tiles(ref, lead, val):
    for a in range(ROW_TILE):
        ref[lead + (pl.ds(a, val.shape[0], stride=ROW_TILE), slice(None))] = val[:, a * LANES:(a + 1) * LANES]


def _mod_body(c_ref, w_ref, b_ref, o_ref):
    s = _silu(c_ref[...])
    o_ref[0] = jnp.dot(s, w_ref[0], precision=HI, preferred_element_type=F32) + b_ref[0]


def _adaln_all(cvecs, w_mod, b_mod):
    depth, d, d6 = w_mod.shape
    tn = 1536
    rows = cvecs.shape[0]
    return pl.pallas_call(
        _mod_body,
        out_shape=jax.ShapeDtypeStruct((depth, rows, d6), F32),
        grid=(depth, d6 // tn),
        in_specs=[pl.BlockSpec((rows, d), lambda l, j: (0, 0)),
                  pl.BlockSpec((1, d, tn), lambda l, j: (l, 0, j)),
                  pl.BlockSpec((1, 1, tn), lambda l, j: (l, 0, j))],
        out_specs=pl.BlockSpec((1, rows, tn), lambda l, j: (l, 0, j)),
        compiler_params=_params("parallel", "parallel"),
        name="adaln",
    )(cvecs, w_mod, b_mod.reshape(depth, 1, d6))


def _head_rms(t, g, gsum_ref):
    w = t.shape[-1]
    sq = t * t
    hi = sq.astype(MXU_DTYPE)
    lo = (sq - hi.astype(F32)).astype(MXU_DTYPE)
    gs = gsum_ref[0:w, 0:w]
    ss = jnp.dot(hi, gs, preferred_element_type=F32) + jnp.dot(lo, gs, preferred_element_type=F32)
    return t * lax.rsqrt(ss * (1.0 / HEAD_DIM) + EPS) * g


def _rope(t, cos, sin_signed):
    w = t.shape[-1]
    rep = w // LANES
    if rep > 1:
        cos = jnp.concatenate([cos] * rep, axis=-1)
        sin_signed = jnp.concatenate([sin_signed] * rep, axis=-1)
    lane = lax.broadcasted_iota(jnp.int32, t.shape, 1)
    first_half = (lane % (2 * ROPE_FREQS)) < ROPE_FREQS
    partner = jnp.where(first_half, pltpu.roll(t, w - ROPE_FREQS, 1), pltpu.roll(t, ROPE_FREQS, 1))
    return t * cos + partner * sin_signed


def _with_ones_lane(v):
    lane = lax.broadcasted_iota(jnp.int32, (v.shape[0], LANES - HEAD_DIM), 1)
    pad = jnp.where(lane == 0, 1.0, 0.0).astype(v.dtype)
    parts = []
    for h in range(v.shape[1] // HEAD_DIM):
        parts += [v[:, h * HEAD_DIM:(h + 1) * HEAD_DIM], pad]
    return jnp.concatenate(parts, axis=-1)


def _inproj_body(x_ref, nw_ref, sh_ref, sc_ref, w_ref, cos_ref, sin_ref, qn_ref, kn_ref, gn_ref,
                 ws_ref, bs_ref, gsum_ref,
                 qa_ref, ka_ref, va_ref, qb_ref, kb_ref, vb_ref, qs_ref, ks_ref, vs_ref, yd_ref):
    x = x_ref[0]
    h = _rms(x, nw_ref[...]) * (1.0 + sc_ref[0]) + sh_ref[0]
    p = _mm(h, w_ref[...])
    o = PROJ_OFFS
    cos, sin = cos_ref[...], sin_ref[...]
    dt = qa_ref.dtype
    qa_ref[0] = (p[:, o[0]:o[1]] * Q_SCALE).astype(dt)
    ka_ref[0] = p[:, o[1]:o[2]].astype(dt)
    va_ref[0] = _with_ones_lane(p[:, o[2]:o[3]].astype(dt))
    qb = _rope(_head_rms(p[:, o[3]:o[4]], qn_ref[...], gsum_ref), cos, sin)
    qb_ref[0] = (qb * Q_SCALE).astype(dt)
    kb_ref[0] = _rope(_head_rms(p[:, o[4]:o[5]], kn_ref[...], gsum_ref), cos, sin).astype(dt)
    vb_ref[0] = _with_ones_lane(p[:, o[5]:o[6]].astype(dt))
    qs_ref[0] = (_rope(p[:, o[6]:o[7]], cos, sin) * Q_SCALE).astype(dt)
    ks_ref[0] = _rope(p[:, o[7]:o[8]], cos, sin).astype(dt)
    vs_ref[0] = _with_ones_lane(p[:, o[8]:o[9]].astype(dt))
    u = jax.nn.gelu(p[:, o[9]:o[10]])
    v = _rms(jax.nn.gelu(p[:, o[10]:o[11]]), gn_ref[...]).astype(MXU_DTYPE)
    lane_group = lax.broadcasted_iota(jnp.int32, (CHUNK, MIXER_WIDTH), 1) // (MIXER_WIDTH // SG_GROUPS)
    for c in range(x.shape[0] // CHUNK):
        rows = slice(c * CHUNK, (c + 1) * CHUNK)
        mixed = bs_ref[...]
        for g in range(SG_GROUPS):
            mg = jnp.dot(ws_ref[g], v[rows], preferred_element_type=F32)
            mixed = mixed + jnp.where(lane_group == g, mg, 0.0)
        yd_ref[0, rows, :] = (u[rows] * mixed).astype(dt)


def _inproj(x, nw, shift, scale, lw, cos, sin, tm):
    b, t, d = x.shape
    widths = tuple(w * (LANES // HEAD_DIM if j % 3 == 2 else 1) for j, w in enumerate(PROJ_SIZES[:9]))
    widths += (MIXER_WIDTH,)
    row = lambda bi, i: (0, 0)
    per_b = lambda bi, i: (bi, 0, 0)
    tile = lambda bi, i: (bi, i, 0)
    return pl.pallas_call(
        _inproj_body,
        out_shape=[jax.ShapeDtypeStruct((b, t, w), ACT_DTYPE) for w in widths],
        grid=(b, t // tm),
        in_specs=[pl.BlockSpec((1, tm, d), tile),
                  pl.BlockSpec((1, d), row),
                  pl.BlockSpec((1, 1, d), per_b),
                  pl.BlockSpec((1, 1, d), per_b),
                  pl.BlockSpec(lw["w_in"].shape, row),
                  pl.BlockSpec((tm, LANES), lambda bi, i: (i, 0)),
                  pl.BlockSpec((tm, LANES), lambda bi, i: (i, 0)),
                  pl.BlockSpec((1, 256), row),
                  pl.BlockSpec((1, 128), row),
                  pl.BlockSpec((1, 256), row),
                  pl.BlockSpec((SG_GROUPS, CHUNK, CHUNK), lambda bi, i: (0, 0, 0)),
                  pl.BlockSpec((CHUNK, MIXER_WIDTH), row),
                  pl.BlockSpec((256, 256), row)],
        out_specs=[pl.BlockSpec((1, tm, w), tile) for w in widths],
        compiler_params=_params("parallel", "parallel"),
        name="inproj",
    )(x, nw, shift, scale, lw["w_in"], cos, sin, lw["qn"], lw["kn"], lw["gn"], lw["w_sgu"],
      lw["b_sgu"], lw["gsum"])


def _attend_all(jobs):
    staged = []
    for q, chunks, sink in jobs:
        scores = []
        for k, _, bias in chunks:
            s = lax.dot_general(q, k, NT_DIMS, preferred_element_type=F32)
            scores.append(s if bias is None else s + bias)
        m = jnp.max(functools.reduce(jnp.maximum, scores), axis=-1, keepdims=True)
        staged.append((scores, m if sink is None else jnp.maximum(m, sink)))
    outs = []
    for (q, chunks, sink), (scores, m) in zip(jobs, staged):
        acc = jnp.zeros((q.shape[0], LANES), F32)
        for s, (_, v, _) in zip(scores, chunks):
            acc = acc + jnp.dot(jnp.exp2(s - m).astype(v.dtype), v, preferred_element_type=F32)
        l = acc[:, HEAD_DIM:HEAD_DIM + 1]
        if sink is not None:
            l = l + jnp.exp2(sink - m)
        outs.append(acc[:, :HEAD_DIM] / l)
    return outs


def _head(ref, h, width=HEAD_DIM):
    return ref[0, :, h * width:(h + 1) * width]


def _local_attn_body(*refs, group, has_sink):
    if has_sink:
        sink_ref, refs = refs[0], refs[1:]
    q_ref, k0, k1, k2, v0, v1, v2, kc_ref, vc_ref, bias_ref, o_ref = refs
    jobs = []
    for h in range(q_ref.shape[-1] // HEAD_DIM):
        kv = h // group
        hb = h if bias_ref.shape[1] > 1 else 0
        chunks = [(_head(kr, kv), _head(vr, kv, LANES), bias_ref[0, hb, j])
                  for j, (kr, vr) in enumerate(((k0, v0), (k1, v1), (k2, v2)))]
        chunks.append((_head(kc_ref, kv), _head(vc_ref, kv, LANES), None))
        jobs.append((_head(q_ref, h), chunks, sink_ref[h] if has_sink else None))
    o_ref[0] = jnp.concatenate(_attend_all(jobs), axis=-1).astype(o_ref.dtype)


def _local_attn(q, k, v, kc, vc, bias, sink=None):
    b, n, qw = q.shape
    kw = k.shape[-1]
    nc = kc.shape[1]
    tq = ATT_BLOCK
    nblk = n // tq
    assert nblk >= 4
    group = qw // kw
    hb = bias.shape[1]

    def kmap(j):
        return lambda bi, i: (bi, jnp.clip(i - 1, 0, nblk - 3) + j, 0)

    def bmap(bi, i):
        return (jnp.where(i == 0, 0, jnp.where(i == nblk - 1, 2, 1)), 0, 0, 0, 0)

    in_specs = [pl.BlockSpec((1, tq, qw), lambda bi, i: (bi, i, 0))]
    vw = v.shape[-1]
    assert nc == tq
    in_specs += [pl.BlockSpec((1, tq, kw), kmap(j)) for j in range(3)]
    in_specs += [pl.BlockSpec((1, tq, vw), kmap(j)) for j in range(3)]
    in_specs += [pl.BlockSpec((1, nc, kw), lambda bi, i: (bi, 0, 0)),
                 pl.BlockSpec((1, nc, vw), lambda bi, i: (bi, 0, 0))]
    in_specs += [pl.BlockSpec((1, hb, 3, tq, tq), bmap)]
    args = [q, k, k, k, v, v, v, kc, vc, bias]
    if sink is not None:
        in_specs = [pl.BlockSpec(memory_space=pltpu.SMEM)] + in_specs
        args = [sink] + args
    return pl.pallas_call(
        functools.partial(_local_attn_body, group=group, has_sink=sink is not None),
        out_shape=jax.ShapeDtypeStruct((b, n, qw), ACT_DTYPE),
        grid=(b, nblk),
        in_specs=in_specs,
        out_specs=pl.BlockSpec((1, tq, qw), lambda bi, i: (bi, i, 0)),
        compiler_params=_params("parallel", "parallel"),
        name="local_attn",
    )(*args)


def _ctx_attn_body(sink_ref, qa, ka, va, qb, kb, vb, qs, ks, vs, oa, ob, oc):
    for q_ref, k_ref, v_ref, o_ref, group, use_sink in (
            (qa, ka, va, oa, 1, False), (qb, kb, vb, ob, 2, False), (qs, ks, vs, oc, 2, True)):
        jobs = [(_head(q_ref, h), [(_head(k_ref, h // group), _head(v_ref, h // group, LANES), None)],
                 sink_ref[h] if use_sink else None) for h in range(q_ref.shape[-1] // HEAD_DIM)]
        o_ref[0] = jnp.concatenate(_attend_all(jobs), axis=-1).astype(o_ref.dtype)


def _ctx_attn(sink, qkv):
    b, nc, _ = qkv[0].shape
    spec = lambda a: pl.BlockSpec((1, nc, a.shape[-1]), lambda bi: (bi, 0, 0))
    return pl.pallas_call(
        _ctx_attn_body,
        out_shape=[jax.ShapeDtypeStruct((b, nc, MIXER_WIDTH), ACT_DTYPE)] * 3,
        grid=(b,),
        in_specs=[pl.BlockSpec(memory_space=pltpu.SMEM)] + [spec(a) for a in qkv],
        out_specs=[pl.BlockSpec((1, nc, MIXER_WIDTH), lambda bi: (bi, 0, 0))] * 3,
        compiler_params=_params("parallel"),
        name="ctx_attn",
    )(sink, *qkv)


def _global_attn_body(q_ref, kt_ref, v_ref, o_ref, s_a, s_b, *, tk):
    tq = q_ref.shape[1]
    n_kv = kt_ref.shape[1] // HEAD_DIM
    n_chunks = kt_ref.shape[2] // tk
    group_w = 2 * HEAD_DIM
    qs = [jnp.concatenate([q_ref[0, :, kv * group_w:kv * group_w + HEAD_DIM],
                           q_ref[0, :, kv * group_w + HEAD_DIM:(kv + 1) * group_w]], axis=0)
          for kv in range(n_kv)]

    def scores(i, s_ref):
        ks = pl.multiple_of(i * tk, tk)
        for kv in range(n_kv):
            s_ref[kv] = jnp.dot(qs[kv], kt_ref[0, kv * HEAD_DIM:(kv + 1) * HEAD_DIM, pl.ds(ks, tk)],
                                preferred_element_type=F32)

    def update(i, s_ref, carry):
        ks = pl.multiple_of(i * tk, tk)
        out = []
        for kv in range(n_kv):
            m, acc = carry[kv]
            s = s_ref[kv]
            m_new = jnp.maximum(m, jnp.max(s, axis=-1, keepdims=True))
            p = jnp.exp2(s - m_new).astype(v_ref.dtype)
            pv = jnp.dot(p, v_ref[0, pl.ds(ks, tk), kv * LANES:(kv + 1) * LANES], preferred_element_type=F32)
            out.append((m_new, jnp.exp2(m - m_new) * acc + pv))
        return tuple(out)

    carry = tuple((jnp.full((2 * tq, 1), NEG_INF, F32), jnp.zeros((2 * tq, LANES), F32)) for _ in range(n_kv))
    scores(0, s_a)

    def pair(j, carry):
        scores(2 * j + 1, s_b)
        carry = update(2 * j, s_a, carry)
        scores(2 * j + 2, s_a)
        return update(2 * j + 1, s_b, carry)

    carry = lax.fori_loop(0, (n_chunks - 1) // 2, pair, carry)
    if n_chunks % 2 == 0:
        scores(n_chunks - 1, s_b)
        carry = update(n_chunks - 2, s_a, carry)
        carry = update(n_chunks - 1, s_b, carry)
    else:
        carry = update(n_chunks - 1, s_a, carry)
    outs = []
    for _, acc in carry:
        o = acc[:, :HEAD_DIM] / acc[:, HEAD_DIM:HEAD_DIM + 1]
        outs += [o[:tq], o[tq:]]
    o_ref[0] = jnp.concatenate(outs, axis=-1).astype(o_ref.dtype)


def _global_attn(q, k, v):
    b, n, qw = q.shape
    nk, kw = k.shape[1:]
    n_kv = kw // HEAD_DIM
    tq = ATT_BLOCK
    tk = 1280 if (nk % 1280 == 0 and nk > 1280) else 256
    kt = jnp.swapaxes(k, 1, 2)
    return pl.pallas_call(
        functools.partial(_global_attn_body, tk=tk),
        out_shape=jax.ShapeDtypeStruct((b, n, qw), ACT_DTYPE),
        grid=(b, n // tq),
        in_specs=[pl.BlockSpec((1, tq, qw), lambda bi, i: (bi, i, 0)),
                  pl.BlockSpec((1, kw, nk), lambda bi, i: (bi, 0, 0)),
                  pl.BlockSpec((1, nk, n_kv * LANES), lambda bi, i: (bi, 0, 0))],
        out_specs=pl.BlockSpec((1, tq, qw), lambda bi, i: (bi, i, 0)),
        scratch_shapes=[pltpu.VMEM((n_kv, 2 * tq, tk), F32), pltpu.VMEM((n_kv, 2 * tq, tk), F32)],
        compiler_params=_params("parallel", "parallel"),
        name="global_attn",
    )(q, kt, v)


def _merge_body(ya, yb, yc, yd, x_ref, on_ref, wo_ref, g1_ref, nf_ref, sh_ref, sc_ref, wr_ref,
                xo_ref, h_ref, aff_ref):
    parts = []
    for j, r in enumerate((ya, yb, yc, yd)):
        y = r[0].astype(F32)
        parts.append(_rms(y, on_ref[:, j * MIXER_WIDTH:(j + 1) * MIXER_WIDTH]).astype(MXU_DTYPE))
    xn = x_ref[0] + g1_ref[0] * _mm(jnp.concatenate(parts, axis=-1), wo_ref[...])
    xo_ref[0] = xn
    h = _rms(xn, nf_ref[...]) * (1.0 + sc_ref[0]) + sh_ref[0]
    _write_row_tiles(h_ref, (0,), h)
    logits = lax.dot_general(wr_ref[...], h, NT_DIMS, precision=HI, preferred_element_type=F32)
    e = jnp.exp(logits - jnp.max(logits, axis=0, keepdims=True))
    aff_ref[0] = e / jnp.sum(e, axis=0, keepdims=True)


def _merge(ys, x, lw, g1, shift, scale, tm):
    b, t, d = x.shape
    ne = lw["w_router_t"].shape[0]
    row = lambda bi, i: (0, 0)
    per_b = lambda bi, i: (bi, 0, 0)
    tile = lambda bi, i: (bi, i, 0)
    return pl.pallas_call(
        _merge_body,
        out_shape=[jax.ShapeDtypeStruct((b, t, d), F32), jax.ShapeDtypeStruct((b, t * ROW_TILE, LANES), F32),
                   jax.ShapeDtypeStruct((b, ne, t), F32)],
        grid=(b, t // tm),
        in_specs=[pl.BlockSpec((1, tm, MIXER_WIDTH), tile)] * 4 + [
            pl.BlockSpec((1, tm, d), tile),
            pl.BlockSpec((1, d), row),
            pl.BlockSpec((d, d), row),
            pl.BlockSpec((1, 1, d), per_b),
            pl.BlockSpec((1, d), row),
            pl.BlockSpec((1, 1, d), per_b),
            pl.BlockSpec((1, 1, d), per_b),
            pl.BlockSpec((ne, d), row)],
        out_specs=[pl.BlockSpec((1, tm, d), tile), pl.BlockSpec((1, tm * ROW_TILE, LANES), tile),
                   pl.BlockSpec((1, ne, tm), lambda bi, i: (bi, 0, i))],
        compiler_params=_params("parallel", "parallel"),
        name="merge",
    )(*ys, x, lw["out_norm"], lw["w_out"], g1, lw["norm_ffn"], shift, scale, lw["w_router_t"])


def _tri(n, m, mode):
    r = lax.broadcasted_iota(jnp.int32, (n, m), 0)
    c = lax.broadcasted_iota(jnp.int32, (n, m), 1)
    return jnp.where({"lt": r < c, "le": r <= c, "gt": r > c}[mode], 1.0, 0.0).astype(MXU_DTYPE)


def _count(mask, axes):
    out = jnp.where(mask, 1.0, 0.0)
    for ax in sorted(axes, reverse=True):
        out = jnp.sum(out, axis=ax, keepdims=True)
    return out


def _kth_largest_bits(bits, cap, axes):
    shape = tuple(1 if a in axes else s for a, s in enumerate(bits.shape))

    def body(i, t):
        cand = t | lax.shift_left(jnp.int32(1), 30 - i)
        return jnp.where(_count(bits >= cand, axes) >= cap, cand, t)

    return lax.fori_loop(0, 31, body, jnp.zeros(shape, jnp.int32))


def _prefix_tokens(m, exact_rows):
    e, r, l = m.shape
    m2 = m.reshape(e * r, l).astype(MXU_DTYPE)
    local = jnp.dot(m2, _tri(l, l, "lt"), preferred_element_type=F32)
    rowtot = jnp.dot(m2, jnp.ones((l, l), MXU_DTYPE), preferred_element_type=F32).reshape(e, r, l)
    below = _tri(r, r, "gt")
    if exact_rows:
        base = [jnp.dot(below, rowtot[i].astype(MXU_DTYPE), preferred_element_type=F32) for i in range(e)]
    else:
        base = [jnp.dot(below.astype(F32), rowtot[i], precision=HI, preferred_element_type=F32)
                for i in range(e)]
    return local.reshape(e, r, l) + jnp.stack(base, axis=0)


def _select_mask(aff, cap, prefix_fn, axes):
    bits = pltpu.bitcast(aff, jnp.int32)
    thr = _kth_largest_bits(bits, cap, axes)
    gt = bits > thr
    eq = bits == thr
    need = cap - _count(gt, axes)
    eq_rank = prefix_fn(jnp.where(eq, 1.0, 0.0))
    take_eq = jnp.where(eq, jnp.where(eq_rank < need, 1.0, 0.0), 0.0)
    return jnp.where(gt, 1.0, take_eq)


def _select_body(aff_ref, sel_ref, prank_ref, tstart_ref, tend_ref, *, cap):
    aff = aff_ref[0]
    ne = aff.shape[0]
    sel = _select_mask(aff, cap, functools.partial(_prefix_tokens, exact_rows=True), (1, 2))
    sel_ref[0] = sel
    cnt = jnp.sum(sel, axis=0)
    tstart = _prefix_tokens(cnt[None], exact_rows=False)[0]
    tstart_ref[0] = tstart.astype(jnp.int32)
    tend_ref[0] = (tstart + cnt).astype(jnp.int32)
    run = tstart
    for e in range(ne):
        prank_ref[0, e] = run.astype(jnp.int32)
        run = run + sel[e]


def _select(aff4, cap):
    b, ne, r, l = aff4.shape
    blk4 = pl.BlockSpec((1, ne, r, l), lambda bi: (bi, 0, 0, 0))
    blk3 = pl.BlockSpec((1, r, l), lambda bi: (bi, 0, 0))
    return pl.pallas_call(
        functools.partial(_select_body, cap=cap),
        out_shape=[jax.ShapeDtypeStruct((b, ne, r, l), F32), jax.ShapeDtypeStruct((b, ne, r, l), jnp.int32),
                   jax.ShapeDtypeStruct((b, r, l), jnp.int32), jax.ShapeDtypeStruct((b, r, l), jnp.int32)],
        grid=(b,),
        in_specs=[blk4],
        out_specs=[blk4, blk4, blk3, blk3],
        compiler_params=_params("parallel"),
        name="moe_select",
    )(aff4)


def _slots_body(sel_ref, aff_ref, prank_ref, idx_ref, dest_ref, gate_ref):
    m = sel_ref[0, 0]
    r, l = m.shape
    cap = idx_ref.shape[2]
    mb = m.astype(MXU_DTYPE)
    linc = jnp.dot(mb, _tri(l, l, "le"), preferred_element_type=F32)
    rowtot = jnp.dot(mb, jnp.ones((l, l), MXU_DTYPE), preferred_element_type=F32)
    rowtot_lane = lax.dot_general(jnp.ones((8, l), MXU_DTYPE), mb, NT_DIMS, preferred_element_type=F32)
    cumrow = jnp.dot(rowtot_lane.astype(MXU_DTYPE), _tri(r, r, "le"), preferred_element_type=F32)[0:1]
    slot = lax.broadcasted_iota(jnp.int32, (cap, r), 0).astype(F32)
    passed = jnp.where(cumrow <= slot, 1.0, 0.0).astype(MXU_DTYPE)
    row_of = jnp.dot(passed, jnp.ones((r, l), MXU_DTYPE), preferred_element_type=F32)[:, 0:1]
    base_of = jnp.dot(passed, rowtot.astype(MXU_DTYPE), preferred_element_type=F32)[:, 0:1]
    onehot = jnp.where(lax.broadcasted_iota(jnp.int32, (cap, r), 1).astype(F32) == row_of, 1.0, 0.0)
    linc_of = jnp.dot(onehot.astype(MXU_DTYPE), linc.astype(MXU_DTYPE), preferred_element_type=F32)
    k = slot[:, 0:1] - base_of
    col_of = jnp.sum(jnp.where(linc_of <= k, 1.0, 0.0), axis=-1, keepdims=True)
    idx_ref[0, 0] = (row_of * l + col_of).astype(jnp.int32)
    at_col = lax.broadcasted_iota(jnp.int32, (cap, l), 1).astype(F32) == col_of
    aff_rows = jnp.dot(onehot, aff_ref[0, 0], precision=HI, preferred_element_type=F32)
    gate_ref[0, 0] = jnp.sum(jnp.where(at_col, aff_rows, 0.0), axis=-1, keepdims=True)
    prank_rows = jnp.dot(onehot, prank_ref[0, 0].astype(F32), precision=HI, preferred_element_type=F32)
    dest_ref[0, 0] = jnp.sum(jnp.where(at_col, prank_rows, 0.0), axis=-1, keepdims=True).astype(jnp.int32)


def _slots(sel, aff4, prank, cap):
    b, ne, r, l = sel.shape
    blk = pl.BlockSpec((1, 1, r, l), lambda bi, e: (bi, e, 0, 0))
    oblk = pl.BlockSpec((1, 1, cap, 1), lambda bi, e: (bi, e, 0, 0))
    return pl.pallas_call(
        _slots_body,
        out_shape=[jax.ShapeDtypeStruct((b, ne, cap, 1), jnp.int32), jax.ShapeDtypeStruct((b, ne, cap, 1), jnp.int32),
                   jax.ShapeDtypeStruct((b, ne, cap, 1), F32)],
        grid=(b, ne),
        in_specs=[blk, blk, blk],
        out_specs=[oblk, oblk, oblk],
        compiler_params=_params("parallel", "parallel"),
        name="moe_slots",
    )(sel, aff4, prank)


FFN_CHUNK = 256


def _ffn_body(idx_ref, dest_ref, h_hbm, gate_ref, wg_ref, wu_ref, wd_ref, z_hbm, xa, xb, ya, yb, sems,
              *, n_tok, n_pair):
    n_b, tiles = pl.num_programs(1), pl.num_programs(2)
    step = (pl.program_id(0) * n_b + pl.program_id(1)) * tiles + pl.program_id(2)
    last = pl.num_programs(0) * n_b * tiles - 1
    ts = xa.shape[0] // ROW_TILE
    n_groups = wg_ref.shape[2] // FFN_CHUNK
    per_group = ts // n_groups

    def sample_of(k):
        return (k // tiles) % n_b

    def tile_of(ref, r):
        return ref.at[pl.ds(pl.multiple_of(r * ROW_TILE, ROW_TILE), ROW_TILE), :]

    def gather(k, half, buf, sem):
        base, rows = (2 * k + half) * ts, sample_of(k) * n_tok
        return lambda s: pltpu.make_async_copy(
            tile_of(h_hbm, rows + idx_ref[base + s]), tile_of(buf, s), sems.at[sem])

    def scatter(k, half, buf, sem):
        base, rows = (2 * k + half) * ts, sample_of(k) * n_pair
        return lambda s: pltpu.make_async_copy(
            tile_of(buf, s), tile_of(z_hbm, rows + dest_ref[base + s]), sems.at[sem])

    def start_all(copy):
        def body(s, c):
            copy(s).start()
            return c
        lax.fori_loop(0, ts, body, 0, unroll=8)

    def wait_rows(buf, sem):
        pltpu.make_async_copy(buf, buf, sems.at[sem]).wait()

    def ffn(xbuf, gate, copies):
        x = _read_row_tiles(xbuf, (), ts).astype(MXU_DTYPE)
        y = None
        for j in range(n_groups):
            for copy in copies:
                for s in range(j * per_group, (j + 1) * per_group):
                    copy(s).start()
            cols = slice(j * FFN_CHUNK, (j + 1) * FFN_CHUNK)
            hid = _silu(_mm(x, wg_ref[0, :, cols])) * _mm(x, wu_ref[0, :, cols])
            part = _mm(hid, wd_ref[0, cols, :])
            y = part if y is None else y + part
        return y * gate

    @pl.when(step == 0)
    def _():
        start_all(gather(step, 0, xa, 0))

    wait_rows(xa, 0)
    y_a = ffn(xa, gate_ref[0, 0, 0:ts], [gather(step, 1, xb, 1)])

    @pl.when(step > 0)
    def _():
        wait_rows(ya, 2)
        wait_rows(yb, 3)

    _write_row_tiles(ya, (), y_a)
    wait_rows(xb, 1)
    nxt = jnp.minimum(step + 1, last)
    y_b = ffn(xb, gate_ref[0, 0, ts:2 * ts], [scatter(step, 0, ya, 2), gather(nxt, 0, xa, 0)])
    _write_row_tiles(yb, (), y_b)
    start_all(scatter(step, 1, yb, 3))

    @pl.when(step == last)
    def _():
        wait_rows(xa, 0)
        wait_rows(ya, 2)
        wait_rows(yb, 3)


def _expert_ffn(idx, dest, gate, h, lw, ts):
    b, ne, cap, _ = gate.shape
    n, d = h.shape[1] // ROW_TILE, ROW_TILE * LANES
    f = lw["w_gate"].shape[-1]
    layer = lw["layer"]
    n_pair = ne * cap
    by_expert = lambda a: jnp.swapaxes(a, 0, 1)
    grid_spec = pltpu.PrefetchScalarGridSpec(
        num_scalar_prefetch=2,
        grid=(ne, b, cap // (2 * ts)),
        in_specs=[pl.BlockSpec(memory_space=pl.ANY),
                  pl.BlockSpec((1, 1, 2 * ts, 1), lambda e, bi, t, *_: (e, bi, t, 0)),
                  pl.BlockSpec((None, 1, d, f), lambda e, bi, t, *_: (layer, e, 0, 0)),
                  pl.BlockSpec((None, 1, d, f), lambda e, bi, t, *_: (layer, e, 0, 0)),
                  pl.BlockSpec((None, 1, f, d), lambda e, bi, t, *_: (layer, e, 0, 0))],
        out_specs=pl.BlockSpec(memory_space=pl.ANY),
        scratch_shapes=[pltpu.VMEM((ts * ROW_TILE, LANES), F32)] * 4 + [pltpu.SemaphoreType.DMA((4,))])
    return pl.pallas_call(
        functools.partial(_ffn_body, n_tok=n, n_pair=n_pair),
        out_shape=jax.ShapeDtypeStruct((b * n_pair * ROW_TILE, LANES), F32),
        grid_spec=grid_spec,
        compiler_params=_params("arbitrary", "arbitrary", "arbitrary"),
        name="moe_ffn",
    )(by_expert(idx).reshape(-1), by_expert(dest).reshape(-1), h.reshape(b * n * ROW_TILE, LANES), by_expert(gate),
      lw["w_gate"], lw["w_up"], lw["w_down"])


COMBINE_DEPTH = 4


def _combine_body(tb_ref, x_ref, g2_ref, ts_ref, te_ref, z_hbm, o_ref, zbuf, acc_ref, sem, *, n_pair):
    n_tiles = pl.num_programs(1)
    tile = pl.program_id(0) * n_tiles + pl.program_id(1)
    depth, pc = zbuf.shape[0], zbuf.shape[1] // ROW_TILE
    tt = x_ref.shape[1]

    def plan(g):
        b, i = g // n_tiles, g % n_tiles
        lo, hi = tb_ref[b * (n_tiles + 1) + i], tb_ref[b * (n_tiles + 1) + i + 1]
        p0 = (lo // 8) * 8
        return b, p0, (hi - p0 + pc - 1) // pc

    def fetch(b, p0, k, slot):
        cs = pl.multiple_of(jnp.minimum(p0 + k * pc, n_pair - pc), 8)
        rows = pl.ds(pl.multiple_of((b * n_pair + cs) * ROW_TILE, 8 * ROW_TILE), pc * ROW_TILE)
        return cs, pltpu.make_async_copy(z_hbm.at[rows, :], zbuf.at[slot], sem.at[slot])

    b, p0, n_chunks = plan(tile)
    start_row, end_row = ts_ref[0, 0], te_ref[0, 0]
    acc_ref[...] = jnp.zeros_like(acc_ref)

    def start_head(b, p0, n_chunks):
        for k in range(depth - 1):
            @pl.when(k < n_chunks)
            def _():
                fetch(b, p0, k, k)[1].start()

    @pl.when(tile == 0)
    def _():
        start_head(b, p0, n_chunks)

    def chunk(k, c):
        slot = k % depth
        ahead = k + depth - 1

        @pl.when(ahead < n_chunks)
        def _():
            fetch(b, p0, ahead, ahead % depth)[1].start()

        cs, cp = fetch(b, p0, k, slot)
        cp.wait()
        pair = cs + lax.broadcasted_iota(jnp.int32, (pc, tt), 0)
        own = jnp.where(pair >= jnp.maximum(start_row, p0 + k * pc), jnp.where(pair < end_row, 1.0, 0.0), 0.0)
        own = own.astype(MXU_DTYPE)
        z = _read_row_tiles(zbuf, (slot,), pc).astype(MXU_DTYPE)
        acc_ref[...] += lax.dot_general(own, z, TN_DIMS, preferred_element_type=F32)
        return c

    lax.fori_loop(0, n_chunks, chunk, 0)

    @pl.when(tile + 1 < pl.num_programs(0) * n_tiles)
    def _():
        start_head(*plan(tile + 1))

    o_ref[0] = x_ref[0] + g2_ref[0] * acc_ref[...]


def _combine(x, g2, tstart, tend, z, n_pair, tt, pc):
    b, n, d = x.shape
    nt = n // tt
    ts4 = tstart.reshape(b, nt, 1, tt)
    te4 = tend.reshape(b, nt, 1, tt)
    bounds = jnp.concatenate([ts4[:, :, 0, 0], jnp.full((b, 1), n_pair, jnp.int32)], axis=1).reshape(-1)
    grid_spec = pltpu.PrefetchScalarGridSpec(
        num_scalar_prefetch=1,
        grid=(b, nt),
        in_specs=[pl.BlockSpec((1, tt, d), lambda bi, i, *_: (bi, i, 0)),
                  pl.BlockSpec((1, 1, d), lambda bi, i, *_: (bi, 0, 0)),
                  pl.BlockSpec((1, 1, 1, tt), lambda bi, i, *_: (bi, i, 0, 0)),
                  pl.BlockSpec((1, 1, 1, tt), lambda bi, i, *_: (bi, i, 0, 0)),
                  pl.BlockSpec(memory_space=pl.ANY)],
        out_specs=pl.BlockSpec((1, tt, d), lambda bi, i, *_: (bi, i, 0)),
        scratch_shapes=[pltpu.VMEM((COMBINE_DEPTH, pc * ROW_TILE, LANES), F32), pltpu.VMEM((tt, d), F32),
                        pltpu.SemaphoreType.DMA((COMBINE_DEPTH,))])
    return pl.pallas_call(
        functools.partial(_combine_body, n_pair=n_pair),
        out_shape=jax.ShapeDtypeStruct((b, n, d), F32),
        grid_spec=grid_spec,
        compiler_params=_params("arbitrary", "arbitrary"),
        name="moe_combine",
    )(bounds, x, g2, ts4, te4, z)


def _moe_latent(x, h, aff, g2, lw):
    b, n, d = x.shape
    ne = aff.shape[1]
    cap = EC_CAPACITY_FACTOR * n // ne
    aff4 = aff.reshape(b, ne, n // LANES, LANES)
    sel, prank, tstart, tend = _select(aff4, cap)
    idx, dest, gate = _slots(sel, aff4, prank, cap)
    z = _expert_ffn(idx, dest, gate, h, lw, min(cap // 2, 256))
    return _combine(x, g2, tstart, tend, z, ne * cap, 256, 256)


def _ctx_coef_body(aff_ref, coef_ref, *, cap):
    nb, _, nc = aff_ref.shape
    excl = _tri(nc, nc, "lt")
    outs = []
    for b in range(nb):
        aff = aff_ref[b]
        sel = _select_mask(aff, cap, lambda mm: jnp.dot(mm.astype(MXU_DTYPE), excl, preferred_element_type=F32), (1,))
        outs.append(sel * aff)
    coef_ref[:, 0, :] = jnp.concatenate(outs, axis=-1)


def _ctx_coef(aff, cap):
    b, ne, nc = aff.shape
    return pl.pallas_call(
        functools.partial(_ctx_coef_body, cap=cap),
        out_shape=jax.ShapeDtypeStruct((ne, 1, b * nc), F32),
        name="ctx_moe_select",
    )(aff)


def _ctx_ffn_body(h_ref, coef_ref, wg_ref, wu_ref, wd_ref, x_ref, g2_ref, o_ref):
    e = pl.program_id(0)
    rows = h_ref.shape[0] // ROW_TILE
    h = _read_row_tiles(h_ref, (), rows)
    y = _mm(_silu(_mm(h, wg_ref[0])) * _mm(h, wu_ref[0]), wd_ref[0])
    diag = (lax.broadcasted_iota(jnp.int32, (rows, rows), 0) == lax.broadcasted_iota(jnp.int32, (rows, rows), 1))
    coef = jnp.sum(jnp.where(diag, coef_ref[0], 0.0), axis=-1, keepdims=True)

    @pl.when(e == 0)
    def _():
        o_ref[...] = x_ref[...]

    o_ref[...] += g2_ref[...] * (coef * y)


def _ctx_moe(xc, hc, aff, g2, lw):
    b, nc, d = xc.shape
    ne = aff.shape[1]
    f = lw["w_gate"].shape[-1]
    layer = lw["layer"]
    coef = _ctx_coef(aff, EC_CAPACITY_FACTOR * nc // ne)
    rows = b * nc
    out = pl.pallas_call(
        _ctx_ffn_body,
        out_shape=jax.ShapeDtypeStruct((rows, d), F32),
        grid=(ne,),
        in_specs=[pl.BlockSpec((rows * ROW_TILE, LANES), lambda e: (0, 0)),
                  pl.BlockSpec((1, 1, rows), lambda e: (e, 0, 0)),
                  pl.BlockSpec((None, 1, d, f), lambda e: (layer, e, 0, 0)),
                  pl.BlockSpec((None, 1, d, f), lambda e: (layer, e, 0, 0)),
                  pl.BlockSpec((None, 1, f, d), lambda e: (layer, e, 0, 0)),
                  pl.BlockSpec((rows, d), lambda e: (0, 0)),
                  pl.BlockSpec((1, d), lambda e: (0, 0))],
        out_specs=pl.BlockSpec((rows, d), lambda e: (0, 0)),
        compiler_params=_params("arbitrary"),
        name="ctx_moe_ffn",
    )(hc.reshape(rows * ROW_TILE, LANES), coef, lw["w_gate"], lw["w_up"], lw["w_down"], xc.reshape(rows, d), g2)
    return out.reshape(b, nc, d)


def _final_body(x_ref, g_ref, o_ref):
    o_ref[0] = _rms(x_ref[0], g_ref[...])


def _final_norm(x, g, tm):
    b, n, d = x.shape
    return pl.pallas_call(
        _final_body,
        out_shape=jax.ShapeDtypeStruct((b, n, d), F32),
        grid=(b, n // tm),
        in_specs=[pl.BlockSpec((1, tm, d), lambda bi, i: (bi, i, 0)), pl.BlockSpec((1, d), lambda bi, i: (0, 0))],
        out_specs=pl.BlockSpec((1, tm, d), lambda bi, i: (bi, i, 0)),
        compiler_params=_params("parallel", "parallel"),
        name="final_norm",
    )(x, g)


def _rope_tables(n):
    t = np.arange(n)
    freqs = ROPE_THETA ** (-np.arange(ROPE_FREQS, dtype=np.float32) / ROPE_FREQS)
    ang_r = (t // GRID_W).astype(np.float32)[:, None] * freqs
    ang_c = (t % GRID_W).astype(np.float32)[:, None] * freqs
    cos = np.concatenate([np.cos(ang_r)] * 2 + [np.cos(ang_c)] * 2, axis=1)
    sin = np.concatenate([-np.sin(ang_r), np.sin(ang_r), -np.sin(ang_c), np.sin(ang_c)], axis=1)
    reps = LANES // HEAD_DIM
    return (jnp.asarray(np.tile(cos, (1, reps)), F32), jnp.asarray(np.tile(sin, (1, reps)), F32))


def _block_geometry(n):
    nblk = n // ATT_BLOCK
    a = np.arange(ATT_BLOCK)
    out = []
    for i in (0, 1, nblk - 1):
        j0 = int(np.clip(i - 1, 0, nblk - 3))
        out.append((i * ATT_BLOCK + a, [(j0 + j) * ATT_BLOCK + a for j in range(3)]))
    return out


def _neighbourhood_bias(rpb, n):
    rows = n // GRID_W
    win_r = min(NA_WIN_R, rows)
    blk_rows = ATT_BLOCK // GRID_W
    cols = np.arange(GRID_W)
    col_off = cols[None, :] - cols[:, None] + NA_WIN_C - 1
    col_hot = jnp.asarray(col_off[:, :, None] == np.arange(2 * NA_WIN_C - 1), F32)
    c0 = np.clip(cols - NA_WIN_C // 2, 0, GRID_W - NA_WIN_C)
    col_ok = (cols[None, :] >= c0[:, None]) & (cols[None, :] < c0[:, None] + NA_WIN_C)
    kinds = []
    for q_tok, k_chunks in _block_geometry(n):
        qr = q_tok[::GRID_W] // GRID_W
        r0 = np.clip(qr - win_r // 2, 0, rows - win_r)
        chunks = []
        for k_tok in k_chunks:
            kr = k_tok[::GRID_W] // GRID_W
            row_off = kr[None, :] - qr[:, None] + NA_WIN_R - 1
            row_hot = jnp.asarray(row_off[:, :, None] == np.arange(2 * NA_WIN_R - 1), F32)
            row_ok = (kr[None, :] >= r0[:, None]) & (kr[None, :] < r0[:, None] + win_r)
            ok = (row_ok[:, None, :, None] & col_ok[None, :, None, :]).reshape(ATT_BLOCK, ATT_BLOCK)
            vals = jnp.einsum("qkr,hrc,xyc->hqxky", row_hot, rpb, col_hot, precision=HI)
            vals = vals.reshape(-1, blk_rows * GRID_W, blk_rows * GRID_W) * LOG2E
            chunks.append(jnp.where(ok[None], vals, NEG_INF))
        kinds.append(jnp.stack(chunks, axis=1))
    return jnp.stack(kinds, axis=0)


def _window_mask(n):
    kinds = []
    for q_tok, k_chunks in _block_geometry(n):
        kinds.append(np.stack([np.where(np.abs(k_tok[None] - q_tok[:, None]) <= WINDOW, 0.0, NEG_INF)
                               for k_tok in k_chunks])[None])
    return jnp.asarray(np.stack(kinds), F32)


def kernel(x, c, ctx, c_ctx, w_mod, b_mod, norm_mix, norm_ffn, w_in, rpb, q_norm, k_norm, sink, sgu_norm, w_sgu,
           b_sgu, out_norm, w_out, w_router, w_gate, w_up, w_down, final_norm):
    depth = w_mod.shape[0]
    b, n, d = x.shape
    nc = ctx.shape[1]
    group_w = MIXER_WIDTH // SG_GROUPS

    cvecs = jnp.concatenate([c, c_ctx[None], jnp.zeros((8 - b - 1, d), F32)], axis=0)
    mods = _adaln_all(cvecs, w_mod, b_mod).reshape(depth, 8, 6, d)

    cos, sin = _rope_tables(n)
    cos_c, sin_c = jnp.ones((nc, LANES), F32), jnp.zeros((nc, LANES), F32)
    win_mask = _window_mask(n)
    blk = np.arange(256) // HEAD_DIM
    gsum = jnp.asarray(blk[:, None] == blk[None, :], MXU_DTYPE)
    wg_all, wu_all, wd_all = (w.astype(MXU_DTYPE) for w in (w_gate, w_up, w_down))

    xc = ctx
    for l in range(depth):
        ctx_needed = l < depth - 1
        lat = [mods[l, :b, j][:, None, :] for j in range(6)]
        cm = [jnp.broadcast_to(mods[l, b, j][None, None, :], (b, 1, d)) for j in range(6)]
        lw = {
            "w_in": w_in[l].astype(MXU_DTYPE),
            "qn": jnp.tile(q_norm[l], 4)[None], "kn": jnp.tile(k_norm[l], 2)[None], "gn": sgu_norm[l][None],
            "w_sgu": w_sgu[l].astype(MXU_DTYPE),
            "b_sgu": jnp.repeat(b_sgu[l].T, group_w, axis=1),
            "gsum": gsum,
            "out_norm": out_norm[l][None], "w_out": w_out[l].astype(MXU_DTYPE), "norm_ffn": norm_ffn[l][None],
            "w_router_t": w_router[l].T,
            "layer": l, "w_gate": wg_all, "w_up": wu_all, "w_down": wd_all,
        }
        nm = norm_mix[l][None]
        hp = _inproj(x, nm, lat[0], lat[1], lw, cos, sin, 512)
        cp = _inproj(xc, nm, cm[0], cm[1], lw, cos_c, sin_c, nc)
        qa, ka, va, qb, kb, vb, qs, ks, vs, yd = hp
        _, ka_c, va_c, _, kb_c, vb_c, _, ks_c, vs_c, _ = cp
        sink_l = sink[l] * LOG2E

        ya = _local_attn(qa, ka, va, ka_c, va_c, _neighbourhood_bias(rpb[l], n))
        yb = _global_attn(qb, jnp.concatenate([kb, kb_c], axis=1), jnp.concatenate([vb, vb_c], axis=1))
        yc = _local_attn(qs, ks, vs, ks_c, vs_c, win_mask, sink_l)
        x_mid, h2, aff = _merge((ya, yb, yc, yd), x, lw, lat[2], lat[3], lat[4], 512)
        x = _moe_latent(x_mid, h2, aff, lat[5], lw)

        if ctx_needed:
            ys_c = _ctx_attn(sink_l, cp[:9])
            xc_mid, hc2, aff_c = _merge((*ys_c, cp[9]), xc, lw, cm[2], cm[3], cm[4], nc)
            xc = _ctx_moe(xc_mid, hc2, aff_c, mods[l, b, 5][None], lw)
    return _final_norm(x, final_norm[None], 512)
```

```python
import functools

import numpy as np
import jax
import jax.numpy as jnp
from jax import lax
from jax.experimental import pallas as pl
from jax.experimental.pallas import tpu as pltpu

HEAD_DIM = 64
GRID_W = 64
MIXER_WIDTH = 256
NA_WIN_R = 8
NA_WIN_C = 16
WINDOW = 128
CHUNK = 128
SG_GROUPS = 4
EC_CAPACITY_FACTOR = 2
ROPE_THETA = 10000.0
ROPE_FREQS = HEAD_DIM // 4
EPS = 1e-6
NEG_INF = -1e30
PROJ_SIZES = (256, 256, 256, 256, 128, 128, 256, 128, 128, 256, 256)
PROJ_OFFS = tuple(int(v) for v in np.cumsum((0,) + PROJ_SIZES))
LOG2E = 1.4426950408889634
Q_SCALE = HEAD_DIM ** -0.5 * LOG2E

LANES = 128
VMEM_LIMIT = 56 * 2 ** 20
ATT_BLOCK = 256
MXU_DTYPE = jnp.bfloat16
ACT_DTYPE = jnp.bfloat16
F32 = jnp.float32
HI = lax.Precision.HIGHEST
NT_DIMS = (((1,), (1,)), ((), ()))
TN_DIMS = (((0,), (0,)), ((), ()))


def _params(*sem):
    return pltpu.CompilerParams(dimension_semantics=sem, vmem_limit_bytes=VMEM_LIMIT)


def _mm(a, b):
    return jnp.dot(a.astype(MXU_DTYPE), b.astype(MXU_DTYPE), preferred_element_type=F32)


def _rms(x, g):
    return x * lax.rsqrt(jnp.mean(x * x, axis=-1, keepdims=True) + EPS) * g


def _silu(x):
    return x / (1.0 + jnp.exp(-x))


ROW_TILE = 8


def _read_row_tiles(ref, lead, n_rows):
    return jnp.concatenate([ref[lead + (pl.ds(a, n_rows, stride=ROW_TILE), slice(None))]
                            for a in range(ROW_TILE)], axis=-1)


def _write_row_tiles(ref, lead, val):
    for a in range(ROW_TILE):
        ref[lead + (pl.ds(a, val.shape[0], stride=ROW_TILE), slice(None))] = val[:, a * LANES:(a + 1) * LANES]


def _mod_body(c_ref, w_ref, b_ref, o_ref):
    s = _silu(c_ref[...])
    o_ref[0] = jnp.dot(s, w_ref[0], precision=HI, preferred_element_type=F32) + b_ref[0]


def _adaln_all(cvecs, w_mod, b_mod):
    depth, d, d6 = w_mod.shape
    tn = 1536
    rows = cvecs.shape[0]
    return pl.pallas_call(
        _mod_body,
        out_shape=jax.ShapeDtypeStruct((depth, rows, d6), F32),
        grid=(depth, d6 // tn),
        in_specs=[pl.BlockSpec((rows, d), lambda l, j: (0, 0)),
                  pl.BlockSpec((1, d, tn), lambda l, j: (l, 0, j)),
                  pl.BlockSpec((1, 1, tn), lambda l, j: (l, 0, j))],
        out_specs=pl.BlockSpec((1, rows, tn), lambda l, j: (l, 0, j)),
        compiler_params=_params("parallel", "parallel"),
        name="adaln",
    )(cvecs, w_mod, b_mod.reshape(depth, 1, d6))


def _head_rms(t, g, gsum_ref):
    w = t.shape[-1]
    sq = t * t
    hi = sq.astype(MXU_DTYPE)
    lo = (sq - hi.astype(F32)).astype(MXU_DTYPE)
    gs = gsum_ref[0:w, 0:w]
    ss = jnp.dot(hi, gs, preferred_element_type=F32) + jnp.dot(lo, gs, preferred_element_type=F32)
    return t * lax.rsqrt(ss * (1.0 / HEAD_DIM) + EPS) * g


def _rope(t, cos, sin_signed):
    w = t.shape[-1]
    rep = w // LANES
    if rep > 1:
        cos = jnp.concatenate([cos] * rep, axis=-1)
        sin_signed = jnp.concatenate([sin_signed] * rep, axis=-1)
    lane = lax.broadcasted_iota(jnp.int32, t.shape, 1)
    first_half = (lane % (2 * ROPE_FREQS)) < ROPE_FREQS
    partner = jnp.where(first_half, pltpu.roll(t, w - ROPE_FREQS, 1), pltpu.roll(t, ROPE_FREQS, 1))
    return t * cos + partner * sin_signed


def _with_ones_lane(v):
    lane = lax.broadcasted_iota(jnp.int32, (v.shape[0], LANES - HEAD_DIM), 1)
    pad = jnp.where(lane == 0, 1.0, 0.0).astype(v.dtype)
    parts = []
    for h in range(v.shape[1] // HEAD_DIM):
        parts += [v[:, h * HEAD_DIM:(h + 1) * HEAD_DIM], pad]
    return jnp.concatenate(parts, axis=-1)


def _inproj_body(x_ref, nw_ref, sh_ref, sc_ref, w_ref, cos_ref, sin_ref, qn_ref, kn_ref, gn_ref,
                 ws_ref, bs_ref, gsum_ref,
                 qa_ref, ka_ref, va_ref, qb_ref, kb_ref, vb_ref, qs_ref, ks_ref, vs_ref, yd_ref):
    x = x_ref[0]
    h = _rms(x, nw_ref[...]) * (1.0 + sc_ref[0]) + sh_ref[0]
    p = _mm(h, w_ref[...])
    o = PROJ_OFFS
    cos, sin = cos_ref[...], sin_ref[...]
    dt = qa_ref.dtype
    qa_ref[0] = (p[:, o[0]:o[1]] * Q_SCALE).astype(dt)
    ka_ref[0] = p[:, o[1]:o[2]].astype(dt)
    va_ref[0] = _with_ones_lane(p[:, o[2]:o[3]].astype(dt))
    qb = _rope(_head_rms(p[:, o[3]:o[4]], qn_ref[...], gsum_ref), cos, sin)
    qb_ref[0] = (qb * Q_SCALE).astype(dt)
    kb_ref[0] = _rope(_head_rms(p[:, o[4]:o[5]], kn_ref[...], gsum_ref), cos, sin).astype(dt)
    vb_ref[0] = _with_ones_lane(p[:, o[5]:o[6]].astype(dt))
    qs_ref[0] = (_rope(p[:, o[6]:o[7]], cos, sin) * Q_SCALE).astype(dt)
    ks_ref[0] = _rope(p[:, o[7]:o[8]], cos, sin).astype(dt)
    vs_ref[0] = _with_ones_lane(p[:, o[8]:o[9]].astype(dt))
    u = jax.nn.gelu(p[:, o[9]:o[10]])
    v = _rms(jax.nn.gelu(p[:, o[10]:o[11]]), gn_ref[...]).astype(MXU_DTYPE)
    lane_group = lax.broadcasted_iota(jnp.int32, (CHUNK, MIXER_WIDTH), 1) // (MIXER_WIDTH // SG_GROUPS)
    for c in range(x.shape[0] // CHUNK):
        rows = slice(c * CHUNK, (c + 1) * CHUNK)
        mixed = bs_ref[...]
        for g in range(SG_GROUPS):
            mg = jnp.dot(ws_ref[g], v[rows], preferred_element_type=F32)
            mixed = mixed + jnp.where(lane_group == g, mg, 0.0)
        yd_ref[0, rows, :] = (u[rows] * mixed).astype(dt)


def _inproj(x, nw, shift, scale, lw, cos, sin, tm):
    b, t, d = x.shape
    widths = tuple(w * (LANES // HEAD_DIM if j % 3 == 2 else 1) for j, w in enumerate(PROJ_SIZES[:9]))
    widths += (MIXER_WIDTH,)
    row = lambda bi, i: (0, 0)
    per_b = lambda bi, i: (bi, 0, 0)
    tile = lambda bi, i: (bi, i, 0)
    return pl.pallas_call(
        _inproj_body,
        out_shape=[jax.ShapeDtypeStruct((b, t, w), ACT_DTYPE) for w in widths],
        grid=(b, t // tm),
        in_specs=[pl.BlockSpec((1, tm, d), tile),
                  pl.BlockSpec((1, d), row),
                  pl.BlockSpec((1, 1, d), per_b),
                  pl.BlockSpec((1, 1, d), per_b),
                  pl.BlockSpec(lw["w_in"].shape, row),
                  pl.BlockSpec((tm, LANES), lambda bi, i: (i, 0)),
                  pl.BlockSpec((tm, LANES), lambda bi, i: (i, 0)),
                  pl.BlockSpec((1, 256), row),
                  pl.BlockSpec((1, 128), row),
                  pl.BlockSpec((1, 256), row),
                  pl.BlockSpec((SG_GROUPS, CHUNK, CHUNK), lambda bi, i: (0, 0, 0)),
                  pl.BlockSpec((CHUNK, MIXER_WIDTH), row),
                  pl.BlockSpec((256, 256), row)],
        out_specs=[pl.BlockSpec((1, tm, w), tile) for w in widths],
        compiler_params=_params("parallel", "parallel"),
        name="inproj",
    )(x, nw, shift, scale, lw["w_in"], cos, sin, lw["qn"], lw["kn"], lw["gn"], lw["w_sgu"],
      lw["b_sgu"], lw["gsum"])


def _attend_all(jobs):
    staged = []
    for q, chunks, sink in jobs:
        scores = []
        for k, _, bias in chunks:
            s = lax.dot_general(q, k, NT_DIMS, preferred_element_type=F32)
            scores.append(s if bias is None else s + bias)
        m = jnp.max(functools.reduce(jnp.maximum, scores), axis=-1, keepdims=True)
        staged.append((scores, m if sink is None else jnp.maximum(m, sink)))
    outs = []
    for (q, chunks, sink), (scores, m) in zip(jobs, staged):
        acc = jnp.zeros((q.shape[0], LANES), F32)
        for s, (_, v, _) in zip(scores, chunks):
            acc = acc + jnp.dot(jnp.exp2(s - m).astype(v.dtype), v, preferred_element_type=F32)
        l = acc[:, HEAD_DIM:HEAD_DIM + 1]
        if sink is not None:
            l = l + jnp.exp2(sink - m)
        outs.append(acc[:, :HEAD_DIM] / l)
    return outs


def _head(ref, h, width=HEAD_DIM):
    return ref[0, :, h * width:(h + 1) * width]


def _local_attn_body(*refs, group, has_sink):
    if has_sink:
        sink_ref, refs = refs[0], refs[1:]
    q_ref, k0, k1, k2, v0, v1, v2, kc_ref, vc_ref, bias_ref, o_ref = refs
    jobs = []
    for h in range(q_ref.shape[-1] // HEAD_DIM):
        kv = h // group
        hb = h if bias_ref.shape[1] > 1 else 0
        chunks = [(_head(kr, kv), _head(vr, kv, LANES), bias_ref[0, hb, j])
                  for j, (kr, vr) in enumerate(((k0, v0), (k1, v1), (k2, v2)))]
        chunks.append((_head(kc_ref, kv), _head(vc_ref, kv, LANES), None))
        jobs.append((_head(q_ref, h), chunks, sink_ref[h] if has_sink else None))
    o_ref[0] = jnp.concatenate(_attend_all(jobs), axis=-1).astype(o_ref.dtype)


def _local_attn(q, k, v, kc, vc, bias, sink=None):
    b, n, qw = q.shape
    kw = k.shape[-1]
    nc = kc.shape[1]
    tq = ATT_BLOCK
    nblk = n // tq
    assert nblk >= 4
    group = qw // kw
    hb = bias.shape[1]

    def kmap(j):
        return lambda bi, i: (bi, jnp.clip(i - 1, 0, nblk - 3) + j, 0)

    def bmap(bi, i):
        return (jnp.where(i == 0, 0, jnp.where(i == nblk - 1, 2, 1)), 0, 0, 0, 0)

    in_specs = [pl.BlockSpec((1, tq, qw), lambda bi, i: (bi, i, 0))]
    vw = v.shape[-1]
    assert nc == tq
    in_specs += [pl.BlockSpec((1, tq, kw), kmap(j)) for j in range(3)]
    in_specs += [pl.BlockSpec((1, tq, vw), kmap(j)) for j in range(3)]
    in_specs += [pl.BlockSpec((1, nc, kw), lambda bi, i: (bi, 0, 0)),
                 pl.BlockSpec((1, nc, vw), lambda bi, i: (bi, 0, 0))]
    in_specs += [pl.BlockSpec((1, hb, 3, tq, tq), bmap)]
    args = [q, k, k, k, v, v, v, kc, vc, bias]
    if sink is not None:
        in_specs = [pl.BlockSpec(memory_space=pltpu.SMEM)] + in_specs
        args = [sink] + args
    return pl.pallas_call(
        functools.partial(_local_attn_body, group=group, has_sink=sink is not None),
        out_shape=jax.ShapeDtypeStruct((b, n, qw), ACT_DTYPE),
        grid=(b, nblk),
        in_specs=in_specs,
        out_specs=pl.BlockSpec((1, tq, qw), lambda bi, i: (bi, i, 0)),
        compiler_params=_params("parallel", "parallel"),
        name="local_attn",
    )(*args)


def _ctx_attn_body(sink_ref, qa, ka, va, qb, kb, vb, qs, ks, vs, oa, ob, oc):
    for q_ref, k_ref, v_ref, o_ref, group, use_sink in (
            (qa, ka, va, oa, 1, False), (qb, kb, vb, ob, 2, False), (qs, ks, vs, oc, 2, True)):
        jobs = [(_head(q_ref, h), [(_head(k_ref, h // group), _head(v_ref, h // group, LANES), None)],
                 sink_ref[h] if use_sink else None) for h in range(q_ref.shape[-1] // HEAD_DIM)]
        o_ref[0] = jnp.concatenate(_attend_all(jobs), axis=-1).astype(o_ref.dtype)


def _ctx_attn(sink, qkv):
    b, nc, _ = qkv[0].shape
    spec = lambda a: pl.BlockSpec((1, nc, a.shape[-1]), lambda bi: (bi, 0, 0))
    return pl.pallas_call(
        _ctx_attn_body,
        out_shape=[jax.ShapeDtypeStruct((b, nc, MIXER_WIDTH), ACT_DTYPE)] * 3,
        grid=(b,),
        in_specs=[pl.BlockSpec(memory_space=pltpu.SMEM)] + [spec(a) for a in qkv],
        out_specs=[pl.BlockSpec((1, nc, MIXER_WIDTH), lambda bi: (bi, 0, 0))] * 3,
        compiler_params=_params("parallel"),
        name="ctx_attn",
    )(sink, *qkv)


def _global_attn_body(q_ref, kt_ref, v_ref, o_ref, s_a, s_b, *, tk):
    tq = q_ref.shape[1]
    n_kv = kt_ref.shape[1] // HEAD_DIM
    n_chunks = kt_ref.shape[2] // tk
    group_w = 2 * HEAD_DIM
    qs = [jnp.concatenate([q_ref[0, :, kv * group_w:kv * group_w + HEAD_DIM],
                           q_ref[0, :, kv * group_w + HEAD_DIM:(kv + 1) * group_w]], axis=0)
          for kv in range(n_kv)]

    def scores(i, s_ref):
        ks = pl.multiple_of(i * tk, tk)
        for kv in range(n_kv):
            s_ref[kv] = jnp.dot(qs[kv], kt_ref[0, kv * HEAD_DIM:(kv + 1) * HEAD_DIM, pl.ds(ks, tk)],
                                preferred_element_type=F32)

    def update(i, s_ref, carry):
        ks = pl.multiple_of(i * tk, tk)
        out = []
        for kv in range(n_kv):
            m, acc = carry[kv]
            s = s_ref[kv]
            m_new = jnp.maximum(m, jnp.max(s, axis=-1, keepdims=True))
            p = jnp.exp2(s - m_new).astype(v_ref.dtype)
            pv = jnp.dot(p, v_ref[0, pl.ds(ks, tk), kv * LANES:(kv + 1) * LANES], preferred_element_type=F32)
            out.append((m_new, jnp.exp2(m - m_new) * acc + pv))
        return tuple(out)

    carry = tuple((jnp.full((2 * tq, 1), NEG_INF, F32), jnp.zeros((2 * tq, LANES), F32)) for _ in range(n_kv))
    scores(0, s_a)

    def pair(j, carry):
        scores(2 * j + 1, s_b)
        carry = update(2 * j, s_a, carry)
        scores(2 * j + 2, s_a)
        return update(2 * j + 1, s_b, carry)

    carry = lax.fori_loop(0, (n_chunks - 1) // 2, pair, carry)
    if n_chunks % 2 == 0:
        scores(n_chunks - 1, s_b)
        carry = update(n_chunks - 2, s_a, carry)
        carry = update(n_chunks - 1, s_b, carry)
    else:
        carry = update(n_chunks - 1, s_a, carry)
    outs = []
    for _, acc in carry:
        o = acc[:, :HEAD_DIM] / acc[:, HEAD_DIM:HEAD_DIM + 1]
        outs += [o[:tq], o[tq:]]
    o_ref[0] = jnp.concatenate(outs, axis=-1).astype(o_ref.dtype)


def _global_attn(q, k, v):
    b, n, qw = q.shape
    nk, kw = k.shape[1:]
    n_kv = kw // HEAD_DIM
    tq = ATT_BLOCK
    tk = 1280 if (nk % 1280 == 0 and nk > 1280) else 256
    kt = jnp.swapaxes(k, 1, 2)
    return pl.pallas_call(
        functools.partial(_global_attn_body, tk=tk),
        out_shape=jax.ShapeDtypeStruct((b, n, qw), ACT_DTYPE),
        grid=(b, n // tq),
        in_specs=[pl.BlockSpec((1, tq, qw), lambda bi, i: (bi, i, 0)),
                  pl.BlockSpec((1, kw, nk), lambda bi, i: (bi, 0, 0)),
                  pl.BlockSpec((1, nk, n_kv * LANES), lambda bi, i: (bi, 0, 0))],
        out_specs=pl.BlockSpec((1, tq, qw), lambda bi, i: (bi, i, 0)),
        scratch_shapes=[pltpu.VMEM((n_kv, 2 * tq, tk), F32), pltpu.VMEM((n_kv, 2 * tq, tk), F32)],
        compiler_params=_params("parallel", "parallel"),
        name="global_attn",
    )(q, kt, v)


def _merge_body(ya, yb, yc, yd, x_ref, on_ref, wo_ref, g1_ref, nf_ref, sh_ref, sc_ref, wr_ref,
                xo_ref, h_ref, aff_ref):
    parts = []
    for j, r in enumerate((ya, yb, yc, yd)):
        y = r[0].astype(F32)
        parts.append(_rms(y, on_ref[:, j * MIXER_WIDTH:(j + 1) * MIXER_WIDTH]).astype(MXU_DTYPE))
    xn = x_ref[0] + g1_ref[0] * _mm(jnp.concatenate(parts, axis=-1), wo_ref[...])
    xo_ref[0] = xn
    h = _rms(xn, nf_ref[...]) * (1.0 + sc_ref[0]) + sh_ref[0]
    _write_row_tiles(h_ref, (0,), h)
    logits = lax.dot_general(wr_ref[...], h, NT_DIMS, precision=HI, preferred_element_type=F32)
    e = jnp.exp(logits - jnp.max(logits, axis=0, keepdims=True))
    aff_ref[0] = e / jnp.sum(e, axis=0, keepdims=True)


def _merge(ys, x, lw, g1, shift, scale, tm):
    b, t, d = x.shape
    ne = lw["w_router_t"].shape[0]
    row = lambda bi, i: (0, 0)
    per_b = lambda bi, i: (bi, 0, 0)
    tile = lambda bi, i: (bi, i, 0)
    return pl.pallas_call(
        _merge_body,
        out_shape=[jax.ShapeDtypeStruct((b, t, d), F32), jax.ShapeDtypeStruct((b, t * ROW_TILE, LANES), F32),
                   jax.ShapeDtypeStruct((b, ne, t), F32)],
        grid=(b, t // tm),
        in_specs=[pl.BlockSpec((1, tm, MIXER_WIDTH), tile)] * 4 + [
            pl.BlockSpec((1, tm, d), tile),
            pl.BlockSpec((1, d), row),
            pl.BlockSpec((d, d), row),
            pl.BlockSpec((1, 1, d), per_b),
            pl.BlockSpec((1, d), row),
            pl.BlockSpec((1, 1, d), per_b),
            pl.BlockSpec((1, 1, d), per_b),
            pl.BlockSpec((ne, d), row)],
        out_specs=[pl.BlockSpec((1, tm, d), tile), pl.BlockSpec((1, tm * ROW_TILE, LANES), tile),
                   pl.BlockSpec((1, ne, tm), lambda bi, i: (bi, 0, i))],
        compiler_params=_params("parallel", "parallel"),
        name="merge",
    )(*ys, x, lw["out_norm"], lw["w_out"], g1, lw["norm_ffn"], shift, scale, lw["w_router_t"])


def _tri(n, m, mode):
    r = lax.broadcasted_iota(jnp.int32, (n, m), 0)
    c = lax.broadcasted_iota(jnp.int32, (n, m), 1)
    return jnp.where({"lt": r < c, "le": r <= c, "gt": r > c}[mode], 1.0, 0.0).astype(MXU_DTYPE)


def _count(mask, axes):
    out = jnp.where(mask, 1.0, 0.0)
    for ax in sorted(axes, reverse=True):
        out = jnp.sum(out, axis=ax, keepdims=True)
    return out


def _kth_largest_bits(bits, cap, axes):
    shape = tuple(1 if a in axes else s for a, s in enumerate(bits.shape))

    def body(i, t):
        cand = t | lax.shift_left(jnp.int32(1), 30 - i)
        return jnp.where(_count(bits >= cand, axes) >= cap, cand, t)

    return lax.fori_loop(0, 31, body, jnp.zeros(shape, jnp.int32))


def _prefix_tokens(m, exact_rows):
    e, r, l = m.shape
    m2 = m.reshape(e * r, l).astype(MXU_DTYPE)
    local = jnp.dot(m2, _tri(l, l, "lt"), preferred_element_type=F32)
    rowtot = jnp.dot(m2, jnp.ones((l, l), MXU_DTYPE), preferred_element_type=F32).reshape(e, r, l)
    below = _tri(r, r, "gt")
    if exact_rows:
        base = [jnp.dot(below, rowtot[i].astype(MXU_DTYPE), preferred_element_type=F32) for i in range(e)]
    else:
        base = [jnp.dot(below.astype(F32), rowtot[i], precision=HI, preferred_element_type=F32)
                for i in range(e)]
    return local.reshape(e, r, l) + jnp.stack(base, axis=0)


def _select_mask(aff, cap, prefix_fn, axes):
    bits = pltpu.bitcast(aff, jnp.int32)
    thr = _kth_largest_bits(bits, cap, axes)
    gt = bits > thr
    eq = bits == thr
    need = cap - _count(gt, axes)
    eq_rank = prefix_fn(jnp.where(eq, 1.0, 0.0))
    take_eq = jnp.where(eq, jnp.where(eq_rank < need, 1.0, 0.0), 0.0)
    return jnp.where(gt, 1.0, take_eq)


def _select_body(aff_ref, sel_ref, prank_ref, tstart_ref, tend_ref, *, cap):
    aff = aff_ref[0]
    ne = aff.shape[0]
    sel = _select_mask(aff, cap, functools.partial(_prefix_tokens, exact_rows=True), (1, 2))
    sel_ref[0] = sel
    cnt = jnp.sum(sel, axis=0)
    tstart = _prefix_tokens(cnt[None], exact_rows=False)[0]
    tstart_ref[0] = tstart.astype(jnp.int32)
    tend_ref[0] = (tstart + cnt).astype(jnp.int32)
    run = tstart
    for e in range(ne):
        prank_ref[0, e] = run.astype(jnp.int32)
        run = run + sel[e]


def _select(aff4, cap):
    b, ne, r, l = aff4.shape
    blk4 = pl.BlockSpec((1, ne, r, l), lambda bi: (bi, 0, 0, 0))
    blk3 = pl.BlockSpec((1, r, l), lambda bi: (bi, 0, 0))
    return pl.pallas_call(
        functools.partial(_select_body, cap=cap),
        out_shape=[jax.ShapeDtypeStruct((b, ne, r, l), F32), jax.ShapeDtypeStruct((b, ne, r, l), jnp.int32),
                   jax.ShapeDtypeStruct((b, r, l), jnp.int32), jax.ShapeDtypeStruct((b, r, l), jnp.int32)],
        grid=(b,),
        in_specs=[blk4],
        out_specs=[blk4, blk4, blk3, blk3],
        compiler_params=_params("parallel"),
        name="moe_select",
    )(aff4)


def _slots_body(sel_ref, aff_ref, prank_ref, idx_ref, dest_ref, gate_ref):
    m = sel_ref[0, 0]
    r, l = m.shape
    cap = idx_ref.shape[2]
    mb = m.astype(MXU_DTYPE)
    linc = jnp.dot(mb, _tri(l, l, "le"), preferred_element_type=F32)
    rowtot = jnp.dot(mb, jnp.ones((l, l), MXU_DTYPE), preferred_element_type=F32)
    rowtot_lane = lax.dot_general(jnp.ones((8, l), MXU_DTYPE), mb, NT_DIMS, preferred_element_type=F32)
    cumrow = jnp.dot(rowtot_lane.astype(MXU_DTYPE), _tri(r, r, "le"), preferred_element_type=F32)[0:1]
    slot = lax.broadcasted_iota(jnp.int32, (cap, r), 0).astype(F32)
    passed = jnp.where(cumrow <= slot, 1.0, 0.0).astype(MXU_DTYPE)
    row_of = jnp.dot(passed, jnp.ones((r, l), MXU_DTYPE), preferred_element_type=F32)[:, 0:1]
    base_of = jnp.dot(passed, rowtot.astype(MXU_DTYPE), preferred_element_type=F32)[:, 0:1]
    onehot = jnp.where(lax.broadcasted_iota(jnp.int32, (cap, r), 1).astype(F32) == row_of, 1.0, 0.0)
    linc_of = jnp.dot(onehot.astype(MXU_DTYPE), linc.astype(MXU_DTYPE), preferred_element_type=F32)
    k = slot[:, 0:1] - base_of
    col_of = jnp.sum(jnp.where(linc_of <= k, 1.0, 0.0), axis=-1, keepdims=True)
    idx_ref[0, 0] = (row_of * l + col_of).astype(jnp.int32)
    at_col = lax.broadcasted_iota(jnp.int32, (cap, l), 1).astype(F32) == col_of
    hot = onehot.astype(MXU_DTYPE)
    aff_rows, rest = jnp.zeros((cap, l), F32), aff_ref[0, 0]
    for _ in range(3):
        piece = rest.astype(MXU_DTYPE)
        aff_rows = aff_rows + jnp.dot(hot, piece, preferred_element_type=F32)
        rest = rest - piece.astype(F32)
    gate_ref[0, 0] = jnp.sum(jnp.where(at_col, aff_rows, 0.0), axis=-1, keepdims=True)
    prank = prank_ref[0, 0]
    prank_rows = (jnp.dot(hot, (prank >> 8).astype(F32).astype(MXU_DTYPE), preferred_element_type=F32) * 256.0
                  + jnp.dot(hot, (prank & 255).astype(F32).astype(MXU_DTYPE), preferred_element_type=F32))
    dest_ref[0, 0] = jnp.sum(jnp.where(at_col, prank_rows, 0.0), axis=-1, keepdims=True).astype(jnp.int32)


def _slots(sel, aff4, prank, cap):
    b, ne, r, l = sel.shape
    blk = pl.BlockSpec((1, 1, r, l), lambda bi, e: (bi, e, 0, 0))
    oblk = pl.BlockSpec((1, 1, cap, 1), lambda bi, e: (bi, e, 0, 0))
    return pl.pallas_call(
        _slots_body,
        out_shape=[jax.ShapeDtypeStruct((b, ne, cap, 1), jnp.int32), jax.ShapeDtypeStruct((b, ne, cap, 1), jnp.int32),
                   jax.ShapeDtypeStruct((b, ne, cap, 1), F32)],
        grid=(b, ne),
        in_specs=[blk, blk, blk],
        out_specs=[oblk, oblk, oblk],
        compiler_params=_params("parallel", "parallel"),
        name="moe_slots",
    )(sel, aff4, prank)


FFN_CHUNK = 256


def _ffn_body(idx_ref, dest_ref, h_hbm, gate_ref, wg_ref, wu_ref, wd_ref, z_hbm, xa, xb, ya, yb, sems,
              *, n_tok, n_pair):
    n_b, tiles = pl.num_programs(1), pl.num_programs(2)
    step = (pl.program_id(0) * n_b + pl.program_id(1)) * tiles + pl.program_id(2)
    last = pl.num_programs(0) * n_b * tiles - 1
    ts = xa.shape[0] // ROW_TILE
    n_groups = wg_ref.shape[2] // FFN_CHUNK
    per_group = ts // n_groups

    def sample_of(k):
        return (k // tiles) % n_b

    def tile_of(ref, r):
        return ref.at[pl.ds(pl.multiple_of(r * ROW_TILE, ROW_TILE), ROW_TILE), :]

    def gather(k, half, buf, sem):
        base, rows = (2 * k + half) * ts, sample_of(k) * n_tok
        return lambda s: pltpu.make_async_copy(
            tile_of(h_hbm, rows + idx_ref[base + s]), tile_of(buf, s), sems.at[sem])

    def scatter(k, half, buf, sem):
        base, rows = (2 * k + half) * ts, sample_of(k) * n_pair
        return lambda s: pltpu.make_async_copy(
            tile_of(buf, s), tile_of(z_hbm, rows + dest_ref[base + s]), sems.at[sem])

    def start_all(copy):
        def body(s, c):
            copy(s).start()
            return c
        lax.fori_loop(0, ts, body, 0, unroll=8)

    def wait_rows(buf, sem):
        pltpu.make_async_copy(buf, buf, sems.at[sem]).wait()

    def ffn(xbuf, gate, copies):
        x = _read_row_tiles(xbuf, (), ts).astype(MXU_DTYPE)
        y = None
        for j in range(n_groups):
            for copy in copies:
                for s in range(j * per_group, (j + 1) * per_group):
                    copy(s).start()
            cols = slice(j * FFN_CHUNK, (j + 1) * FFN_CHUNK)
            hid = _silu(_mm(x, wg_ref[0, :, cols])) * _mm(x, wu_ref[0, :, cols])
            part = _mm(hid, wd_ref[0, cols, :])
            y = part if y is None else y + part
        return y * gate

    @pl.when(step == 0)
    def _():
        start_all(gather(step, 0, xa, 0))

    wait_rows(xa, 0)
    y_a = ffn(xa, gate_ref[0, 0, 0:ts], [gather(step, 1, xb, 1)])

    @pl.when(step > 0)
    def _():
        wait_rows(ya, 2)
        wait_rows(yb, 3)

    _write_row_tiles(ya, (), y_a)
    wait_rows(xb, 1)
    nxt = jnp.minimum(step + 1, last)
    y_b = ffn(xb, gate_ref[0, 0, ts:2 * ts], [scatter(step, 0, ya, 2), gather(nxt, 0, xa, 0)])
    _write_row_tiles(yb, (), y_b)
    start_all(scatter(step, 1, yb, 3))

    @pl.when(step == last)
    def _():
        wait_rows(xa, 0)
        wait_rows(ya, 2)
        wait_rows(yb, 3)


def _expert_ffn(idx, dest, gate, h, lw, ts):
    b, ne, cap, _ = gate.shape
    n, d = h.shape[1] // ROW_TILE, ROW_TILE * LANES
    f = lw["w_gate"].shape[-1]
    layer = lw["layer"]
    n_pair = ne * cap
    by_expert = lambda a: jnp.swapaxes(a, 0, 1)
    grid_spec = pltpu.PrefetchScalarGridSpec(
        num_scalar_prefetch=2,
        grid=(ne, b, cap // (2 * ts)),
        in_specs=[pl.BlockSpec(memory_space=pl.ANY),
                  pl.BlockSpec((1, 1, 2 * ts, 1), lambda e, bi, t, *_: (e, bi, t, 0)),
                  pl.BlockSpec((None, 1, d, f), lambda e, bi, t, *_: (layer, e, 0, 0)),
                  pl.BlockSpec((None, 1, d, f), lambda e, bi, t, *_: (layer, e, 0, 0)),
                  pl.BlockSpec((None, 1, f, d), lambda e, bi, t, *_: (layer, e, 0, 0))],
        out_specs=pl.BlockSpec(memory_space=pl.ANY),
        scratch_shapes=[pltpu.VMEM((ts * ROW_TILE, LANES), F32)] * 4 + [pltpu.SemaphoreType.DMA((4,))])
    return pl.pallas_call(
        functools.partial(_ffn_body, n_tok=n, n_pair=n_pair),
        out_shape=jax.ShapeDtypeStruct((b * n_pair * ROW_TILE, LANES), F32),
        grid_spec=grid_spec,
        compiler_params=_params("arbitrary", "arbitrary", "arbitrary"),
        name="moe_ffn",
    )(by_expert(idx).reshape(-1), by_expert(dest).reshape(-1), h.reshape(b * n * ROW_TILE, LANES), by_expert(gate),
      lw["w_gate"], lw["w_up"], lw["w_down"])


COMBINE_DEPTH = 4


def _combine_body(tb_ref, x_ref, g2_ref, ts_ref, te_ref, fg_ref, z_hbm, o_ref, zbuf, acc_ref, sem,
                  *, n_pair, final):
    n_tiles = pl.num_programs(1)
    tile = pl.program_id(0) * n_tiles + pl.program_id(1)
    depth, pc = zbuf.shape[0], zbuf.shape[1] // ROW_TILE
    tt = x_ref.shape[1]

    def plan(g):
        b, i = g // n_tiles, g % n_tiles
        lo, hi = tb_ref[b * (n_tiles + 1) + i], tb_ref[b * (n_tiles + 1) + i + 1]
        p0 = (lo // 8) * 8
        return b, p0, (hi - p0 + pc - 1) // pc

    def fetch(b, p0, k, slot):
        cs = pl.multiple_of(jnp.minimum(p0 + k * pc, n_pair - pc), 8)
        rows = pl.ds(pl.multiple_of((b * n_pair + cs) * ROW_TILE, 8 * ROW_TILE), pc * ROW_TILE)
        return cs, pltpu.make_async_copy(z_hbm.at[rows, :], zbuf.at[slot], sem.at[slot])

    b, p0, n_chunks = plan(tile)
    start_row, end_row = ts_ref[0, 0], te_ref[0, 0]
    acc_ref[...] = jnp.zeros_like(acc_ref)

    def start_head(b, p0, n_chunks):
        for k in range(depth - 1):
            @pl.when(k < n_chunks)
            def _():
                fetch(b, p0, k, k)[1].start()

    @pl.when(tile == 0)
    def _():
        start_head(b, p0, n_chunks)

    def chunk(k, c):
        slot = k % depth
        ahead = k + depth - 1

        @pl.when(ahead < n_chunks)
        def _():
            fetch(b, p0, ahead, ahead % depth)[1].start()

        cs, cp = fetch(b, p0, k, slot)
        cp.wait()
        pair = cs + lax.broadcasted_iota(jnp.int32, (pc, tt), 0)
        own = jnp.where(pair >= jnp.maximum(start_row, p0 + k * pc), jnp.where(pair < end_row, 1.0, 0.0), 0.0)
        own = own.astype(MXU_DTYPE)
        z = _read_row_tiles(zbuf, (slot,), pc).astype(MXU_DTYPE)
        acc_ref[...] += lax.dot_general(own, z, TN_DIMS, preferred_element_type=F32)
        return c

    lax.fori_loop(0, n_chunks, chunk, 0)

    @pl.when(tile + 1 < pl.num_programs(0) * n_tiles)
    def _():
        start_head(*plan(tile + 1))

    out = x_ref[0] + g2_ref[0] * acc_ref[...]
    o_ref[0] = _rms(out, fg_ref[...]) if final else out


def _combine(x, g2, tstart, tend, z, n_pair, tt, pc, final_g, final):
    b, n, d = x.shape
    nt = n // tt
    ts4 = tstart.reshape(b, nt, 1, tt)
    te4 = tend.reshape(b, nt, 1, tt)
    bounds = jnp.concatenate([ts4[:, :, 0, 0], jnp.full((b, 1), n_pair, jnp.int32)], axis=1).reshape(-1)
    grid_spec = pltpu.PrefetchScalarGridSpec(
        num_scalar_prefetch=1,
        grid=(b, nt),
        in_specs=[pl.BlockSpec((1, tt, d), lambda bi, i, *_: (bi, i, 0)),
                  pl.BlockSpec((1, 1, d), lambda bi, i, *_: (bi, 0, 0)),
                  pl.BlockSpec((1, 1, 1, tt), lambda bi, i, *_: (bi, i, 0, 0)),
                  pl.BlockSpec((1, 1, 1, tt), lambda bi, i, *_: (bi, i, 0, 0)),
                  pl.BlockSpec((1, d), lambda bi, i, *_: (0, 0)),
                  pl.BlockSpec(memory_space=pl.ANY)],
        out_specs=pl.BlockSpec((1, tt, d), lambda bi, i, *_: (bi, i, 0)),
        scratch_shapes=[pltpu.VMEM((COMBINE_DEPTH, pc * ROW_TILE, LANES), F32), pltpu.VMEM((tt, d), F32),
                        pltpu.SemaphoreType.DMA((COMBINE_DEPTH,))])
    return pl.pallas_call(
        functools.partial(_combine_body, n_pair=n_pair, final=final),
        out_shape=jax.ShapeDtypeStruct((b, n, d), F32),
        grid_spec=grid_spec,
        compiler_params=_params("arbitrary", "arbitrary"),
        name="moe_combine",
    )(bounds, x, g2, ts4, te4, final_g, z)


def _moe_latent(x, h, aff, g2, lw, final_g, final):
    b, n, d = x.shape
    ne = aff.shape[1]
    cap = EC_CAPACITY_FACTOR * n // ne
    aff4 = aff.reshape(b, ne, n // LANES, LANES)
    sel, prank, tstart, tend = _select(aff4, cap)
    idx, dest, gate = _slots(sel, aff4, prank, cap)
    z = _expert_ffn(idx, dest, gate, h, lw, min(cap // 2, 256))
    assert ne * cap <= 1 << 16
    return _combine(x, g2, tstart, tend, z, ne * cap, 256, 256, final_g, final)


def _ctx_coef_body(aff_ref, coef_ref, slot_ref, *, cap):
    nb, _, nc = aff_ref.shape
    excl = _tri(nc, nc, "lt")
    prefix = lambda mm: jnp.dot(mm.astype(MXU_DTYPE), excl, preferred_element_type=F32)
    coefs, slots = [], []
    for b in range(nb):
        aff = aff_ref[b]
        sel = _select_mask(aff, cap, prefix, (1,))
        coefs.append(sel * aff)
        slots.append(jnp.where(sel > 0.0, prefix(sel) + float(b * cap), -1.0))
    coef_ref[:, 0, :] = jnp.concatenate(coefs, axis=-1)
    slot_ref[:, 0, :] = jnp.concatenate(slots, axis=-1)


def _ctx_coef(aff, cap):
    b, ne, nc = aff.shape
    return pl.pallas_call(
        functools.partial(_ctx_coef_body, cap=cap),
        out_shape=[jax.ShapeDtypeStruct((ne, 1, b * nc), F32)] * 2,
        name="ctx_moe_select",
    )(aff)


def _ctx_ffn_body(h_ref, coef_ref, slot_ref, wg_ref, wu_ref, wd_ref, x_ref, g2_ref, o_ref, *, n_slots):
    e = pl.program_id(0)
    rows = h_ref.shape[0] // ROW_TILE
    h = _read_row_tiles(h_ref, (), rows)
    picked = jnp.where(lax.broadcasted_iota(jnp.int32, (n_slots, rows), 0).astype(F32) == slot_ref[0], 1.0, 0.0)
    gate = jnp.sum(picked * coef_ref[0], axis=-1, keepdims=True)
    x = _mm(picked, h)
    y = _mm(_silu(_mm(x, wg_ref[0])) * _mm(x, wu_ref[0]), wd_ref[0]) * gate

    @pl.when(e == 0)
    def _():
        o_ref[...] = x_ref[...]

    o_ref[...] += g2_ref[...] * lax.dot_general(picked.astype(MXU_DTYPE), y.astype(MXU_DTYPE), TN_DIMS,
                                                 preferred_element_type=F32)


def _ctx_moe(xc, hc, aff, g2, lw):
    b, nc, d = xc.shape
    ne = aff.shape[1]
    f = lw["w_gate"].shape[-1]
    layer = lw["layer"]
    cap = EC_CAPACITY_FACTOR * nc // ne
    coef, slot = _ctx_coef(aff, cap)
    rows = b * nc
    per_e = pl.BlockSpec((1, 1, rows), lambda e: (e, 0, 0))
    out = pl.pallas_call(
        functools.partial(_ctx_ffn_body, n_slots=b * cap),
        out_shape=jax.ShapeDtypeStruct((rows, d), F32),
        grid=(ne,),
        in_specs=[pl.BlockSpec((rows * ROW_TILE, LANES), lambda e: (0, 0)),
                  per_e, per_e,
                  pl.BlockSpec((None, 1, d, f), lambda e: (layer, e, 0, 0)),
                  pl.BlockSpec((None, 1, d, f), lambda e: (layer, e, 0, 0)),
                  pl.BlockSpec((None, 1, f, d), lambda e: (layer, e, 0, 0)),
                  pl.BlockSpec((rows, d), lambda e: (0, 0)),
                  pl.BlockSpec((1, d), lambda e: (0, 0))],
        out_specs=pl.BlockSpec((rows, d), lambda e: (0, 0)),
        compiler_params=_params("arbitrary"),
        name="ctx_moe_ffn",
    )(hc.reshape(rows * ROW_TILE, LANES), coef, slot, lw["w_gate"], lw["w_up"], lw["w_down"],
      xc.reshape(rows, d), g2)
    return out.reshape(b, nc, d)


def _rope_tables(n):
    t = np.arange(n)
    freqs = ROPE_THETA ** (-np.arange(ROPE_FREQS, dtype=np.float32) / ROPE_FREQS)
    ang_r = (t // GRID_W).astype(np.float32)[:, None] * freqs
    ang_c = (t % GRID_W).astype(np.float32)[:, None] * freqs
    cos = np.concatenate([np.cos(ang_r)] * 2 + [np.cos(ang_c)] * 2, axis=1)
    sin = np.concatenate([-np.sin(ang_r), np.sin(ang_r), -np.sin(ang_c), np.sin(ang_c)], axis=1)
    reps = LANES // HEAD_DIM
    return (jnp.asarray(np.tile(cos, (1, reps)), F32), jnp.asarray(np.tile(sin, (1, reps)), F32))


def _block_geometry(n):
    nblk = n // ATT_BLOCK
    a = np.arange(ATT_BLOCK)
    out = []
    for i in (0, 1, nblk - 1):
        j0 = int(np.clip(i - 1, 0, nblk - 3))
        out.append((i * ATT_BLOCK + a, [(j0 + j) * ATT_BLOCK + a for j in range(3)]))
    return out


def _neighbourhood_bias(rpb, n):
    rows = n // GRID_W
    win_r = min(NA_WIN_R, rows)
    blk_rows = ATT_BLOCK // GRID_W
    cols = np.arange(GRID_W)
    col_off = cols[None, :] - cols[:, None] + NA_WIN_C - 1
    col_hot = jnp.asarray(col_off[:, :, None] == np.arange(2 * NA_WIN_C - 1), F32)
    c0 = np.clip(cols - NA_WIN_C // 2, 0, GRID_W - NA_WIN_C)
    col_ok = (cols[None, :] >= c0[:, None]) & (cols[None, :] < c0[:, None] + NA_WIN_C)
    kinds = []
    for q_tok, k_chunks in _block_geometry(n):
        qr = q_tok[::GRID_W] // GRID_W
        r0 = np.clip(qr - win_r // 2, 0, rows - win_r)
        chunks = []
        for k_tok in k_chunks:
            kr = k_tok[::GRID_W] // GRID_W
            row_off = kr[None, :] - qr[:, None] + NA_WIN_R - 1
            row_hot = jnp.asarray(row_off[:, :, None] == np.arange(2 * NA_WIN_R - 1), F32)
            row_ok = (kr[None, :] >= r0[:, None]) & (kr[None, :] < r0[:, None] + win_r)
            ok = (row_ok[:, None, :, None] & col_ok[None, :, None, :]).reshape(ATT_BLOCK, ATT_BLOCK)
            vals = jnp.einsum("qkr,hrc,xyc->hqxky", row_hot, rpb, col_hot, precision=HI)
            vals = vals.reshape(-1, blk_rows * GRID_W, blk_rows * GRID_W) * LOG2E
            chunks.append(jnp.where(ok[None], vals, NEG_INF))
        kinds.append(jnp.stack(chunks, axis=1))
    return jnp.stack(kinds, axis=0)


def _window_mask(n):
    kinds = []
    for q_tok, k_chunks in _block_geometry(n):
        kinds.append(np.stack([np.where(np.abs(k_tok[None] - q_tok[:, None]) <= WINDOW, 0.0, NEG_INF)
                               for k_tok in k_chunks])[None])
    return jnp.asarray(np.stack(kinds), F32)


def kernel(x, c, ctx, c_ctx, w_mod, b_mod, norm_mix, norm_ffn, w_in, rpb, q_norm, k_norm, sink, sgu_norm, w_sgu,
           b_sgu, out_norm, w_out, w_router, w_gate, w_up, w_down, final_norm):
    depth = w_mod.shape[0]
    b, n, d = x.shape
    nc = ctx.shape[1]
    group_w = MIXER_WIDTH // SG_GROUPS

    cvecs = jnp.concatenate([c, c_ctx[None], jnp.zeros((8 - b - 1, d), F32)], axis=0)
    mods = _adaln_all(cvecs, w_mod, b_mod).reshape(depth, 8, 6, d)

    cos, sin = _rope_tables(n)
    cos_c, sin_c = jnp.ones((nc, LANES), F32), jnp.zeros((nc, LANES), F32)
    win_mask = _window_mask(n)
    blk = np.arange(256) // HEAD_DIM
    gsum = jnp.asarray(blk[:, None] == blk[None, :], MXU_DTYPE)
    wg_all, wu_all, wd_all = (w.astype(MXU_DTYPE) for w in (w_gate, w_up, w_down))

    xc = ctx
    for l in range(depth):
        ctx_needed = l < depth - 1
        lat = [mods[l, :b, j][:, None, :] for j in range(6)]
        cm = [jnp.broadcast_to(mods[l, b, j][None, None, :], (b, 1, d)) for j in range(6)]
        lw = {
            "w_in": w_in[l].astype(MXU_DTYPE),
            "qn": jnp.tile(q_norm[l], 4)[None], "kn": jnp.tile(k_norm[l], 2)[None], "gn": sgu_norm[l][None],
            "w_sgu": w_sgu[l].astype(MXU_DTYPE),
            "b_sgu": jnp.repeat(b_sgu[l].T, group_w, axis=1),
            "gsum": gsum,
            "out_norm": out_norm[l][None], "w_out": w_out[l].astype(MXU_DTYPE), "norm_ffn": norm_ffn[l][None],
            "w_router_t": w_router[l].T,
            "layer": l, "w_gate": wg_all, "w_up": wu_all, "w_down": wd_all,
        }
        nm = norm_mix[l][None]
        hp = _inproj(x, nm, lat[0], lat[1], lw, cos, sin, 512)
        cp = _inproj(xc, nm, cm[0], cm[1], lw, cos_c, sin_c, nc)
        qa, ka, va, qb, kb, vb, qs, ks, vs, yd = hp
        _, ka_c, va_c, _, kb_c, vb_c, _, ks_c, vs_c, _ = cp
        sink_l = sink[l] * LOG2E

        ya = _local_attn(qa, ka, va, ka_c, va_c, _neighbourhood_bias(rpb[l], n))
        yb = _global_attn(qb, jnp.concatenate([kb, kb_c], axis=1), jnp.concatenate([vb, vb_c], axis=1))
        yc = _local_attn(qs, ks, vs, ks_c, vs_c, win_mask, sink_l)
        x_mid, h2, aff = _merge((ya, yb, yc, yd), x, lw, lat[2], lat[3], lat[4], 512)
        x = _moe_latent(x_mid, h2, aff, lat[5], lw, final_norm[None], l == depth - 1)

        if ctx_needed:
            ys_c = _ctx_attn(sink_l, cp[:9])
            xc_mid, hc2, aff_c = _merge((*ys_c, cp[9]), xc, lw, cm[2], cm[3], cm[4], nc)
            xc = _ctx_moe(xc_mid, hc2, aff_c, mods[l, b, 5][None], lw)
    return x
```

```python
import functools

import numpy as np
import jax
import jax.numpy as jnp
from jax import lax
from jax.experimental import pallas as pl
from jax.experimental.pallas import tpu as pltpu

HEAD_DIM = 64
GRID_W = 64
MIXER_WIDTH = 256
NA_WIN_R = 8
NA_WIN_C = 16
WINDOW = 128
CHUNK = 128
SG_GROUPS = 4
EC_CAPACITY_FACTOR = 2
ROPE_THETA = 10000.0
ROPE_FREQS = HEAD_DIM // 4
EPS = 1e-6
NEG_INF = -1e30
PROJ_SIZES = (256, 256, 256, 256, 128, 128, 256, 128, 128, 256, 256)
PROJ_OFFS = tuple(int(v) for v in np.cumsum((0,) + PROJ_SIZES))
LOG2E = 1.4426950408889634
Q_SCALE = HEAD_DIM ** -0.5 * LOG2E

LANES = 128
VMEM_LIMIT = 56 * 2 ** 20
ATT_BLOCK = 256
MXU_DTYPE = jnp.bfloat16
ACT_DTYPE = jnp.bfloat16
F32 = jnp.float32
HI = lax.Precision.HIGHEST
NT_DIMS = (((1,), (1,)), ((), ()))
TN_DIMS = (((0,), (0,)), ((), ()))


def _params(*sem):
    return pltpu.CompilerParams(dimension_semantics=sem, vmem_limit_bytes=VMEM_LIMIT)


def _mm(a, b):
    return jnp.dot(a.astype(MXU_DTYPE), b.astype(MXU_DTYPE), preferred_element_type=F32)


def _rms(x, g):
    return x * lax.rsqrt(jnp.mean(x * x, axis=-1, keepdims=True) + EPS) * g


def _silu(x):
    return x / (1.0 + jnp.exp(-x))


ROW_TILE = 8


def _read_row_tiles(ref, lead, n_rows):
    return jnp.concatenate([ref[lead + (pl.ds(a, n_rows, stride=ROW_TILE), slice(None))]
                            for a in range(ROW_TILE)], axis=-1)


def _write_row_tiles(ref, lead, val):
    for a in range(ROW_TILE):
        ref[lead + (pl.ds(a, val.shape[0], stride=ROW_TILE), slice(None))] = val[:, a * LANES:(a + 1) * LANES]


def _mod_body(c_ref, w_ref, b_ref, o_ref):
    s = _silu(c_ref[...])
    o_ref[0] = jnp.dot(s, w_ref[0], precision=HI, preferred_element_type=F32) + b_ref[0]


def _adaln_all(cvecs, w_mod, b_mod):
    depth, d, d6 = w_mod.shape
    tn = 1536
    rows = cvecs.shape[0]
    return pl.pallas_call(
        _mod_body,
        out_shape=jax.ShapeDtypeStruct((depth, rows, d6), F32),
        grid=(depth, d6 // tn),
        in_specs=[pl.BlockSpec((rows, d), lambda l, j: (0, 0)),
                  pl.BlockSpec((1, d, tn), lambda l, j: (l, 0, j)),
                  pl.BlockSpec((1, 1, tn), lambda l, j: (l, 0, j))],
        out_specs=pl.BlockSpec((1, rows, tn), lambda l, j: (l, 0, j)),
        compiler_params=_params("parallel", "parallel"),
        name="adaln",
    )(cvecs, w_mod, b_mod.reshape(depth, 1, d6))


def _head_rms(t, g, gsum_ref):
    w = t.shape[-1]
    sq = t * t
    hi = sq.astype(MXU_DTYPE)
    lo = (sq - hi.astype(F32)).astype(MXU_DTYPE)
    gs = gsum_ref[0:w, 0:w]
    ss = jnp.dot(hi, gs, preferred_element_type=F32) + jnp.dot(lo, gs, preferred_element_type=F32)
    return t * lax.rsqrt(ss * (1.0 / HEAD_DIM) + EPS) * g


def _rope(t, cos, sin_signed):
    w = t.shape[-1]
    rep = w // LANES
    if rep > 1:
        cos = jnp.concatenate([cos] * rep, axis=-1)
        sin_signed = jnp.concatenate([sin_signed] * rep, axis=-1)
    lane = lax.broadcasted_iota(jnp.int32, t.shape, 1)
    first_half = (lane % (2 * ROPE_FREQS)) < ROPE_FREQS
    partner = jnp.where(first_half, pltpu.roll(t, w - ROPE_FREQS, 1), pltpu.roll(t, ROPE_FREQS, 1))
    return t * cos + partner * sin_signed


def _with_ones_lane(v):
    lane = lax.broadcasted_iota(jnp.int32, (v.shape[0], LANES - HEAD_DIM), 1)
    pad = jnp.where(lane == 0, 1.0, 0.0).astype(v.dtype)
    parts = []
    for h in range(v.shape[1] // HEAD_DIM):
        parts += [v[:, h * HEAD_DIM:(h + 1) * HEAD_DIM], pad]
    return jnp.concatenate(parts, axis=-1)


def _inproj_body(x_ref, nw_ref, sh_ref, sc_ref, w_ref, cos_ref, sin_ref, qn_ref, kn_ref, gn_ref,
                 ws_ref, bs_ref, gsum_ref,
                 qa_ref, ka_ref, va_ref, qb_ref, kb_ref, vb_ref, qs_ref, ks_ref, vs_ref, yd_ref):
    x = x_ref[0]
    h = _rms(x, nw_ref[...]) * (1.0 + sc_ref[0]) + sh_ref[0]
    p = _mm(h, w_ref[...])
    o = PROJ_OFFS
    cos, sin = cos_ref[...], sin_ref[...]
    dt = qa_ref.dtype
    qa_ref[0] = (p[:, o[0]:o[1]] * Q_SCALE).astype(dt)
    ka_ref[0] = p[:, o[1]:o[2]].astype(dt)
    va_ref[0] = _with_ones_lane(p[:, o[2]:o[3]].astype(dt))
    qb = _rope(_head_rms(p[:, o[3]:o[4]], qn_ref[...], gsum_ref), cos, sin)
    qb_ref[0] = (qb * Q_SCALE).astype(dt)
    kb_ref[0] = _rope(_head_rms(p[:, o[4]:o[5]], kn_ref[...], gsum_ref), cos, sin).astype(dt)
    vb_ref[0] = _with_ones_lane(p[:, o[5]:o[6]].astype(dt))
    qs_ref[0] = (_rope(p[:, o[6]:o[7]], cos, sin) * Q_SCALE).astype(dt)
    ks_ref[0] = _rope(p[:, o[7]:o[8]], cos, sin).astype(dt)
    vs_ref[0] = _with_ones_lane(p[:, o[8]:o[9]].astype(dt))
    u = jax.nn.gelu(p[:, o[9]:o[10]])
    v = _rms(jax.nn.gelu(p[:, o[10]:o[11]]), gn_ref[...]).astype(MXU_DTYPE)
    lane_group = lax.broadcasted_iota(jnp.int32, (CHUNK, MIXER_WIDTH), 1) // (MIXER_WIDTH // SG_GROUPS)
    for c in range(x.shape[0] // CHUNK):
        rows = slice(c * CHUNK, (c + 1) * CHUNK)
        mixed = bs_ref[...]
        for g in range(SG_GROUPS):
            mg = jnp.dot(ws_ref[g], v[rows], preferred_element_type=F32)
            mixed = mixed + jnp.where(lane_group == g, mg, 0.0)
        yd_ref[0, rows, :] = (u[rows] * mixed).astype(dt)


def _inproj(x, nw, shift, scale, lw, cos, sin, tm):
    b, t, d = x.shape
    widths = tuple(w * (LANES // HEAD_DIM if j % 3 == 2 else 1) for j, w in enumerate(PROJ_SIZES[:9]))
    widths += (MIXER_WIDTH,)
    row = lambda bi, i: (0, 0)
    per_b = lambda bi, i: (bi, 0, 0)
    tile = lambda bi, i: (bi, i, 0)
    return pl.pallas_call(
        _inproj_body,
        out_shape=[jax.ShapeDtypeStruct((b, t, w), ACT_DTYPE) for w in widths],
        grid=(b, t // tm),
        in_specs=[pl.BlockSpec((1, tm, d), tile),
                  pl.BlockSpec((1, d), row),
                  pl.BlockSpec((1, 1, d), per_b),
                  pl.BlockSpec((1, 1, d), per_b),
                  pl.BlockSpec(lw["w_in"].shape, row),
                  pl.BlockSpec((tm, LANES), lambda bi, i: (i, 0)),
                  pl.BlockSpec((tm, LANES), lambda bi, i: (i, 0)),
                  pl.BlockSpec((1, 256), row),
                  pl.BlockSpec((1, 128), row),
                  pl.BlockSpec((1, 256), row),
                  pl.BlockSpec((SG_GROUPS, CHUNK, CHUNK), lambda bi, i: (0, 0, 0)),
                  pl.BlockSpec((CHUNK, MIXER_WIDTH), row),
                  pl.BlockSpec((256, 256), row)],
        out_specs=[pl.BlockSpec((1, tm, w), tile) for w in widths],
        compiler_params=_params("parallel", "parallel"),
        name="inproj",
    )(x, nw, shift, scale, lw["w_in"], cos, sin, lw["qn"], lw["kn"], lw["gn"], lw["w_sgu"],
      lw["b_sgu"], lw["gsum"])


def _attend_all(jobs):
    staged = []
    for q, chunks, sink in jobs:
        scores = []
        for k, _, bias in chunks:
            s = lax.dot_general(q, k, NT_DIMS, preferred_element_type=F32)
            scores.append(s if bias is None else s + bias)
        m = jnp.max(functools.reduce(jnp.maximum, scores), axis=-1, keepdims=True)
        staged.append((scores, m if sink is None else jnp.maximum(m, sink)))
    outs = []
    for (q, chunks, sink), (scores, m) in zip(jobs, staged):
        acc = jnp.zeros((q.shape[0], LANES), F32)
        for s, (_, v, _) in zip(scores, chunks):
            acc = acc + jnp.dot(jnp.exp2(s - m).astype(v.dtype), v, preferred_element_type=F32)
        l = acc[:, HEAD_DIM:HEAD_DIM + 1]
        if sink is not None:
            l = l + jnp.exp2(sink - m)
        outs.append(acc[:, :HEAD_DIM] / l)
    return outs


def _head(ref, h, width=HEAD_DIM):
    return ref[0, :, h * width:(h + 1) * width]


def _local_attn_body(*refs, group, has_sink):
    if has_sink:
        sink_ref, refs = refs[0], refs[1:]
    q_ref, k0, k1, k2, v0, v1, v2, kc_ref, vc_ref, bias_ref, o_ref = refs
    jobs = []
    for h in range(q_ref.shape[-1] // HEAD_DIM):
        kv = h // group
        hb = h if bias_ref.shape[1] > 1 else 0
        chunks = [(_head(kr, kv), _head(vr, kv, LANES), bias_ref[0, hb, j])
                  for j, (kr, vr) in enumerate(((k0, v0), (k1, v1), (k2, v2)))]
        chunks.append((_head(kc_ref, kv), _head(vc_ref, kv, LANES), None))
        jobs.append((_head(q_ref, h), chunks, sink_ref[h] if has_sink else None))
    o_ref[0] = jnp.concatenate(_attend_all(jobs), axis=-1).astype(o_ref.dtype)


def _local_attn(q, k, v, kc, vc, bias, sink=None):
    b, n, qw = q.shape
    kw = k.shape[-1]
    nc = kc.shape[1]
    tq = ATT_BLOCK
    nblk = n // tq
    assert nblk >= 4
    group = qw // kw
    hb = bias.shape[1]

    def kmap(j):
        return lambda bi, i: (bi, jnp.clip(i - 1, 0, nblk - 3) + j, 0)

    def bmap(bi, i):
        return (jnp.where(i == 0, 0, jnp.where(i == nblk - 1, 2, 1)), 0, 0, 0, 0)

    in_specs = [pl.BlockSpec((1, tq, qw), lambda bi, i: (bi, i, 0))]
    vw = v.shape[-1]
    assert nc == tq
    in_specs += [pl.BlockSpec((1, tq, kw), kmap(j)) for j in range(3)]
    in_specs += [pl.BlockSpec((1, tq, vw), kmap(j)) for j in range(3)]
    in_specs += [pl.BlockSpec((1, nc, kw), lambda bi, i: (bi, 0, 0)),
                 pl.BlockSpec((1, nc, vw), lambda bi, i: (bi, 0, 0))]
    in_specs += [pl.BlockSpec((1, hb, 3, tq, tq), bmap)]
    args = [q, k, k, k, v, v, v, kc, vc, bias]
    if sink is not None:
        in_specs = [pl.BlockSpec(memory_space=pltpu.SMEM)] + in_specs
        args = [sink] + args
    return pl.pallas_call(
        functools.partial(_local_attn_body, group=group, has_sink=sink is not None),
        out_shape=jax.ShapeDtypeStruct((b, n, qw), ACT_DTYPE),
        grid=(b, nblk),
        in_specs=in_specs,
        out_specs=pl.BlockSpec((1, tq, qw), lambda bi, i: (bi, i, 0)),
        compiler_params=_params("parallel", "parallel"),
        name="local_attn",
    )(*args)


def _ctx_attn_body(sink_ref, qa, ka, va, qb, kb, vb, qs, ks, vs, oa, ob, oc):
    for q_ref, k_ref, v_ref, o_ref, group, use_sink in (
            (qa, ka, va, oa, 1, False), (qb, kb, vb, ob, 2, False), (qs, ks, vs, oc, 2, True)):
        jobs = [(_head(q_ref, h), [(_head(k_ref, h // group), _head(v_ref, h // group, LANES), None)],
                 sink_ref[h] if use_sink else None) for h in range(q_ref.shape[-1] // HEAD_DIM)]
        o_ref[0] = jnp.concatenate(_attend_all(jobs), axis=-1).astype(o_ref.dtype)


def _ctx_attn(sink, qkv):
    b, nc, _ = qkv[0].shape
    spec = lambda a: pl.BlockSpec((1, nc, a.shape[-1]), lambda bi: (bi, 0, 0))
    return pl.pallas_call(
        _ctx_attn_body,
        out_shape=[jax.ShapeDtypeStruct((b, nc, MIXER_WIDTH), ACT_DTYPE)] * 3,
        grid=(b,),
        in_specs=[pl.BlockSpec(memory_space=pltpu.SMEM)] + [spec(a) for a in qkv],
        out_specs=[pl.BlockSpec((1, nc, MIXER_WIDTH), lambda bi: (bi, 0, 0))] * 3,
        compiler_params=_params("parallel"),
        name="ctx_attn",
    )(sink, *qkv)


GLOBAL_UNROLL = 6


def _global_attn_body(q_ref, kt_ref, v_ref, o_ref, s_a, s_b, *, tk):
    tq = q_ref.shape[1]
    n_kv = kt_ref.shape[1] // HEAD_DIM
    n_chunks = kt_ref.shape[2] // tk
    group_w = 2 * HEAD_DIM
    qs = [jnp.concatenate([q_ref[0, :, kv * group_w:kv * group_w + HEAD_DIM],
                           q_ref[0, :, kv * group_w + HEAD_DIM:(kv + 1) * group_w]], axis=0)
          for kv in range(n_kv)]

    def scores(i, s_ref):
        ks = pl.multiple_of(i * tk, tk)
        for kv in range(n_kv):
            s_ref[kv] = jnp.dot(qs[kv], kt_ref[0, kv * HEAD_DIM:(kv + 1) * HEAD_DIM, pl.ds(ks, tk)],
                                preferred_element_type=F32)

    def update(i, s_ref, carry):
        ks = pl.multiple_of(i * tk, tk)
        out = []
        for kv in range(n_kv):
            m, acc = carry[kv]
            s = s_ref[kv]
            m_new = jnp.maximum(m, jnp.max(s, axis=-1, keepdims=True))
            p = jnp.exp2(s - m_new).astype(v_ref.dtype)
            pv = jnp.dot(p, v_ref[0, pl.ds(ks, tk), kv * LANES:(kv + 1) * LANES], preferred_element_type=F32)
            out.append((m_new, jnp.exp2(m - m_new) * acc + pv))
        return tuple(out)

    carry = tuple((jnp.full((2 * tq, 1), NEG_INF, F32), jnp.zeros((2 * tq, LANES), F32)) for _ in range(n_kv))
    scores(0, s_a)

    def pair(j, carry):
        scores(2 * j + 1, s_b)
        carry = update(2 * j, s_a, carry)
        scores(2 * j + 2, s_a)
        return update(2 * j + 1, s_b, carry)

    n_pairs = (n_chunks - 1) // 2
    unroll = GLOBAL_UNROLL if n_pairs % GLOBAL_UNROLL == 0 else 1

    def pairs(j, carry):
        for u in range(unroll):
            carry = pair(j * unroll + u, carry)
        return carry

    carry = lax.fori_loop(0, n_pairs // unroll, pairs, carry)
    if n_chunks % 2 == 0:
        scores(n_chunks - 1, s_b)
        carry = update(n_chunks - 2, s_a, carry)
        carry = update(n_chunks - 1, s_b, carry)
    else:
        carry = update(n_chunks - 1, s_a, carry)
    outs = []
    for _, acc in carry:
        o = acc[:, :HEAD_DIM] / acc[:, HEAD_DIM:HEAD_DIM + 1]
        outs += [o[:tq], o[tq:]]
    o_ref[0] = jnp.concatenate(outs, axis=-1).astype(o_ref.dtype)


def _global_attn(q, k, v):
    b, n, qw = q.shape
    nk, kw = k.shape[1:]
    n_kv = kw // HEAD_DIM
    tq = ATT_BLOCK
    tk = 1280 if (nk % 1280 == 0 and nk > 1280) else 256
    kt = jnp.swapaxes(k, 1, 2)
    return pl.pallas_call(
        functools.partial(_global_attn_body, tk=tk),
        out_shape=jax.ShapeDtypeStruct((b, n, qw), ACT_DTYPE),
        grid=(b, n // tq),
        in_specs=[pl.BlockSpec((1, tq, qw), lambda bi, i: (bi, i, 0)),
                  pl.BlockSpec((1, kw, nk), lambda bi, i: (bi, 0, 0)),
                  pl.BlockSpec((1, nk, n_kv * LANES), lambda bi, i: (bi, 0, 0))],
        out_specs=pl.BlockSpec((1, tq, qw), lambda bi, i: (bi, i, 0)),
        scratch_shapes=[pltpu.VMEM((n_kv, 2 * tq, tk), F32), pltpu.VMEM((n_kv, 2 * tq, tk), F32)],
        compiler_params=_params("parallel", "parallel"),
        name="global_attn",
    )(q, kt, v)


def _merge_body(ya, yb, yc, yd, x_ref, on_ref, wo_ref, g1_ref, nf_ref, sh_ref, sc_ref, wr_ref,
                xo_ref, h_ref, aff_ref):
    parts = []
    for j, r in enumerate((ya, yb, yc, yd)):
        y = r[0].astype(F32)
        parts.append(_rms(y, on_ref[:, j * MIXER_WIDTH:(j + 1) * MIXER_WIDTH]).astype(MXU_DTYPE))
    xn = x_ref[0] + g1_ref[0] * _mm(jnp.concatenate(parts, axis=-1), wo_ref[...])
    xo_ref[0] = xn
    h = _rms(xn, nf_ref[...]) * (1.0 + sc_ref[0]) + sh_ref[0]
    _write_row_tiles(h_ref, (0,), h)
    logits = lax.dot_general(wr_ref[...], h, NT_DIMS, precision=HI, preferred_element_type=F32)
    e = jnp.exp(logits - jnp.max(logits, axis=0, keepdims=True))
    aff_ref[0] = e / jnp.sum(e, axis=0, keepdims=True)


def _merge(ys, x, lw, g1, shift, scale, tm):
    b, t, d = x.shape
    ne = lw["w_router_t"].shape[0]
    row = lambda bi, i: (0, 0)
    per_b = lambda bi, i: (bi, 0, 0)
    tile = lambda bi, i: (bi, i, 0)
    return pl.pallas_call(
        _merge_body,
        out_shape=[jax.ShapeDtypeStruct((b, t, d), F32), jax.ShapeDtypeStruct((b, t * ROW_TILE, LANES), F32),
                   jax.ShapeDtypeStruct((b, ne, t), F32)],
        grid=(b, t // tm),
        in_specs=[pl.BlockSpec((1, tm, MIXER_WIDTH), tile)] * 4 + [
            pl.BlockSpec((1, tm, d), tile),
            pl.BlockSpec((1, d), row),
            pl.BlockSpec((d, d), row),
            pl.BlockSpec((1, 1, d), per_b),
            pl.BlockSpec((1, d), row),
            pl.BlockSpec((1, 1, d), per_b),
            pl.BlockSpec((1, 1, d), per_b),
            pl.BlockSpec((ne, d), row)],
        out_specs=[pl.BlockSpec((1, tm, d), tile), pl.BlockSpec((1, tm * ROW_TILE, LANES), tile),
                   pl.BlockSpec((1, ne, tm), lambda bi, i: (bi, 0, i))],
        compiler_params=_params("parallel", "parallel"),
        name="merge",
    )(*ys, x, lw["out_norm"], lw["w_out"], g1, lw["norm_ffn"], shift, scale, lw["w_router_t"])


def _tri(n, m, mode):
    r = lax.broadcasted_iota(jnp.int32, (n, m), 0)
    c = lax.broadcasted_iota(jnp.int32, (n, m), 1)
    return jnp.where({"lt": r < c, "le": r <= c, "gt": r > c}[mode], 1.0, 0.0).astype(MXU_DTYPE)


def _count(mask, axes):
    out = jnp.where(mask, 1.0, 0.0)
    for ax in sorted(axes, reverse=True):
        out = jnp.sum(out, axis=ax, keepdims=True)
    return out


def _kth_largest_bits(bits, cap, axes):
    shape = tuple(1 if a in axes else s for a, s in enumerate(bits.shape))

    def body(i, t):
        cand = t | lax.shift_left(jnp.int32(1), 30 - i)
        return jnp.where(_count(bits >= cand, axes) >= cap, cand, t)

    return lax.fori_loop(0, 31, body, jnp.zeros(shape, jnp.int32))


def _prefix_tokens(m, exact_rows):
    e, r, l = m.shape
    m2 = m.reshape(e * r, l).astype(MXU_DTYPE)
    local = jnp.dot(m2, _tri(l, l, "lt"), preferred_element_type=F32)
    rowtot = jnp.dot(m2, jnp.ones((l, l), MXU_DTYPE), preferred_element_type=F32).reshape(e, r, l)
    below = _tri(r, r, "gt")
    if exact_rows:
        base = [jnp.dot(below, rowtot[i].astype(MXU_DTYPE), preferred_element_type=F32) for i in range(e)]
    else:
        base = [jnp.dot(below.astype(F32), rowtot[i], precision=HI, preferred_element_type=F32)
                for i in range(e)]
    return local.reshape(e, r, l) + jnp.stack(base, axis=0)


def _select_mask(aff, cap, prefix_fn, axes):
    bits = pltpu.bitcast(aff, jnp.int32)
    thr = _kth_largest_bits(bits, cap, axes)
    gt = bits > thr
    eq = bits == thr
    need = cap - _count(gt, axes)
    eq_rank = prefix_fn(jnp.where(eq, 1.0, 0.0))
    take_eq = jnp.where(eq, jnp.where(eq_rank < need, 1.0, 0.0), 0.0)
    return jnp.where(gt, 1.0, take_eq)


def _select_body(aff_ref, sel_ref, prank_ref, tstart_ref, tend_ref, *, cap):
    aff = aff_ref[0]
    ne = aff.shape[0]
    sel = _select_mask(aff, cap, functools.partial(_prefix_tokens, exact_rows=True), (1, 2))
    sel_ref[0] = sel
    cnt = jnp.sum(sel, axis=0)
    tstart = _prefix_tokens(cnt[None], exact_rows=False)[0]
    tstart_ref[0] = tstart.astype(jnp.int32)
    tend_ref[0] = (tstart + cnt).astype(jnp.int32)
    run = tstart
    for e in range(ne):
        prank_ref[0, e] = run.astype(jnp.int32)
        run = run + sel[e]


def _select(aff4, cap):
    b, ne, r, l = aff4.shape
    blk4 = pl.BlockSpec((1, ne, r, l), lambda bi: (bi, 0, 0, 0))
    blk3 = pl.BlockSpec((1, r, l), lambda bi: (bi, 0, 0))
    return pl.pallas_call(
        functools.partial(_select_body, cap=cap),
        out_shape=[jax.ShapeDtypeStruct((b, ne, r, l), F32), jax.ShapeDtypeStruct((b, ne, r, l), jnp.int32),
                   jax.ShapeDtypeStruct((b, r, l), jnp.int32), jax.ShapeDtypeStruct((b, r, l), jnp.int32)],
        grid=(b,),
        in_specs=[blk4],
        out_specs=[blk4, blk4, blk3, blk3],
        compiler_params=_params("parallel"),
        name="moe_select",
    )(aff4)


def _slots_body(sel_ref, aff_ref, prank_ref, idx_ref, dest_ref, gate_ref):
    m = sel_ref[0, 0]
    r, l = m.shape
    cap = idx_ref.shape[2]
    mb = m.astype(MXU_DTYPE)
    linc = jnp.dot(mb, _tri(l, l, "le"), preferred_element_type=F32)
    rowtot = jnp.dot(mb, jnp.ones((l, l), MXU_DTYPE), preferred_element_type=F32)
    rowtot_lane = lax.dot_general(jnp.ones((8, l), MXU_DTYPE), mb, NT_DIMS, preferred_element_type=F32)
    cumrow = jnp.dot(rowtot_lane.astype(MXU_DTYPE), _tri(r, r, "le"), preferred_element_type=F32)[0:1]
    slot = lax.broadcasted_iota(jnp.int32, (cap, r), 0).astype(F32)
    passed = jnp.where(cumrow <= slot, 1.0, 0.0).astype(MXU_DTYPE)
    row_of = jnp.dot(passed, jnp.ones((r, l), MXU_DTYPE), preferred_element_type=F32)[:, 0:1]
    base_of = jnp.dot(passed, rowtot.astype(MXU_DTYPE), preferred_element_type=F32)[:, 0:1]
    onehot = jnp.where(lax.broadcasted_iota(jnp.int32, (cap, r), 1).astype(F32) == row_of, 1.0, 0.0)
    linc_of = jnp.dot(onehot.astype(MXU_DTYPE), linc.astype(MXU_DTYPE), preferred_element_type=F32)
    k = slot[:, 0:1] - base_of
    col_of = jnp.sum(jnp.where(linc_of <= k, 1.0, 0.0), axis=-1, keepdims=True)
    idx_ref[0, 0] = (row_of * l + col_of).astype(jnp.int32)
    at_col = lax.broadcasted_iota(jnp.int32, (cap, l), 1).astype(F32) == col_of
    hot = onehot.astype(MXU_DTYPE)
    aff_rows, rest = jnp.zeros((cap, l), F32), aff_ref[0, 0]
    for _ in range(3):
        piece = rest.astype(MXU_DTYPE)
        aff_rows = aff_rows + jnp.dot(hot, piece, preferred_element_type=F32)
        rest = rest - piece.astype(F32)
    gate_ref[0, 0] = jnp.sum(jnp.where(at_col, aff_rows, 0.0), axis=-1, keepdims=True)
    prank = prank_ref[0, 0]
    prank_rows = (jnp.dot(hot, (prank >> 8).astype(F32).astype(MXU_DTYPE), preferred_element_type=F32) * 256.0
                  + jnp.dot(hot, (prank & 255).astype(F32).astype(MXU_DTYPE), preferred_element_type=F32))
    dest_ref[0, 0] = jnp.sum(jnp.where(at_col, prank_rows, 0.0), axis=-1, keepdims=True).astype(jnp.int32)


def _slots(sel, aff4, prank, cap):
    b, ne, r, l = sel.shape
    blk = pl.BlockSpec((1, 1, r, l), lambda bi, e: (bi, e, 0, 0))
    oblk = pl.BlockSpec((1, 1, cap, 1), lambda bi, e: (bi, e, 0, 0))
    return pl.pallas_call(
        _slots_body,
        out_shape=[jax.ShapeDtypeStruct((b, ne, cap, 1), jnp.int32), jax.ShapeDtypeStruct((b, ne, cap, 1), jnp.int32),
                   jax.ShapeDtypeStruct((b, ne, cap, 1), F32)],
        grid=(b, ne),
        in_specs=[blk, blk, blk],
        out_specs=[oblk, oblk, oblk],
        compiler_params=_params("parallel", "parallel"),
        name="moe_slots",
    )(sel, aff4, prank)


FFN_CHUNK = 256


def _ffn_body(idx_ref, dest_ref, h_hbm, gate_ref, wg_ref, wu_ref, wd_ref, z_hbm, xa, xb, ya, yb, sems,
              *, n_tok, n_pair):
    n_b, tiles = pl.num_programs(1), pl.num_programs(2)
    step = (pl.program_id(0) * n_b + pl.program_id(1)) * tiles + pl.program_id(2)
    last = pl.num_programs(0) * n_b * tiles - 1
    ts = xa.shape[0] // ROW_TILE
    n_groups = wg_ref.shape[2] // FFN_CHUNK
    per_group = ts // n_groups

    def sample_of(k):
        return (k // tiles) % n_b

    def tile_of(ref, r):
        return ref.at[pl.ds(pl.multiple_of(r * ROW_TILE, ROW_TILE), ROW_TILE), :]

    def gather(k, half, buf, sem):
        base, rows = (2 * k + half) * ts, sample_of(k) * n_tok
        return lambda s: pltpu.make_async_copy(
            tile_of(h_hbm, rows + idx_ref[base + s]), tile_of(buf, s), sems.at[sem])

    def scatter(k, half, buf, sem):
        base, rows = (2 * k + half) * ts, sample_of(k) * n_pair
        return lambda s: pltpu.make_async_copy(
            tile_of(buf, s), tile_of(z_hbm, rows + dest_ref[base + s]), sems.at[sem])

    def start_all(copy):
        def body(s, c):
            copy(s).start()
            return c
        lax.fori_loop(0, ts, body, 0, unroll=8)

    def wait_rows(buf, sem):
        pltpu.make_async_copy(buf, buf, sems.at[sem]).wait()

    def ffn(xbuf, gate, copies):
        x = _read_row_tiles(xbuf, (), ts).astype(MXU_DTYPE)
        y = None
        for j in range(n_groups):
            for copy in copies:
                for s in range(j * per_group, (j + 1) * per_group):
                    copy(s).start()
            cols = slice(j * FFN_CHUNK, (j + 1) * FFN_CHUNK)
            hid = _silu(_mm(x, wg_ref[0, :, cols])) * _mm(x, wu_ref[0, :, cols])
            part = _mm(hid, wd_ref[0, cols, :])
            y = part if y is None else y + part
        return y * gate

    @pl.when(step == 0)
    def _():
        start_all(gather(step, 0, xa, 0))

    wait_rows(xa, 0)
    y_a = ffn(xa, gate_ref[0, 0, 0:ts], [gather(step, 1, xb, 1)])

    @pl.when(step > 0)
    def _():
        wait_rows(ya, 2)
        wait_rows(yb, 3)

    _write_row_tiles(ya, (), y_a)
    wait_rows(xb, 1)
    nxt = jnp.minimum(step + 1, last)
    y_b = ffn(xb, gate_ref[0, 0, ts:2 * ts], [scatter(step, 0, ya, 2), gather(nxt, 0, xa, 0)])
    _write_row_tiles(yb, (), y_b)
    start_all(scatter(step, 1, yb, 3))

    @pl.when(step == last)
    def _():
        wait_rows(xa, 0)
        wait_rows(ya, 2)
        wait_rows(yb, 3)


def _expert_ffn(idx, dest, gate, h, lw, ts):
    b, ne, cap, _ = gate.shape
    n, d = h.shape[1] // ROW_TILE, ROW_TILE * LANES
    f = lw["w_gate"].shape[-1]
    layer = lw["layer"]
    n_pair = ne * cap
    by_expert = lambda a: jnp.swapaxes(a, 0, 1)
    grid_spec = pltpu.PrefetchScalarGridSpec(
        num_scalar_prefetch=2,
        grid=(ne, b, cap // (2 * ts)),
        in_specs=[pl.BlockSpec(memory_space=pl.ANY),
                  pl.BlockSpec((1, 1, 2 * ts, 1), lambda e, bi, t, *_: (e, bi, t, 0)),
                  pl.BlockSpec((None, 1, d, f), lambda e, bi, t, *_: (layer, e, 0, 0)),
                  pl.BlockSpec((None, 1, d, f), lambda e, bi, t, *_: (layer, e, 0, 0)),
                  pl.BlockSpec((None, 1, f, d), lambda e, bi, t, *_: (layer, e, 0, 0))],
        out_specs=pl.BlockSpec(memory_space=pl.ANY),
        scratch_shapes=[pltpu.VMEM((ts * ROW_TILE, LANES), F32)] * 4 + [pltpu.SemaphoreType.DMA((4,))])
    return pl.pallas_call(
        functools.partial(_ffn_body, n_tok=n, n_pair=n_pair),
        out_shape=jax.ShapeDtypeStruct((b * n_pair * ROW_TILE, LANES), F32),
        grid_spec=grid_spec,
        compiler_params=_params("arbitrary", "arbitrary", "arbitrary"),
        name="moe_ffn",
    )(by_expert(idx).reshape(-1), by_expert(dest).reshape(-1), h.reshape(b * n * ROW_TILE, LANES), by_expert(gate),
      lw["w_gate"], lw["w_up"], lw["w_down"])


COMBINE_DEPTH = 4


def _combine_body(tb_ref, x_ref, g2_ref, ts_ref, te_ref, fg_ref, z_hbm, o_ref, zbuf, acc_ref, sem,
                  *, n_pair, final):
    n_tiles = pl.num_programs(1)
    tile = pl.program_id(0) * n_tiles + pl.program_id(1)
    depth, pc = zbuf.shape[0], zbuf.shape[1] // ROW_TILE
    tt = x_ref.shape[1]

    def plan(g):
        b, i = g // n_tiles, g % n_tiles
        lo, hi = tb_ref[b * (n_tiles + 1) + i], tb_ref[b * (n_tiles + 1) + i + 1]
        p0 = (lo // 8) * 8
        return b, p0, (hi - p0 + pc - 1) // pc

    def fetch(b, p0, k, slot):
        cs = pl.multiple_of(jnp.minimum(p0 + k * pc, n_pair - pc), 8)
        rows = pl.ds(pl.multiple_of((b * n_pair + cs) * ROW_TILE, 8 * ROW_TILE), pc * ROW_TILE)
        return cs, pltpu.make_async_copy(z_hbm.at[rows, :], zbuf.at[slot], sem.at[slot])

    b, p0, n_chunks = plan(tile)
    start_row, end_row = ts_ref[0, 0], te_ref[0, 0]
    acc_ref[...] = jnp.zeros_like(acc_ref)

    def start_head(b, p0, n_chunks):
        for k in range(depth - 1):
            @pl.when(k < n_chunks)
            def _():
                fetch(b, p0, k, k)[1].start()

    @pl.when(tile == 0)
    def _():
        start_head(b, p0, n_chunks)

    def chunk(k, c):
        slot = k % depth
        ahead = k + depth - 1

        @pl.when(ahead < n_chunks)
        def _():
            fetch(b, p0, ahead, ahead % depth)[1].start()

        cs, cp = fetch(b, p0, k, slot)
        cp.wait()
        pair = cs + lax.broadcasted_iota(jnp.int32, (pc, tt), 0)
        own = jnp.where(pair >= jnp.maximum(start_row, p0 + k * pc), jnp.where(pair < end_row, 1.0, 0.0), 0.0)
        own = own.astype(MXU_DTYPE)
        z = _read_row_tiles(zbuf, (slot,), pc).astype(MXU_DTYPE)
        acc_ref[...] += lax.dot_general(own, z, TN_DIMS, preferred_element_type=F32)
        return c

    lax.fori_loop(0, n_chunks, chunk, 0)

    @pl.when(tile + 1 < pl.num_programs(0) * n_tiles)
    def _():
        start_head(*plan(tile + 1))

    out = x_ref[0] + g2_ref[0] * acc_ref[...]
    o_ref[0] = _rms(out, fg_ref[...]) if final else out


def _combine(x, g2, tstart, tend, z, n_pair, tt, pc, final_g, final):
    b, n, d = x.shape
    nt = n // tt
    ts4 = tstart.reshape(b, nt, 1, tt)
    te4 = tend.reshape(b, nt, 1, tt)
    bounds = jnp.concatenate([ts4[:, :, 0, 0], jnp.full((b, 1), n_pair, jnp.int32)], axis=1).reshape(-1)
    grid_spec = pltpu.PrefetchScalarGridSpec(
        num_scalar_prefetch=1,
        grid=(b, nt),
        in_specs=[pl.BlockSpec((1, tt, d), lambda bi, i, *_: (bi, i, 0)),
                  pl.BlockSpec((1, 1, d), lambda bi, i, *_: (bi, 0, 0)),
                  pl.BlockSpec((1, 1, 1, tt), lambda bi, i, *_: (bi, i, 0, 0)),
                  pl.BlockSpec((1, 1, 1, tt), lambda bi, i, *_: (bi, i, 0, 0)),
                  pl.BlockSpec((1, d), lambda bi, i, *_: (0, 0)),
                  pl.BlockSpec(memory_space=pl.ANY)],
        out_specs=pl.BlockSpec((1, tt, d), lambda bi, i, *_: (bi, i, 0)),
        scratch_shapes=[pltpu.VMEM((COMBINE_DEPTH, pc * ROW_TILE, LANES), F32), pltpu.VMEM((tt, d), F32),
                        pltpu.SemaphoreType.DMA((COMBINE_DEPTH,))])
    return pl.pallas_call(
        functools.partial(_combine_body, n_pair=n_pair, final=final),
        out_shape=jax.ShapeDtypeStruct((b, n, d), F32),
        grid_spec=grid_spec,
        compiler_params=_params("arbitrary", "arbitrary"),
        name="moe_combine",
    )(bounds, x, g2, ts4, te4, final_g, z)


def _moe_latent(x, h, aff, g2, lw, final_g, final):
    b, n, d = x.shape
    ne = aff.shape[1]
    cap = EC_CAPACITY_FACTOR * n // ne
    aff4 = aff.reshape(b, ne, n // LANES, LANES)
    sel, prank, tstart, tend = _select(aff4, cap)
    idx, dest, gate = _slots(sel, aff4, prank, cap)
    z = _expert_ffn(idx, dest, gate, h, lw, min(cap // 2, 256))
    assert ne * cap <= 1 << 16
    return _combine(x, g2, tstart, tend, z, ne * cap, 256, 256, final_g, final)


def _ctx_coef_body(aff_ref, coef_ref, slot_ref, *, cap):
    nb, _, nc = aff_ref.shape
    excl = _tri(nc, nc, "lt")
    prefix = lambda mm: jnp.dot(mm.astype(MXU_DTYPE), excl, preferred_element_type=F32)
    coefs, slots = [], []
    for b in range(nb):
        aff = aff_ref[b]
        sel = _select_mask(aff, cap, prefix, (1,))
        coefs.append(sel * aff)
        slots.append(jnp.where(sel > 0.0, prefix(sel) + float(b * cap), -1.0))
    coef_ref[:, 0, :] = jnp.concatenate(coefs, axis=-1)
    slot_ref[:, 0, :] = jnp.concatenate(slots, axis=-1)


def _ctx_coef(aff, cap):
    b, ne, nc = aff.shape
    return pl.pallas_call(
        functools.partial(_ctx_coef_body, cap=cap),
        out_shape=[jax.ShapeDtypeStruct((ne, 1, b * nc), F32)] * 2,
        name="ctx_moe_select",
    )(aff)


def _ctx_ffn_body(h_ref, coef_ref, slot_ref, wg_ref, wu_ref, wd_ref, x_ref, g2_ref, o_ref, *, n_slots):
    e = pl.program_id(0)
    rows = h_ref.shape[0] // ROW_TILE
    h = _read_row_tiles(h_ref, (), rows)
    picked = jnp.where(lax.broadcasted_iota(jnp.int32, (n_slots, rows), 0).astype(F32) == slot_ref[0], 1.0, 0.0)
    gate = jnp.sum(picked * coef_ref[0], axis=-1, keepdims=True)
    x = _mm(picked, h)
    y = _mm(_silu(_mm(x, wg_ref[0])) * _mm(x, wu_ref[0]), wd_ref[0]) * gate

    @pl.when(e == 0)
    def _():
        o_ref[...] = x_ref[...]

    o_ref[...] += g2_ref[...] * lax.dot_general(picked.astype(MXU_DTYPE), y.astype(MXU_DTYPE), TN_DIMS,
                                                 preferred_element_type=F32)


def _ctx_moe(xc, hc, aff, g2, lw):
    b, nc, d = xc.shape
    ne = aff.shape[1]
    f = lw["w_gate"].shape[-1]
    layer = lw["layer"]
    cap = EC_CAPACITY_FACTOR * nc // ne
    coef, slot = _ctx_coef(aff, cap)
    rows = b * nc
    per_e = pl.BlockSpec((1, 1, rows), lambda e: (e, 0, 0))
    out = pl.pallas_call(
        functools.partial(_ctx_ffn_body, n_slots=b * cap),
        out_shape=jax.ShapeDtypeStruct((rows, d), F32),
        grid=(ne,),
        in_specs=[pl.BlockSpec((rows * ROW_TILE, LANES), lambda e: (0, 0)),
                  per_e, per_e,
                  pl.BlockSpec((None, 1, d, f), lambda e: (layer, e, 0, 0)),
                  pl.BlockSpec((None, 1, d, f), lambda e: (layer, e, 0, 0)),
                  pl.BlockSpec((None, 1, f, d), lambda e: (layer, e, 0, 0)),
                  pl.BlockSpec((rows, d), lambda e: (0, 0)),
                  pl.BlockSpec((1, d), lambda e: (0, 0))],
        out_specs=pl.BlockSpec((rows, d), lambda e: (0, 0)),
        compiler_params=_params("arbitrary"),
        name="ctx_moe_ffn",
    )(hc.reshape(rows * ROW_TILE, LANES), coef, slot, lw["w_gate"], lw["w_up"], lw["w_down"],
      xc.reshape(rows, d), g2)
    return out.reshape(b, nc, d)


def _rope_tables(n):
    t = np.arange(n)
    freqs = ROPE_THETA ** (-np.arange(ROPE_FREQS, dtype=np.float32) / ROPE_FREQS)
    ang_r = (t // GRID_W).astype(np.float32)[:, None] * freqs
    ang_c = (t % GRID_W).astype(np.float32)[:, None] * freqs
    cos = np.concatenate([np.cos(ang_r)] * 2 + [np.cos(ang_c)] * 2, axis=1)
    sin = np.concatenate([-np.sin(ang_r), np.sin(ang_r), -np.sin(ang_c), np.sin(ang_c)], axis=1)
    reps = LANES // HEAD_DIM
    return (jnp.asarray(np.tile(cos, (1, reps)), F32), jnp.asarray(np.tile(sin, (1, reps)), F32))


def _block_geometry(n):
    nblk = n // ATT_BLOCK
    a = np.arange(ATT_BLOCK)
    out = []
    for i in (0, 1, nblk - 1):
        j0 = int(np.clip(i - 1, 0, nblk - 3))
        out.append((i * ATT_BLOCK + a, [(j0 + j) * ATT_BLOCK + a for j in range(3)]))
    return out


def _neighbourhood_bias(rpb, n):
    rows = n // GRID_W
    win_r = min(NA_WIN_R, rows)
    blk_rows = ATT_BLOCK // GRID_W
    cols = np.arange(GRID_W)
    col_off = cols[None, :] - cols[:, None] + NA_WIN_C - 1
    col_hot = jnp.asarray(col_off[:, :, None] == np.arange(2 * NA_WIN_C - 1), F32)
    c0 = np.clip(cols - NA_WIN_C // 2, 0, GRID_W - NA_WIN_C)
    col_ok = (cols[None, :] >= c0[:, None]) & (cols[None, :] < c0[:, None] + NA_WIN_C)
    kinds = []
    for q_tok, k_chunks in _block_geometry(n):
        qr = q_tok[::GRID_W] // GRID_W
        r0 = np.clip(qr - win_r // 2, 0, rows - win_r)
        chunks = []
        for k_tok in k_chunks:
            kr = k_tok[::GRID_W] // GRID_W
            row_off = kr[None, :] - qr[:, None] + NA_WIN_R - 1
            row_hot = jnp.asarray(row_off[:, :, None] == np.arange(2 * NA_WIN_R - 1), F32)
            row_ok = (kr[None, :] >= r0[:, None]) & (kr[None, :] < r0[:, None] + win_r)
            ok = (row_ok[:, None, :, None] & col_ok[None, :, None, :]).reshape(ATT_BLOCK, ATT_BLOCK)
            vals = jnp.einsum("qkr,hrc,xyc->hqxky", row_hot, rpb, col_hot, precision=HI)
            vals = vals.reshape(-1, blk_rows * GRID_W, blk_rows * GRID_W) * LOG2E
            chunks.append(jnp.where(ok[None], vals, NEG_INF))
        kinds.append(jnp.stack(chunks, axis=1))
    return jnp.stack(kinds, axis=0)


def _window_mask(n):
    kinds = []
    for q_tok, k_chunks in _block_geometry(n):
        kinds.append(np.stack([np.where(np.abs(k_tok[None] - q_tok[:, None]) <= WINDOW, 0.0, NEG_INF)
                               for k_tok in k_chunks])[None])
    return jnp.asarray(np.stack(kinds), F32)


def kernel(x, c, ctx, c_ctx, w_mod, b_mod, norm_mix, norm_ffn, w_in, rpb, q_norm, k_norm, sink, sgu_norm, w_sgu,
           b_sgu, out_norm, w_out, w_router, w_gate, w_up, w_down, final_norm):
    depth = w_mod.shape[0]
    b, n, d = x.shape
    nc = ctx.shape[1]
    group_w = MIXER_WIDTH // SG_GROUPS

    cvecs = jnp.concatenate([c, c_ctx[None], jnp.zeros((8 - b - 1, d), F32)], axis=0)
    mods = _adaln_all(cvecs, w_mod, b_mod).reshape(depth, 8, 6, d)

    cos, sin = _rope_tables(n)
    cos_c, sin_c = jnp.ones((nc, LANES), F32), jnp.zeros((nc, LANES), F32)
    win_mask = _window_mask(n)
    blk = np.arange(256) // HEAD_DIM
    gsum = jnp.asarray(blk[:, None] == blk[None, :], MXU_DTYPE)
    wg_all, wu_all, wd_all = (w.astype(MXU_DTYPE) for w in (w_gate, w_up, w_down))

    xc = ctx
    for l in range(depth):
        ctx_needed = l < depth - 1
        lat = [mods[l, :b, j][:, None, :] for j in range(6)]
        cm = [jnp.broadcast_to(mods[l, b, j][None, None, :], (b, 1, d)) for j in range(6)]
        lw = {
            "w_in": w_in[l].astype(MXU_DTYPE),
            "qn": jnp.tile(q_norm[l], 4)[None], "kn": jnp.tile(k_norm[l], 2)[None], "gn": sgu_norm[l][None],
            "w_sgu": w_sgu[l].astype(MXU_DTYPE),
            "b_sgu": jnp.repeat(b_sgu[l].T, group_w, axis=1),
            "gsum": gsum,
            "out_norm": out_norm[l][None], "w_out": w_out[l].astype(MXU_DTYPE), "norm_ffn": norm_ffn[l][None],
            "w_router_t": w_router[l].T,
            "layer": l, "w_gate": wg_all, "w_up": wu_all, "w_down": wd_all,
        }
        nm = norm_mix[l][None]
        hp = _inproj(x, nm, lat[0], lat[1], lw, cos, sin, 512)
        cp = _inproj(xc, nm, cm[0], cm[1], lw, cos_c, sin_c, nc)
        qa, ka, va, qb, kb, vb, qs, ks, vs, yd = hp
        _, ka_c, va_c, _, kb_c, vb_c, _, ks_c, vs_c, _ = cp
        sink_l = sink[l] * LOG2E

        ya = _local_attn(qa, ka, va, ka_c, va_c, _neighbourhood_bias(rpb[l], n))
        yb = _global_attn(qb, jnp.concatenate([kb, kb_c], axis=1), jnp.concatenate([vb, vb_c], axis=1))
        yc = _local_attn(qs, ks, vs, ks_c, vs_c, win_mask, sink_l)
        x_mid, h2, aff = _merge((ya, yb, yc, yd), x, lw, lat[2], lat[3], lat[4], 512)
        x = _moe_latent(x_mid, h2, aff, lat[5], lw, final_norm[None], l == depth - 1)

        if ctx_needed:
            ys_c = _ctx_attn(sink_l, cp[:9])
            xc_mid, hc2, aff_c = _merge((*ys_c, cp[9]), xc, lw, cm[2], cm[3], cm[4], nc)
            xc = _ctx_moe(xc_mid, hc2, aff_c, mods[l, b, 5][None], lw)
    return x
```

```python
import functools

import numpy as np
import jax
import jax.numpy as jnp
from jax import lax
from jax.experimental import pallas as pl
from jax.experimental.pallas import tpu as pltpu

HEAD_DIM = 64
GRID_W = 64
MIXER_WIDTH = 256
NA_WIN_R = 8
NA_WIN_C = 16
WINDOW = 128
CHUNK = 128
SG_GROUPS = 4
EC_CAPACITY_FACTOR = 2
ROPE_THETA = 10000.0
ROPE_FREQS = HEAD_DIM // 4
EPS = 1e-6
NEG_INF = -1e30
PROJ_SIZES = (256, 256, 256, 256, 128, 128, 256, 128, 128, 256, 256)
PROJ_OFFS = tuple(int(v) for v in np.cumsum((0,) + PROJ_SIZES))
LOG2E = 1.4426950408889634
Q_SCALE = HEAD_DIM ** -0.5 * LOG2E

LANES = 128
VMEM_LIMIT = 56 * 2 ** 20
ATT_BLOCK = 256
MXU_DTYPE = jnp.bfloat16
ACT_DTYPE = jnp.bfloat16
F32 = jnp.float32
HI = lax.Precision.HIGHEST
NT_DIMS = (((1,), (1,)), ((), ()))
TN_DIMS = (((0,), (0,)), ((), ()))


def _params(*sem):
    return pltpu.CompilerParams(dimension_semantics=sem, vmem_limit_bytes=VMEM_LIMIT)


def _mm(a, b):
    return jnp.dot(a.astype(MXU_DTYPE), b.astype(MXU_DTYPE), preferred_element_type=F32)


def _rms(x, g):
    return x * lax.rsqrt(jnp.mean(x * x, axis=-1, keepdims=True) + EPS) * g


def _silu(x):
    return x / (1.0 + jnp.exp(-x))


ROW_TILE = 8


def _read_row_tiles(ref, lead, n_rows):
    return jnp.concatenate([ref[lead + (pl.ds(a, n_rows, stride=ROW_TILE), slice(None))]
                            for a in range(ROW_TILE)], axis=-1)


def _write_row_tiles(ref, lead, val):
    for a in range(ROW_TILE):
        ref[lead + (pl.ds(a, val.shape[0], stride=ROW_TILE), slice(None))] = val[:, a * LANES:(a + 1) * LANES]


def _mod_body(c_ref, w_ref, b_ref, o_ref):
    s = _silu(c_ref[...])
    o_ref[0] = jnp.dot(s, w_ref[0], precision=HI, preferred_element_type=F32) + b_ref[0]


def _adaln_all(cvecs, w_mod, b_mod):
    depth, d, d6 = w_mod.shape
    tn = 1536
    rows = cvecs.shape[0]
    return pl.pallas_call(
        _mod_body,
        out_shape=jax.ShapeDtypeStruct((depth, rows, d6), F32),
        grid=(depth, d6 // tn),
        in_specs=[pl.BlockSpec((rows, d), lambda l, j: (0, 0)),
                  pl.BlockSpec((1, d, tn), lambda l, j: (l, 0, j)),
                  pl.BlockSpec((1, 1, tn), lambda l, j: (l, 0, j))],
        out_specs=pl.BlockSpec((1, rows, tn), lambda l, j: (l, 0, j)),
        compiler_params=_params("parallel", "parallel"),
        name="adaln",
    )(cvecs, w_mod, b_mod.reshape(depth, 1, d6))


def _head_rms(t, g, gsum_ref):
    w = t.shape[-1]
    sq = t * t
    hi = sq.astype(MXU_DTYPE)
    lo = (sq - hi.astype(F32)).astype(MXU_DTYPE)
    gs = gsum_ref[0:w, 0:w]
    ss = jnp.dot(hi, gs, preferred_element_type=F32) + jnp.dot(lo, gs, preferred_element_type=F32)
    return t * lax.rsqrt(ss * (1.0 / HEAD_DIM) + EPS) * g


def _rope(t, cos, sin_signed):
    w = t.shape[-1]
    rep = w // LANES
    if rep > 1:
        cos = jnp.concatenate([cos] * rep, axis=-1)
        sin_signed = jnp.concatenate([sin_signed] * rep, axis=-1)
    lane = lax.broadcasted_iota(jnp.int32, t.shape, 1)
    first_half = (lane % (2 * ROPE_FREQS)) < ROPE_FREQS
    partner = jnp.where(first_half, pltpu.roll(t, w - ROPE_FREQS, 1), pltpu.roll(t, ROPE_FREQS, 1))
    return t * cos + partner * sin_signed


def _with_ones_lane(v):
    lane = lax.broadcasted_iota(jnp.int32, (v.shape[0], LANES - HEAD_DIM), 1)
    pad = jnp.where(lane == 0, 1.0, 0.0).astype(v.dtype)
    parts = []
    for h in range(v.shape[1] // HEAD_DIM):
        parts += [v[:, h * HEAD_DIM:(h + 1) * HEAD_DIM], pad]
    return jnp.concatenate(parts, axis=-1)


def _inproj_body(x_ref, nw_ref, sh_ref, sc_ref, w_ref, cos_ref, sin_ref, qn_ref, kn_ref, gn_ref,
                 ws_ref, bs_ref, gsum_ref,
                 qa_ref, ka_ref, va_ref, qb_ref, kb_ref, vb_ref, qs_ref, ks_ref, vs_ref, yd_ref):
    x = x_ref[0]
    h = _rms(x, nw_ref[...]) * (1.0 + sc_ref[0]) + sh_ref[0]
    p = _mm(h, w_ref[...])
    o = PROJ_OFFS
    cos, sin = cos_ref[...], sin_ref[...]
    dt = qa_ref.dtype
    qa_ref[0] = (p[:, o[0]:o[1]] * Q_SCALE).astype(dt)
    ka_ref[0] = p[:, o[1]:o[2]].astype(dt)
    va_ref[0] = _with_ones_lane(p[:, o[2]:o[3]].astype(dt))
    qb = _rope(_head_rms(p[:, o[3]:o[4]], qn_ref[...], gsum_ref), cos, sin)
    qb_ref[0] = (qb * Q_SCALE).astype(dt)
    kb_ref[0] = _rope(_head_rms(p[:, o[4]:o[5]], kn_ref[...], gsum_ref), cos, sin).astype(dt)
    vb_ref[0] = _with_ones_lane(p[:, o[5]:o[6]].astype(dt))
    qs_ref[0] = (_rope(p[:, o[6]:o[7]], cos, sin) * Q_SCALE).astype(dt)
    ks_ref[0] = _rope(p[:, o[7]:o[8]], cos, sin).astype(dt)
    vs_ref[0] = _with_ones_lane(p[:, o[8]:o[9]].astype(dt))
    u = jax.nn.gelu(p[:, o[9]:o[10]])
    v = _rms(jax.nn.gelu(p[:, o[10]:o[11]]), gn_ref[...]).astype(MXU_DTYPE)
    lane_group = lax.broadcasted_iota(jnp.int32, (CHUNK, MIXER_WIDTH), 1) // (MIXER_WIDTH // SG_GROUPS)
    for c in range(x.shape[0] // CHUNK):
        rows = slice(c * CHUNK, (c + 1) * CHUNK)
        mixed = bs_ref[...]
        for g in range(SG_GROUPS):
            mg = jnp.dot(ws_ref[g], v[rows], preferred_element_type=F32)
            mixed = mixed + jnp.where(lane_group == g, mg, 0.0)
        yd_ref[0, rows, :] = (u[rows] * mixed).astype(dt)


def _inproj(x, nw, shift, scale, lw, cos, sin, tm):
    b, t, d = x.shape
    widths = tuple(w * (LANES // HEAD_DIM if j % 3 == 2 else 1) for j, w in enumerate(PROJ_SIZES[:9]))
    widths += (MIXER_WIDTH,)
    row = lambda bi, i: (0, 0)
    per_b = lambda bi, i: (bi, 0, 0)
    tile = lambda bi, i: (bi, i, 0)
    return pl.pallas_call(
        _inproj_body,
        out_shape=[jax.ShapeDtypeStruct((b, t, w), ACT_DTYPE) for w in widths],
        grid=(b, t // tm),
        in_specs=[pl.BlockSpec((1, tm, d), tile),
                  pl.BlockSpec((1, d), row),
                  pl.BlockSpec((1, 1, d), per_b),
                  pl.BlockSpec((1, 1, d), per_b),
                  pl.BlockSpec(lw["w_in"].shape, row),
                  pl.BlockSpec((tm, LANES), lambda bi, i: (i, 0)),
                  pl.BlockSpec((tm, LANES), lambda bi, i: (i, 0)),
                  pl.BlockSpec((1, 256), row),
                  pl.BlockSpec((1, 128), row),
                  pl.BlockSpec((1, 256), row),
                  pl.BlockSpec((SG_GROUPS, CHUNK, CHUNK), lambda bi, i: (0, 0, 0)),
                  pl.BlockSpec((CHUNK, MIXER_WIDTH), row),
                  pl.BlockSpec((256, 256), row)],
        out_specs=[pl.BlockSpec((1, tm, w), tile) for w in widths],
        compiler_params=_params("parallel", "parallel"),
        name="inproj",
    )(x, nw, shift, scale, lw["w_in"], cos, sin, lw["qn"], lw["kn"], lw["gn"], lw["w_sgu"],
      lw["b_sgu"], lw["gsum"])


def _attend_all(jobs):
    staged = []
    for q, chunks, sink in jobs:
        scores = []
        for k, _, bias in chunks:
            s = lax.dot_general(q, k, NT_DIMS, preferred_element_type=F32)
            scores.append(s if bias is None else s + bias)
        m = jnp.max(functools.reduce(jnp.maximum, scores), axis=-1, keepdims=True)
        staged.append((scores, m if sink is None else jnp.maximum(m, sink)))
    outs = []
    for (q, chunks, sink), (scores, m) in zip(jobs, staged):
        acc = jnp.zeros((q.shape[0], LANES), F32)
        for s, (_, v, _) in zip(scores, chunks):
            acc = acc + jnp.dot(jnp.exp2(s - m).astype(v.dtype), v, preferred_element_type=F32)
        l = acc[:, HEAD_DIM:HEAD_DIM + 1]
        if sink is not None:
            l = l + jnp.exp2(sink - m)
        outs.append(acc[:, :HEAD_DIM] / l)
    return outs


def _head(ref, h, width=HEAD_DIM):
    return ref[0, :, h * width:(h + 1) * width]


def _local_attn_body(*refs, group, has_sink):
    if has_sink:
        sink_ref, refs = refs[0], refs[1:]
    q_ref, k0, k1, k2, v0, v1, v2, kc_ref, vc_ref, bias_ref, o_ref = refs
    jobs = []
    for h in range(q_ref.shape[-1] // HEAD_DIM):
        kv = h // group
        hb = h if bias_ref.shape[1] > 1 else 0
        chunks = [(_head(kr, kv), _head(vr, kv, LANES), bias_ref[0, hb, j])
                  for j, (kr, vr) in enumerate(((k0, v0), (k1, v1), (k2, v2)))]
        chunks.append((_head(kc_ref, kv), _head(vc_ref, kv, LANES), None))
        jobs.append((_head(q_ref, h), chunks, sink_ref[h] if has_sink else None))
    o_ref[0] = jnp.concatenate(_attend_all(jobs), axis=-1).astype(o_ref.dtype)


def _local_attn(q, k, v, kc, vc, bias, sink=None):
    b, n, qw = q.shape
    kw = k.shape[-1]
    nc = kc.shape[1]
    tq = ATT_BLOCK
    nblk = n // tq
    assert nblk >= 4
    group = qw // kw
    hb = bias.shape[1]

    def kmap(j):
        return lambda bi, i: (bi, jnp.clip(i - 1, 0, nblk - 3) + j, 0)

    def bmap(bi, i):
        return (jnp.where(i == 0, 0, jnp.where(i == nblk - 1, 2, 1)), 0, 0, 0, 0)

    in_specs = [pl.BlockSpec((1, tq, qw), lambda bi, i: (bi, i, 0))]
    vw = v.shape[-1]
    assert nc == tq
    in_specs += [pl.BlockSpec((1, tq, kw), kmap(j)) for j in range(3)]
    in_specs += [pl.BlockSpec((1, tq, vw), kmap(j)) for j in range(3)]
    in_specs += [pl.BlockSpec((1, nc, kw), lambda bi, i: (bi, 0, 0)),
                 pl.BlockSpec((1, nc, vw), lambda bi, i: (bi, 0, 0))]
    in_specs += [pl.BlockSpec((1, hb, 3, tq, tq), bmap)]
    args = [q, k, k, k, v, v, v, kc, vc, bias]
    if sink is not None:
        in_specs = [pl.BlockSpec(memory_space=pltpu.SMEM)] + in_specs
        args = [sink] + args
    return pl.pallas_call(
        functools.partial(_local_attn_body, group=group, has_sink=sink is not None),
        out_shape=jax.ShapeDtypeStruct((b, n, qw), ACT_DTYPE),
        grid=(b, nblk),
        in_specs=in_specs,
        out_specs=pl.BlockSpec((1, tq, qw), lambda bi, i: (bi, i, 0)),
        compiler_params=_params("parallel", "parallel"),
        name="local_attn",
    )(*args)


def _ctx_attn_body(sink_ref, qa, ka, va, qb, kb, vb, qs, ks, vs, oa, ob, oc):
    for q_ref, k_ref, v_ref, o_ref, group, use_sink in (
            (qa, ka, va, oa, 1, False), (qb, kb, vb, ob, 2, False), (qs, ks, vs, oc, 2, True)):
        jobs = [(_head(q_ref, h), [(_head(k_ref, h // group), _head(v_ref, h // group, LANES), None)],
                 sink_ref[h] if use_sink else None) for h in range(q_ref.shape[-1] // HEAD_DIM)]
        o_ref[0] = jnp.concatenate(_attend_all(jobs), axis=-1).astype(o_ref.dtype)


def _ctx_attn(sink, qkv):
    b, nc, _ = qkv[0].shape
    spec = lambda a: pl.BlockSpec((1, nc, a.shape[-1]), lambda bi: (bi, 0, 0))
    return pl.pallas_call(
        _ctx_attn_body,
        out_shape=[jax.ShapeDtypeStruct((b, nc, MIXER_WIDTH), ACT_DTYPE)] * 3,
        grid=(b,),
        in_specs=[pl.BlockSpec(memory_space=pltpu.SMEM)] + [spec(a) for a in qkv],
        out_specs=[pl.BlockSpec((1, nc, MIXER_WIDTH), lambda bi: (bi, 0, 0))] * 3,
        compiler_params=_params("parallel"),
        name="ctx_attn",
    )(sink, *qkv)


GLOBAL_UNROLL = 6


def _global_attn_body(q_ref, kt_ref, v_ref, o_ref, s_a, s_b, *, tk):
    tq = q_ref.shape[1]
    n_kv = kt_ref.shape[1] // HEAD_DIM
    n_chunks = kt_ref.shape[2] // tk
    group_w = 2 * HEAD_DIM
    qs = [jnp.concatenate([q_ref[0, :, kv * group_w:kv * group_w + HEAD_DIM],
                           q_ref[0, :, kv * group_w + HEAD_DIM:(kv + 1) * group_w]], axis=0)
          for kv in range(n_kv)]

    def scores(i, s_ref):
        ks = pl.multiple_of(i * tk, tk)
        for kv in range(n_kv):
            s_ref[kv] = jnp.dot(qs[kv], kt_ref[0, kv * HEAD_DIM:(kv + 1) * HEAD_DIM, pl.ds(ks, tk)],
                                preferred_element_type=F32)

    def update(i, s_ref, carry):
        ks = pl.multiple_of(i * tk, tk)
        out = []
        for kv in range(n_kv):
            m, acc = carry[kv]
            s = s_ref[kv]
            m_new = jnp.maximum(m, jnp.max(s, axis=-1, keepdims=True))
            p = jnp.exp2(s - m_new).astype(v_ref.dtype)
            pv = jnp.dot(p, v_ref[0, pl.ds(ks, tk), kv * LANES:(kv + 1) * LANES], preferred_element_type=F32)
            out.append((m_new, jnp.exp2(m - m_new) * acc + pv))
        return tuple(out)

    carry = tuple((jnp.full((2 * tq, 1), NEG_INF, F32), jnp.zeros((2 * tq, LANES), F32)) for _ in range(n_kv))
    scores(0, s_a)

    def pair(j, carry):
        scores(2 * j + 1, s_b)
        carry = update(2 * j, s_a, carry)
        scores(2 * j + 2, s_a)
        return update(2 * j + 1, s_b, carry)

    n_pairs = (n_chunks - 1) // 2
    unroll = GLOBAL_UNROLL if n_pairs % GLOBAL_UNROLL == 0 else 1

    def pairs(j, carry):
        for u in range(unroll):
            carry = pair(j * unroll + u, carry)
        return carry

    carry = lax.fori_loop(0, n_pairs // unroll, pairs, carry)
    if n_chunks % 2 == 0:
        scores(n_chunks - 1, s_b)
        carry = update(n_chunks - 2, s_a, carry)
        carry = update(n_chunks - 1, s_b, carry)
    else:
        carry = update(n_chunks - 1, s_a, carry)
    outs = []
    for _, acc in carry:
        o = acc[:, :HEAD_DIM] / acc[:, HEAD_DIM:HEAD_DIM + 1]
        outs += [o[:tq], o[tq:]]
    o_ref[0] = jnp.concatenate(outs, axis=-1).astype(o_ref.dtype)


def _global_attn(q, k, v):
    b, n, qw = q.shape
    nk, kw = k.shape[1:]
    n_kv = kw // HEAD_DIM
    tq = ATT_BLOCK
    tk = 1280 if (nk % 1280 == 0 and nk > 1280) else 256
    kt = jnp.swapaxes(k, 1, 2)
    return pl.pallas_call(
        functools.partial(_global_attn_body, tk=tk),
        out_shape=jax.ShapeDtypeStruct((b, n, qw), ACT_DTYPE),
        grid=(b, n // tq),
        in_specs=[pl.BlockSpec((1, tq, qw), lambda bi, i: (bi, i, 0)),
                  pl.BlockSpec((1, kw, nk), lambda bi, i: (bi, 0, 0)),
                  pl.BlockSpec((1, nk, n_kv * LANES), lambda bi, i: (bi, 0, 0))],
        out_specs=pl.BlockSpec((1, tq, qw), lambda bi, i: (bi, i, 0)),
        scratch_shapes=[pltpu.VMEM((n_kv, 2 * tq, tk), F32), pltpu.VMEM((n_kv, 2 * tq, tk), F32)],
        compiler_params=_params("parallel", "parallel"),
        name="global_attn",
    )(q, kt, v)


def _merge_body(ya, yb, yc, yd, x_ref, on_ref, wo_ref, g1_ref, nf_ref, sh_ref, sc_ref, wr_ref,
                xo_ref, h_ref, aff_ref):
    parts = []
    for j, r in enumerate((ya, yb, yc, yd)):
        y = r[0].astype(F32)
        parts.append(_rms(y, on_ref[:, j * MIXER_WIDTH:(j + 1) * MIXER_WIDTH]).astype(MXU_DTYPE))
    xn = x_ref[0] + g1_ref[0] * _mm(jnp.concatenate(parts, axis=-1), wo_ref[...])
    xo_ref[0] = xn
    h = _rms(xn, nf_ref[...]) * (1.0 + sc_ref[0]) + sh_ref[0]
    _write_row_tiles(h_ref, (0,), h)
    logits = lax.dot_general(wr_ref[...], h, NT_DIMS, precision=HI, preferred_element_type=F32)
    e = jnp.exp(logits - jnp.max(logits, axis=0, keepdims=True))
    aff_ref[0] = e / jnp.sum(e, axis=0, keepdims=True)


def _merge(ys, x, lw, g1, shift, scale, tm):
    b, t, d = x.shape
    ne = lw["w_router_t"].shape[0]
    row = lambda bi, i: (0, 0)
    per_b = lambda bi, i: (bi, 0, 0)
    tile = lambda bi, i: (bi, i, 0)
    return pl.pallas_call(
        _merge_body,
        out_shape=[jax.ShapeDtypeStruct((b, t, d), F32), jax.ShapeDtypeStruct((b, t * ROW_TILE, LANES), F32),
                   jax.ShapeDtypeStruct((b, ne, t), F32)],
        grid=(b, t // tm),
        in_specs=[pl.BlockSpec((1, tm, MIXER_WIDTH), tile)] * 4 + [
            pl.BlockSpec((1, tm, d), tile),
            pl.BlockSpec((1, d), row),
            pl.BlockSpec((d, d), row),
            pl.BlockSpec((1, 1, d), per_b),
            pl.BlockSpec((1, d), row),
            pl.BlockSpec((1, 1, d), per_b),
            pl.BlockSpec((1, 1, d), per_b),
            pl.BlockSpec((ne, d), row)],
        out_specs=[pl.BlockSpec((1, tm, d), tile), pl.BlockSpec((1, tm * ROW_TILE, LANES), tile),
                   pl.BlockSpec((1, ne, tm), lambda bi, i: (bi, 0, i))],
        compiler_params=_params("parallel", "parallel"),
        name="merge",
    )(*ys, x, lw["out_norm"], lw["w_out"], g1, lw["norm_ffn"], shift, scale, lw["w_router_t"])


def _tri(n, m, mode):
    r = lax.broadcasted_iota(jnp.int32, (n, m), 0)
    c = lax.broadcasted_iota(jnp.int32, (n, m), 1)
    return jnp.where({"lt": r < c, "le": r <= c, "gt": r > c}[mode], 1.0, 0.0).astype(MXU_DTYPE)


def _count(mask, axes):
    out = jnp.where(mask, 1.0, 0.0)
    for ax in sorted(axes, reverse=True):
        out = jnp.sum(out, axis=ax, keepdims=True)
    return out


def _kth_largest_bits(bits, cap, axes):
    shape = tuple(1 if a in axes else s for a, s in enumerate(bits.shape))

    def body(i, t):
        cand = t | lax.shift_left(jnp.int32(1), 30 - i)
        return jnp.where(_count(bits >= cand, axes) >= cap, cand, t)

    return lax.fori_loop(0, 31, body, jnp.zeros(shape, jnp.int32))


def _prefix_tokens(m, exact_rows):
    e, r, l = m.shape
    m2 = m.reshape(e * r, l).astype(MXU_DTYPE)
    local = jnp.dot(m2, _tri(l, l, "lt"), preferred_element_type=F32)
    rowtot = jnp.dot(m2, jnp.ones((l, l), MXU_DTYPE), preferred_element_type=F32).reshape(e, r, l)
    below = _tri(r, r, "gt")
    if exact_rows:
        base = [jnp.dot(below, rowtot[i].astype(MXU_DTYPE), preferred_element_type=F32) for i in range(e)]
    else:
        base = [jnp.dot(below.astype(F32), rowtot[i], precision=HI, preferred_element_type=F32)
                for i in range(e)]
    return local.reshape(e, r, l) + jnp.stack(base, axis=0)


def _select_mask(aff, cap, prefix_fn, axes):
    bits = pltpu.bitcast(aff, jnp.int32)
    thr = _kth_largest_bits(bits, cap, axes)
    gt = bits > thr
    eq = bits == thr
    need = cap - _count(gt, axes)
    eq_rank = prefix_fn(jnp.where(eq, 1.0, 0.0))
    take_eq = jnp.where(eq, jnp.where(eq_rank < need, 1.0, 0.0), 0.0)
    return jnp.where(gt, 1.0, take_eq)


def _select_body(aff_ref, sel_ref, prank_ref, tstart_ref, tend_ref, *, cap):
    aff = aff_ref[0]
    ne = aff.shape[0]
    sel = _select_mask(aff, cap, functools.partial(_prefix_tokens, exact_rows=True), (1, 2))
    sel_ref[0] = sel
    cnt = jnp.sum(sel, axis=0)
    tstart = _prefix_tokens(cnt[None], exact_rows=False)[0]
    tstart_ref[0] = tstart.astype(jnp.int32)
    tend_ref[0] = (tstart + cnt).astype(jnp.int32)
    run = tstart
    for e in range(ne):
        prank_ref[0, e] = run.astype(jnp.int32)
        run = run + sel[e]


def _select(aff4, cap):
    b, ne, r, l = aff4.shape
    blk4 = pl.BlockSpec((1, ne, r, l), lambda bi: (bi, 0, 0, 0))
    blk3 = pl.BlockSpec((1, r, l), lambda bi: (bi, 0, 0))
    return pl.pallas_call(
        functools.partial(_select_body, cap=cap),
        out_shape=[jax.ShapeDtypeStruct((b, ne, r, l), F32), jax.ShapeDtypeStruct((b, ne, r, l), jnp.int32),
                   jax.ShapeDtypeStruct((b, r, l), jnp.int32), jax.ShapeDtypeStruct((b, r, l), jnp.int32)],
        grid=(b,),
        in_specs=[blk4],
        out_specs=[blk4, blk4, blk3, blk3],
        compiler_params=_params("parallel"),
        name="moe_select",
    )(aff4)


def _slots_body(sel_ref, aff_ref, prank_ref, idx_ref, dest_ref, gate_ref):
    m = sel_ref[0, 0]
    r, l = m.shape
    cap = idx_ref.shape[2]
    mb = m.astype(MXU_DTYPE)
    linc = jnp.dot(mb, _tri(l, l, "le"), preferred_element_type=F32)
    rowtot = jnp.dot(mb, jnp.ones((l, l), MXU_DTYPE), preferred_element_type=F32)
    rowtot_lane = lax.dot_general(jnp.ones((8, l), MXU_DTYPE), mb, NT_DIMS, preferred_element_type=F32)
    cumrow = jnp.dot(rowtot_lane.astype(MXU_DTYPE), _tri(r, r, "le"), preferred_element_type=F32)[0:1]
    slot = lax.broadcasted_iota(jnp.int32, (cap, r), 0).astype(F32)
    passed = jnp.where(cumrow <= slot, 1.0, 0.0).astype(MXU_DTYPE)
    row_of = jnp.dot(passed, jnp.ones((r, l), MXU_DTYPE), preferred_element_type=F32)[:, 0:1]
    base_of = jnp.dot(passed, rowtot.astype(MXU_DTYPE), preferred_element_type=F32)[:, 0:1]
    onehot = jnp.where(lax.broadcasted_iota(jnp.int32, (cap, r), 1).astype(F32) == row_of, 1.0, 0.0)
    linc_of = jnp.dot(onehot.astype(MXU_DTYPE), linc.astype(MXU_DTYPE), preferred_element_type=F32)
    k = slot[:, 0:1] - base_of
    col_of = jnp.sum(jnp.where(linc_of <= k, 1.0, 0.0), axis=-1, keepdims=True)
    idx_ref[0, 0] = (row_of * l + col_of).astype(jnp.int32)
    at_col = lax.broadcasted_iota(jnp.int32, (cap, l), 1).astype(F32) == col_of
    hot = onehot.astype(MXU_DTYPE)
    aff_rows, rest = jnp.zeros((cap, l), F32), aff_ref[0, 0]
    for _ in range(3):
        piece = rest.astype(MXU_DTYPE)
        aff_rows = aff_rows + jnp.dot(hot, piece, preferred_element_type=F32)
        rest = rest - piece.astype(F32)
    gate_ref[0, 0] = jnp.sum(jnp.where(at_col, aff_rows, 0.0), axis=-1, keepdims=True)
    prank = prank_ref[0, 0]
    prank_rows = (jnp.dot(hot, (prank >> 8).astype(F32).astype(MXU_DTYPE), preferred_element_type=F32) * 256.0
                  + jnp.dot(hot, (prank & 255).astype(F32).astype(MXU_DTYPE), preferred_element_type=F32))
    dest_ref[0, 0] = jnp.sum(jnp.where(at_col, prank_rows, 0.0), axis=-1, keepdims=True).astype(jnp.int32)


def _slots(sel, aff4, prank, cap):
    b, ne, r, l = sel.shape
    blk = pl.BlockSpec((1, 1, r, l), lambda bi, e: (bi, e, 0, 0))
    oblk = pl.BlockSpec((1, 1, cap, 1), lambda bi, e: (bi, e, 0, 0))
    return pl.pallas_call(
        _slots_body,
        out_shape=[jax.ShapeDtypeStruct((b, ne, cap, 1), jnp.int32), jax.ShapeDtypeStruct((b, ne, cap, 1), jnp.int32),
                   jax.ShapeDtypeStruct((b, ne, cap, 1), F32)],
        grid=(b, ne),
        in_specs=[blk, blk, blk],
        out_specs=[oblk, oblk, oblk],
        compiler_params=_params("parallel", "parallel"),
        name="moe_slots",
    )(sel, aff4, prank)


FFN_CHUNK = 256
DMA_THREADS = 2


def _ffn_body(idx_ref, dest_ref, h_hbm, gate_ref, wg_ref, wu_ref, wd_ref, z_hbm, xa, xb, ya, yb, sems,
              *, n_tok, n_pair):
    n_b, tiles = pl.num_programs(1), pl.num_programs(2)
    step = (pl.program_id(0) * n_b + pl.program_id(1)) * tiles + pl.program_id(2)
    last = pl.num_programs(0) * n_b * tiles - 1
    ts = xa.shape[0] // ROW_TILE
    n_groups = wg_ref.shape[2] // FFN_CHUNK
    per_group = ts // n_groups

    def sample_of(k):
        return (k // tiles) % n_b

    def tile_of(ref, r):
        return ref.at[pl.ds(pl.multiple_of(r * ROW_TILE, ROW_TILE), ROW_TILE), :]

    def gather(k, half, buf, sem):
        base, rows = (2 * k + half) * ts, sample_of(k) * n_tok
        return lambda s: pltpu.make_async_copy(
            tile_of(h_hbm, rows + idx_ref[base + s]), tile_of(buf, s), sems.at[sem])

    def scatter(k, half, buf, sem):
        base, rows = (2 * k + half) * ts, sample_of(k) * n_pair
        return lambda s: pltpu.make_async_copy(
            tile_of(buf, s), tile_of(z_hbm, rows + dest_ref[base + s]), sems.at[sem])

    def start_all(copy):
        def body(g, c):
            for u in range(8):
                copy(g * 8 + u).start(priority=u % DMA_THREADS)
            return c
        lax.fori_loop(0, ts // 8, body, 0)

    def wait_rows(buf, sem):
        pltpu.make_async_copy(buf, buf, sems.at[sem]).wait()

    def ffn(xbuf, gate, copies):
        x = _read_row_tiles(xbuf, (), ts).astype(MXU_DTYPE)
        for copy in copies:
            for s in range(ts):
                copy(s).start(priority=s % DMA_THREADS)
        y = None
        for j in range(n_groups):
            cols = slice(j * FFN_CHUNK, (j + 1) * FFN_CHUNK)
            hid = _silu(_mm(x, wg_ref[0, :, cols])) * _mm(x, wu_ref[0, :, cols])
            part = _mm(hid, wd_ref[0, cols, :])
            y = part if y is None else y + part
        return y * gate

    nxt = jnp.minimum(step + 1, last)
    gate_a, gate_b = gate_ref[0, 0, 0:ts], gate_ref[0, 0, ts:2 * ts]
    prefetch = [gather(step, 1, xb, 1), gather(nxt, 0, xa, 0)]

    @pl.when(step == 0)
    def _():
        start_all(gather(step, 0, xa, 0))
        wait_rows(xa, 0)
        _write_row_tiles(ya, (), ffn(xa, gate_a, prefetch))

    @pl.when(step > 0)
    def _():
        wait_rows(xa, 0)
        y_a = ffn(xa, gate_a, [scatter(step - 1, 0, ya, 2), scatter(step - 1, 1, yb, 3)] + prefetch)
        wait_rows(ya, 2)
        _write_row_tiles(ya, (), y_a)

    wait_rows(xb, 1)
    y_b = ffn(xb, gate_b, [])

    @pl.when(step > 0)
    def _():
        wait_rows(yb, 3)

    _write_row_tiles(yb, (), y_b)

    @pl.when(step == last)
    def _():
        start_all(scatter(step, 0, ya, 2))
        start_all(scatter(step, 1, yb, 3))
        wait_rows(xa, 0)
        wait_rows(ya, 2)
        wait_rows(yb, 3)


def _expert_ffn(idx, dest, gate, h, lw, ts):
    b, ne, cap, _ = gate.shape
    n, d = h.shape[1] // ROW_TILE, ROW_TILE * LANES
    f = lw["w_gate"].shape[-1]
    layer = lw["layer"]
    n_pair = ne * cap
    by_expert = lambda a: jnp.swapaxes(a, 0, 1)
    grid_spec = pltpu.PrefetchScalarGridSpec(
        num_scalar_prefetch=2,
        grid=(ne, b, cap // (2 * ts)),
        in_specs=[pl.BlockSpec(memory_space=pl.ANY),
                  pl.BlockSpec((1, 1, 2 * ts, 1), lambda e, bi, t, *_: (e, bi, t, 0)),
                  pl.BlockSpec((None, 1, d, f), lambda e, bi, t, *_: (layer, e, 0, 0)),
                  pl.BlockSpec((None, 1, d, f), lambda e, bi, t, *_: (layer, e, 0, 0)),
                  pl.BlockSpec((None, 1, f, d), lambda e, bi, t, *_: (layer, e, 0, 0))],
        out_specs=pl.BlockSpec(memory_space=pl.ANY),
        scratch_shapes=[pltpu.VMEM((ts * ROW_TILE, LANES), F32)] * 4 + [pltpu.SemaphoreType.DMA((4,))])
    return pl.pallas_call(
        functools.partial(_ffn_body, n_tok=n, n_pair=n_pair),
        out_shape=jax.ShapeDtypeStruct((b * n_pair * ROW_TILE, LANES), F32),
        grid_spec=grid_spec,
        compiler_params=_params("arbitrary", "arbitrary", "arbitrary"),
        name="moe_ffn",
    )(by_expert(idx).reshape(-1), by_expert(dest).reshape(-1), h.reshape(b * n * ROW_TILE, LANES), by_expert(gate),
      lw["w_gate"], lw["w_up"], lw["w_down"])


COMBINE_DEPTH = 4


def _combine_body(tb_ref, x_ref, g2_ref, ts_ref, te_ref, fg_ref, z_hbm, o_ref, zbuf, acc_ref, sem,
                  *, n_pair, final):
    n_tiles = pl.num_programs(1)
    tile = pl.program_id(0) * n_tiles + pl.program_id(1)
    depth, pc = zbuf.shape[0], zbuf.shape[1] // ROW_TILE
    tt = x_ref.shape[1]

    def plan(g):
        b, i = g // n_tiles, g % n_tiles
        lo, hi = tb_ref[b * (n_tiles + 1) + i], tb_ref[b * (n_tiles + 1) + i + 1]
        p0 = (lo // 8) * 8
        return b, p0, (hi - p0 + pc - 1) // pc

    def fetch(b, p0, k, slot):
        cs = pl.multiple_of(jnp.minimum(p0 + k * pc, n_pair - pc), 8)
        rows = pl.ds(pl.multiple_of((b * n_pair + cs) * ROW_TILE, 8 * ROW_TILE), pc * ROW_TILE)
        return cs, pltpu.make_async_copy(z_hbm.at[rows, :], zbuf.at[slot], sem.at[slot])

    b, p0, n_chunks = plan(tile)
    start_row, end_row = ts_ref[0, 0], te_ref[0, 0]
    acc_ref[...] = jnp.zeros_like(acc_ref)

    def start_head(b, p0, n_chunks):
        for k in range(depth - 1):
            @pl.when(k < n_chunks)
            def _():
                fetch(b, p0, k, k)[1].start()

    @pl.when(tile == 0)
    def _():
        start_head(b, p0, n_chunks)

    def chunk(k, c):
        slot = k % depth
        ahead = k + depth - 1

        @pl.when(ahead < n_chunks)
        def _():
            fetch(b, p0, ahead, ahead % depth)[1].start()

        cs, cp = fetch(b, p0, k, slot)
        cp.wait()
        pair = cs + lax.broadcasted_iota(jnp.int32, (pc, tt), 0)
        own = jnp.where(pair >= jnp.maximum(start_row, p0 + k * pc), jnp.where(pair < end_row, 1.0, 0.0), 0.0)
        own = own.astype(MXU_DTYPE)
        z = _read_row_tiles(zbuf, (slot,), pc).astype(MXU_DTYPE)
        acc_ref[...] += lax.dot_general(own, z, TN_DIMS, preferred_element_type=F32)
        return c

    lax.fori_loop(0, n_chunks, chunk, 0)

    @pl.when(tile + 1 < pl.num_programs(0) * n_tiles)
    def _():
        start_head(*plan(tile + 1))

    out = x_ref[0] + g2_ref[0] * acc_ref[...]
    o_ref[0] = _rms(out, fg_ref[...]) if final else out


def _combine(x, g2, tstart, tend, z, n_pair, tt, pc, final_g, final):
    b, n, d = x.shape
    nt = n // tt
    ts4 = tstart.reshape(b, nt, 1, tt)
    te4 = tend.reshape(b, nt, 1, tt)
    bounds = jnp.concatenate([ts4[:, :, 0, 0], jnp.full((b, 1), n_pair, jnp.int32)], axis=1).reshape(-1)
    grid_spec = pltpu.PrefetchScalarGridSpec(
        num_scalar_prefetch=1,
        grid=(b, nt),
        in_specs=[pl.BlockSpec((1, tt, d), lambda bi, i, *_: (bi, i, 0)),
                  pl.BlockSpec((1, 1, d), lambda bi, i, *_: (bi, 0, 0)),
                  pl.BlockSpec((1, 1, 1, tt), lambda bi, i, *_: (bi, i, 0, 0)),
                  pl.BlockSpec((1, 1, 1, tt), lambda bi, i, *_: (bi, i, 0, 0)),
                  pl.BlockSpec((1, d), lambda bi, i, *_: (0, 0)),
                  pl.BlockSpec(memory_space=pl.ANY)],
        out_specs=pl.BlockSpec((1, tt, d), lambda bi, i, *_: (bi, i, 0)),
        scratch_shapes=[pltpu.VMEM((COMBINE_DEPTH, pc * ROW_TILE, LANES), F32), pltpu.VMEM((tt, d), F32),
                        pltpu.SemaphoreType.DMA((COMBINE_DEPTH,))])
    return pl.pallas_call(
        functools.partial(_combine_body, n_pair=n_pair, final=final),
        out_shape=jax.ShapeDtypeStruct((b, n, d), F32),
        grid_spec=grid_spec,
        compiler_params=_params("arbitrary", "arbitrary"),
        name="moe_combine",
    )(bounds, x, g2, ts4, te4, final_g, z)


def _moe_latent(x, h, aff, g2, lw, final_g, final):
    b, n, d = x.shape
    ne = aff.shape[1]
    cap = EC_CAPACITY_FACTOR * n // ne
    aff4 = aff.reshape(b, ne, n // LANES, LANES)
    sel, prank, tstart, tend = _select(aff4, cap)
    idx, dest, gate = _slots(sel, aff4, prank, cap)
    z = _expert_ffn(idx, dest, gate, h, lw, min(cap // 2, 256))
    assert ne * cap <= 1 << 16
    return _combine(x, g2, tstart, tend, z, ne * cap, 256, 256, final_g, final)


def _ctx_coef_body(aff_ref, coef_ref, slot_ref, *, cap):
    nb, _, nc = aff_ref.shape
    excl = _tri(nc, nc, "lt")
    prefix = lambda mm: jnp.dot(mm.astype(MXU_DTYPE), excl, preferred_element_type=F32)
    coefs, slots = [], []
    for b in range(nb):
        aff = aff_ref[b]
        sel = _select_mask(aff, cap, prefix, (1,))
        coefs.append(sel * aff)
        slots.append(jnp.where(sel > 0.0, prefix(sel) + float(b * cap), -1.0))
    coef_ref[:, 0, :] = jnp.concatenate(coefs, axis=-1)
    slot_ref[:, 0, :] = jnp.concatenate(slots, axis=-1)


def _ctx_coef(aff, cap):
    b, ne, nc = aff.shape
    return pl.pallas_call(
        functools.partial(_ctx_coef_body, cap=cap),
        out_shape=[jax.ShapeDtypeStruct((ne, 1, b * nc), F32)] * 2,
        name="ctx_moe_select",
    )(aff)


def _ctx_ffn_body(h_ref, coef_ref, slot_ref, wg_ref, wu_ref, wd_ref, x_ref, g2_ref, o_ref, *, n_slots):
    e = pl.program_id(0)
    rows = h_ref.shape[0] // ROW_TILE
    h = _read_row_tiles(h_ref, (), rows)
    picked = jnp.where(lax.broadcasted_iota(jnp.int32, (n_slots, rows), 0).astype(F32) == slot_ref[0], 1.0, 0.0)
    gate = jnp.sum(picked * coef_ref[0], axis=-1, keepdims=True)
    x = _mm(picked, h)
    y = _mm(_silu(_mm(x, wg_ref[0])) * _mm(x, wu_ref[0]), wd_ref[0]) * gate

    @pl.when(e == 0)
    def _():
        o_ref[...] = x_ref[...]

    o_ref[...] += g2_ref[...] * lax.dot_general(picked.astype(MXU_DTYPE), y.astype(MXU_DTYPE), TN_DIMS,
                                                 preferred_element_type=F32)


def _ctx_moe(xc, hc, aff, g2, lw):
    b, nc, d = xc.shape
    ne = aff.shape[1]
    f = lw["w_gate"].shape[-1]
    layer = lw["layer"]
    cap = EC_CAPACITY_FACTOR * nc // ne
    coef, slot = _ctx_coef(aff, cap)
    rows = b * nc
    per_e = pl.BlockSpec((1, 1, rows), lambda e: (e, 0, 0))
    out = pl.pallas_call(
        functools.partial(_ctx_ffn_body, n_slots=b * cap),
        out_shape=jax.ShapeDtypeStruct((rows, d), F32),
        grid=(ne,),
        in_specs=[pl.BlockSpec((rows * ROW_TILE, LANES), lambda e: (0, 0)),
                  per_e, per_e,
                  pl.BlockSpec((None, 1, d, f), lambda e: (layer, e, 0, 0)),
                  pl.BlockSpec((None, 1, d, f), lambda e: (layer, e, 0, 0)),
                  pl.BlockSpec((None, 1, f, d), lambda e: (layer, e, 0, 0)),
                  pl.BlockSpec((rows, d), lambda e: (0, 0)),
                  pl.BlockSpec((1, d), lambda e: (0, 0))],
        out_specs=pl.BlockSpec((rows, d), lambda e: (0, 0)),
        compiler_params=_params("arbitrary"),
        name="ctx_moe_ffn",
    )(hc.reshape(rows * ROW_TILE, LANES), coef, slot, lw["w_gate"], lw["w_up"], lw["w_down"],
      xc.reshape(rows, d), g2)
    return out.reshape(b, nc, d)


def _rope_tables(n):
    t = np.arange(n)
    freqs = ROPE_THETA ** (-np.arange(ROPE_FREQS, dtype=np.float32) / ROPE_FREQS)
    ang_r = (t // GRID_W).astype(np.float32)[:, None] * freqs
    ang_c = (t % GRID_W).astype(np.float32)[:, None] * freqs
    cos = np.concatenate([np.cos(ang_r)] * 2 + [np.cos(ang_c)] * 2, axis=1)
    sin = np.concatenate([-np.sin(ang_r), np.sin(ang_r), -np.sin(ang_c), np.sin(ang_c)], axis=1)
    reps = LANES // HEAD_DIM
    return (jnp.asarray(np.tile(cos, (1, reps)), F32), jnp.asarray(np.tile(sin, (1, reps)), F32))


def _block_geometry(n):
    nblk = n // ATT_BLOCK
    a = np.arange(ATT_BLOCK)
    out = []
    for i in (0, 1, nblk - 1):
        j0 = int(np.clip(i - 1, 0, nblk - 3))
        out.append((i * ATT_BLOCK + a, [(j0 + j) * ATT_BLOCK + a for j in range(3)]))
    return out


def _neighbourhood_bias(rpb, n):
    rows = n // GRID_W
    win_r = min(NA_WIN_R, rows)
    blk_rows = ATT_BLOCK // GRID_W
    cols = np.arange(GRID_W)
    col_off = cols[None, :] - cols[:, None] + NA_WIN_C - 1
    col_hot = jnp.asarray(col_off[:, :, None] == np.arange(2 * NA_WIN_C - 1), F32)
    c0 = np.clip(cols - NA_WIN_C // 2, 0, GRID_W - NA_WIN_C)
    col_ok = (cols[None, :] >= c0[:, None]) & (cols[None, :] < c0[:, None] + NA_WIN_C)
    kinds = []
    for q_tok, k_chunks in _block_geometry(n):
        qr = q_tok[::GRID_W] // GRID_W
        r0 = np.clip(qr - win_r // 2, 0, rows - win_r)
        chunks = []
        for k_tok in k_chunks:
            kr = k_tok[::GRID_W] // GRID_W
            row_off = kr[None, :] - qr[:, None] + NA_WIN_R - 1
            row_hot = jnp.asarray(row_off[:, :, None] == np.arange(2 * NA_WIN_R - 1), F32)
            row_ok = (kr[None, :] >= r0[:, None]) & (kr[None, :] < r0[:, None] + win_r)
            ok = (row_ok[:, None, :, None] & col_ok[None, :, None, :]).reshape(ATT_BLOCK, ATT_BLOCK)
            vals = jnp.einsum("qkr,hrc,xyc->hqxky", row_hot, rpb, col_hot, precision=HI)
            vals = vals.reshape(-1, blk_rows * GRID_W, blk_rows * GRID_W) * LOG2E
            chunks.append(jnp.where(ok[None], vals, NEG_INF))
        kinds.append(jnp.stack(chunks, axis=1))
    return jnp.stack(kinds, axis=0)


def _window_mask(n):
    kinds = []
    for q_tok, k_chunks in _block_geometry(n):
        kinds.append(np.stack([np.where(np.abs(k_tok[None] - q_tok[:, None]) <= WINDOW, 0.0, NEG_INF)
                               for k_tok in k_chunks])[None])
    return jnp.asarray(np.stack(kinds), F32)


def kernel(x, c, ctx, c_ctx, w_mod, b_mod, norm_mix, norm_ffn, w_in, rpb, q_norm, k_norm, sink, sgu_norm, w_sgu,
           b_sgu, out_norm, w_out, w_router, w_gate, w_up, w_down, final_norm):
    depth = w_mod.shape[0]
    b, n, d = x.shape
    nc = ctx.shape[1]
    group_w = MIXER_WIDTH // SG_GROUPS

    cvecs = jnp.concatenate([c, c_ctx[None], jnp.zeros((8 - b - 1, d), F32)], axis=0)
    mods = _adaln_all(cvecs, w_mod, b_mod).reshape(depth, 8, 6, d)

    cos, sin = _rope_tables(n)
    cos_c, sin_c = jnp.ones((nc, LANES), F32), jnp.zeros((nc, LANES), F32)
    win_mask = _window_mask(n)
    blk = np.arange(256) // HEAD_DIM
    gsum = jnp.asarray(blk[:, None] == blk[None, :], MXU_DTYPE)
    wg_all, wu_all, wd_all = (w.astype(MXU_DTYPE) for w in (w_gate, w_up, w_down))

    xc = ctx
    for l in range(depth):
        ctx_needed = l < depth - 1
        lat = [mods[l, :b, j][:, None, :] for j in range(6)]
        cm = [jnp.broadcast_to(mods[l, b, j][None, None, :], (b, 1, d)) for j in range(6)]
        lw = {
            "w_in": w_in[l].astype(MXU_DTYPE),
            "qn": jnp.tile(q_norm[l], 4)[None], "kn": jnp.tile(k_norm[l], 2)[None], "gn": sgu_norm[l][None],
            "w_sgu": w_sgu[l].astype(MXU_DTYPE),
            "b_sgu": jnp.repeat(b_sgu[l].T, group_w, axis=1),
            "gsum": gsum,
            "out_norm": out_norm[l][None], "w_out": w_out[l].astype(MXU_DTYPE), "norm_ffn": norm_ffn[l][None],
            "w_router_t": w_router[l].T,
            "layer": l, "w_gate": wg_all, "w_up": wu_all, "w_down": wd_all,
        }
        nm = norm_mix[l][None]
        hp = _inproj(x, nm, lat[0], lat[1], lw, cos, sin, 512)
        cp = _inproj(xc, nm, cm[0], cm[1], lw, cos_c, sin_c, nc)
        qa, ka, va, qb, kb, vb, qs, ks, vs, yd = hp
        _, ka_c, va_c, _, kb_c, vb_c, _, ks_c, vs_c, _ = cp
        sink_l = sink[l] * LOG2E

        ya = _local_attn(qa, ka, va, ka_c, va_c, _neighbourhood_bias(rpb[l], n))
        yb = _global_attn(qb, jnp.concatenate([kb, kb_c], axis=1), jnp.concatenate([vb, vb_c], axis=1))
        yc = _local_attn(qs, ks, vs, ks_c, vs_c, win_mask, sink_l)
        x_mid, h2, aff = _merge((ya, yb, yc, yd), x, lw, lat[2], lat[3], lat[4], 512)
        x = _moe_latent(x_mid, h2, aff, lat[5], lw, final_norm[None], l == depth - 1)

        if ctx_needed:
            ys_c = _ctx_attn(sink_l, cp[:9])
            xc_mid, hc2, aff_c = _merge((*ys_c, cp[9]), xc, lw, cm[2], cm[3], cm[4], nc)
            xc = _ctx_moe(xc_mid, hc2, aff_c, mods[l, b, 5][None], lw)
    return x
```

```python
import functools

import numpy as np
import jax
import jax.numpy as jnp
from jax import lax
from jax.experimental import pallas as pl
from jax.experimental.pallas import tpu as pltpu

HEAD_DIM = 64
GRID_W = 64
MIXER_WIDTH = 256
NA_WIN_R = 8
NA_WIN_C = 16
WINDOW = 128
CHUNK = 128
SG_GROUPS = 4
EC_CAPACITY_FACTOR = 2
ROPE_THETA = 10000.0
ROPE_FREQS = HEAD_DIM // 4
EPS = 1e-6
NEG_INF = -1e30
PROJ_SIZES = (256, 256, 256, 256, 128, 128, 256, 128, 128, 256, 256)
PROJ_OFFS = tuple(int(v) for v in np.cumsum((0,) + PROJ_SIZES))
LOG2E = 1.4426950408889634
Q_SCALE = HEAD_DIM ** -0.5 * LOG2E

LANES = 128
VMEM_LIMIT = 56 * 2 ** 20
ATT_BLOCK = 256
ROW_BLOCK = 512
GLOBAL_KEYS = 1280
FFN_SLOTS = 256
COMBINE_TOKENS = 256
COMBINE_PAIRS = 256
MXU_DTYPE = jnp.bfloat16
ACT_DTYPE = jnp.bfloat16
F32 = jnp.float32
HI = lax.Precision.HIGHEST
NT_DIMS = (((1,), (1,)), ((), ()))
TN_DIMS = (((0,), (0,)), ((), ()))


def _params(*sem):
    return pltpu.CompilerParams(dimension_semantics=sem, vmem_limit_bytes=VMEM_LIMIT)


def _mm(a, b):
    return jnp.dot(a.astype(MXU_DTYPE), b.astype(MXU_DTYPE), preferred_element_type=F32)


def _rms(x, g):
    return x * lax.rsqrt(jnp.mean(x * x, axis=-1, keepdims=True) + EPS) * g


def _silu(x):
    return x / (1.0 + jnp.exp(-x))


ROW_TILE = 8


def _read_row_tiles(ref, lead, n_rows):
    return jnp.concatenate([ref[lead + (pl.ds(a, n_rows, stride=ROW_TILE), slice(None))]
                            for a in range(ROW_TILE)], axis=-1)


def _write_row_tiles(ref, lead, val):
    for a in range(ROW_TILE):
        ref[lead + (pl.ds(a, val.shape[0], stride=ROW_TILE), slice(None))] = val[:, a * LANES:(a + 1) * LANES]


def _mod_body(c_ref, w_ref, b_ref, o_ref):
    s = _silu(c_ref[...])
    o_ref[0] = jnp.dot(s, w_ref[0], precision=HI, preferred_element_type=F32) + b_ref[0]


def _adaln_all(cvecs, w_mod, b_mod):
    depth, d, d6 = w_mod.shape
    tn = 1536
    rows = cvecs.shape[0]
    return pl.pallas_call(
        _mod_body,
        out_shape=jax.ShapeDtypeStruct((depth, rows, d6), F32),
        grid=(depth, d6 // tn),
        in_specs=[pl.BlockSpec((rows, d), lambda l, j: (0, 0)),
                  pl.BlockSpec((1, d, tn), lambda l, j: (l, 0, j)),
                  pl.BlockSpec((1, 1, tn), lambda l, j: (l, 0, j))],
        out_specs=pl.BlockSpec((1, rows, tn), lambda l, j: (l, 0, j)),
        compiler_params=_params("parallel", "parallel"),
        name="adaln",
    )(cvecs, w_mod, b_mod.reshape(depth, 1, d6))


def _head_rms(t, g, gsum_ref):
    w = t.shape[-1]
    sq = t * t
    hi = sq.astype(MXU_DTYPE)
    lo = (sq - hi.astype(F32)).astype(MXU_DTYPE)
    gs = gsum_ref[0:w, 0:w]
    ss = jnp.dot(hi, gs, preferred_element_type=F32) + jnp.dot(lo, gs, preferred_element_type=F32)
    return t * lax.rsqrt(ss * (1.0 / HEAD_DIM) + EPS) * g


def _rope(t, cos, sin_signed):
    w = t.shape[-1]
    rep = w // LANES
    if rep > 1:
        cos = jnp.concatenate([cos] * rep, axis=-1)
        sin_signed = jnp.concatenate([sin_signed] * rep, axis=-1)
    lane = lax.broadcasted_iota(jnp.int32, t.shape, 1)
    first_half = (lane % (2 * ROPE_FREQS)) < ROPE_FREQS
    partner = jnp.where(first_half, pltpu.roll(t, w - ROPE_FREQS, 1), pltpu.roll(t, ROPE_FREQS, 1))
    return t * cos + partner * sin_signed


def _with_ones_lane(v):
    lane = lax.broadcasted_iota(jnp.int32, (v.shape[0], LANES - HEAD_DIM), 1)
    pad = jnp.where(lane == 0, 1.0, 0.0).astype(v.dtype)
    parts = []
    for h in range(v.shape[1] // HEAD_DIM):
        parts += [v[:, h * HEAD_DIM:(h + 1) * HEAD_DIM], pad]
    return jnp.concatenate(parts, axis=-1)


def _inproj_body(x_ref, nw_ref, sh_ref, sc_ref, w_ref, cos_ref, sin_ref, qn_ref, kn_ref, gn_ref,
                 ws_ref, bs_ref, gsum_ref,
                 qa_ref, ka_ref, va_ref, qb_ref, kb_ref, vb_ref, qs_ref, ks_ref, vs_ref, yd_ref):
    x = x_ref[0]
    h = _rms(x, nw_ref[...]) * (1.0 + sc_ref[0]) + sh_ref[0]
    p = _mm(h, w_ref[...])
    o = PROJ_OFFS
    cos, sin = cos_ref[...], sin_ref[...]
    dt = qa_ref.dtype
    qa_ref[0] = (p[:, o[0]:o[1]] * Q_SCALE).astype(dt)
    ka_ref[0] = p[:, o[1]:o[2]].astype(dt)
    va_ref[0] = _with_ones_lane(p[:, o[2]:o[3]].astype(dt))
    qb = _rope(_head_rms(p[:, o[3]:o[4]], qn_ref[...], gsum_ref), cos, sin)
    qb_ref[0] = (qb * Q_SCALE).astype(dt)
    kb_ref[0] = _rope(_head_rms(p[:, o[4]:o[5]], kn_ref[...], gsum_ref), cos, sin).astype(dt)
    vb_ref[0] = _with_ones_lane(p[:, o[5]:o[6]].astype(dt))
    qs_ref[0] = (_rope(p[:, o[6]:o[7]], cos, sin) * Q_SCALE).astype(dt)
    ks_ref[0] = _rope(p[:, o[7]:o[8]], cos, sin).astype(dt)
    vs_ref[0] = _with_ones_lane(p[:, o[8]:o[9]].astype(dt))
    u = jax.nn.gelu(p[:, o[9]:o[10]])
    v = _rms(jax.nn.gelu(p[:, o[10]:o[11]]), gn_ref[...]).astype(MXU_DTYPE)
    lane_group = lax.broadcasted_iota(jnp.int32, (CHUNK, MIXER_WIDTH), 1) // (MIXER_WIDTH // SG_GROUPS)
    for c in range(x.shape[0] // CHUNK):
        rows = slice(c * CHUNK, (c + 1) * CHUNK)
        mixed = bs_ref[...]
        for g in range(SG_GROUPS):
            mg = jnp.dot(ws_ref[g], v[rows], preferred_element_type=F32)
            mixed = mixed + jnp.where(lane_group == g, mg, 0.0)
        yd_ref[0, rows, :] = (u[rows] * mixed).astype(dt)


def _inproj(x, nw, shift, scale, lw, cos, sin, tm):
    b, t, d = x.shape
    widths = tuple(w * (LANES // HEAD_DIM if j % 3 == 2 else 1) for j, w in enumerate(PROJ_SIZES[:9]))
    widths += (MIXER_WIDTH,)
    row = lambda bi, i: (0, 0)
    per_b = lambda bi, i: (bi, 0, 0)
    tile = lambda bi, i: (bi, i, 0)
    return pl.pallas_call(
        _inproj_body,
        out_shape=[jax.ShapeDtypeStruct((b, t, w), ACT_DTYPE) for w in widths],
        grid=(b, t // tm),
        in_specs=[pl.BlockSpec((1, tm, d), tile),
                  pl.BlockSpec((1, d), row),
                  pl.BlockSpec((1, 1, d), per_b),
                  pl.BlockSpec((1, 1, d), per_b),
                  pl.BlockSpec(lw["w_in"].shape, row),
                  pl.BlockSpec((tm, LANES), lambda bi, i: (i, 0)),
                  pl.BlockSpec((tm, LANES), lambda bi, i: (i, 0)),
                  pl.BlockSpec((1, 256), row),
                  pl.BlockSpec((1, 128), row),
                  pl.BlockSpec((1, 256), row),
                  pl.BlockSpec((SG_GROUPS, CHUNK, CHUNK), lambda bi, i: (0, 0, 0)),
                  pl.BlockSpec((CHUNK, MIXER_WIDTH), row),
                  pl.BlockSpec((256, 256), row)],
        out_specs=[pl.BlockSpec((1, tm, w), tile) for w in widths],
        compiler_params=_params("parallel", "parallel"),
        name="inproj",
    )(x, nw, shift, scale, lw["w_in"], cos, sin, lw["qn"], lw["kn"], lw["gn"], lw["w_sgu"],
      lw["b_sgu"], lw["gsum"])


def _attend_all(jobs):
    staged = []
    for q, chunks, sink in jobs:
        scores = []
        for k, _, bias in chunks:
            s = lax.dot_general(q, k, NT_DIMS, preferred_element_type=F32)
            scores.append(s if bias is None else s + bias)
        m = jnp.max(functools.reduce(jnp.maximum, scores), axis=-1, keepdims=True)
        staged.append((scores, m if sink is None else jnp.maximum(m, sink)))
    outs = []
    for (q, chunks, sink), (scores, m) in zip(jobs, staged):
        acc = jnp.zeros((q.shape[0], LANES), F32)
        for s, (_, v, _) in zip(scores, chunks):
            acc = acc + jnp.dot(jnp.exp2(s - m).astype(v.dtype), v, preferred_element_type=F32)
        l = acc[:, HEAD_DIM:HEAD_DIM + 1]
        if sink is not None:
            l = l + jnp.exp2(sink - m)
        outs.append(acc[:, :HEAD_DIM] / l)
    return outs


def _head(ref, h, width=HEAD_DIM):
    return ref[0, :, h * width:(h + 1) * width]


def _local_attn_body(*refs, group, has_sink):
    if has_sink:
        sink_ref, refs = refs[0], refs[1:]
    q_ref, k0, k1, k2, v0, v1, v2, kc_ref, vc_ref, bias_ref, o_ref = refs
    jobs = []
    for h in range(q_ref.shape[-1] // HEAD_DIM):
        kv = h // group
        hb = h if bias_ref.shape[1] > 1 else 0
        chunks = [(_head(kr, kv), _head(vr, kv, LANES), bias_ref[0, hb, j])
                  for j, (kr, vr) in enumerate(((k0, v0), (k1, v1), (k2, v2)))]
        chunks.append((_head(kc_ref, kv), _head(vc_ref, kv, LANES), None))
        jobs.append((_head(q_ref, h), chunks, sink_ref[h] if has_sink else None))
    o_ref[0] = jnp.concatenate(_attend_all(jobs), axis=-1).astype(o_ref.dtype)


def _local_attn(q, k, v, kc, vc, bias, sink=None):
    b, n, qw = q.shape
    kw = k.shape[-1]
    nc = kc.shape[1]
    tq = ATT_BLOCK
    nblk = n // tq
    assert nblk >= 4
    group = qw // kw
    hb = bias.shape[1]

    def kmap(j):
        return lambda bi, i: (bi, jnp.clip(i - 1, 0, nblk - 3) + j, 0)

    def bmap(bi, i):
        return (jnp.where(i == 0, 0, jnp.where(i == nblk - 1, 2, 1)), 0, 0, 0, 0)

    in_specs = [pl.BlockSpec((1, tq, qw), lambda bi, i: (bi, i, 0))]
    vw = v.shape[-1]
    assert nc == tq
    in_specs += [pl.BlockSpec((1, tq, kw), kmap(j)) for j in range(3)]
    in_specs += [pl.BlockSpec((1, tq, vw), kmap(j)) for j in range(3)]
    in_specs += [pl.BlockSpec((1, nc, kw), lambda bi, i: (bi, 0, 0)),
                 pl.BlockSpec((1, nc, vw), lambda bi, i: (bi, 0, 0))]
    in_specs += [pl.BlockSpec((1, hb, 3, tq, tq), bmap)]
    args = [q, k, k, k, v, v, v, kc, vc, bias]
    if sink is not None:
        in_specs = [pl.BlockSpec(memory_space=pltpu.SMEM)] + in_specs
        args = [sink] + args
    return pl.pallas_call(
        functools.partial(_local_attn_body, group=group, has_sink=sink is not None),
        out_shape=jax.ShapeDtypeStruct((b, n, qw), ACT_DTYPE),
        grid=(b, nblk),
        in_specs=in_specs,
        out_specs=pl.BlockSpec((1, tq, qw), lambda bi, i: (bi, i, 0)),
        compiler_params=_params("parallel", "parallel"),
        name="local_attn",
    )(*args)


def _ctx_attn_body(sink_ref, qa, ka, va, qb, kb, vb, qs, ks, vs, oa, ob, oc):
    for q_ref, k_ref, v_ref, o_ref, group, use_sink in (
            (qa, ka, va, oa, 1, False), (qb, kb, vb, ob, 2, False), (qs, ks, vs, oc, 2, True)):
        jobs = [(_head(q_ref, h), [(_head(k_ref, h // group), _head(v_ref, h // group, LANES), None)],
                 sink_ref[h] if use_sink else None) for h in range(q_ref.shape[-1] // HEAD_DIM)]
        o_ref[0] = jnp.concatenate(_attend_all(jobs), axis=-1).astype(o_ref.dtype)


def _ctx_attn(sink, qkv):
    b, nc, _ = qkv[0].shape
    spec = lambda a: pl.BlockSpec((1, nc, a.shape[-1]), lambda bi: (bi, 0, 0))
    return pl.pallas_call(
        _ctx_attn_body,
        out_shape=[jax.ShapeDtypeStruct((b, nc, MIXER_WIDTH), ACT_DTYPE)] * 3,
        grid=(b,),
        in_specs=[pl.BlockSpec(memory_space=pltpu.SMEM)] + [spec(a) for a in qkv],
        out_specs=[pl.BlockSpec((1, nc, MIXER_WIDTH), lambda bi: (bi, 0, 0))] * 3,
        compiler_params=_params("parallel"),
        name="ctx_attn",
    )(sink, *qkv)


GLOBAL_UNROLL = 6


def _global_attn_body(q_ref, kt_ref, v_ref, o_ref, s_a, s_b, *, tk):
    tq = q_ref.shape[1]
    n_kv = kt_ref.shape[1] // HEAD_DIM
    n_chunks = kt_ref.shape[2] // tk
    group_w = 2 * HEAD_DIM
    qs = [jnp.concatenate([q_ref[0, :, kv * group_w:kv * group_w + HEAD_DIM],
                           q_ref[0, :, kv * group_w + HEAD_DIM:(kv + 1) * group_w]], axis=0)
          for kv in range(n_kv)]

    def scores(i, s_ref):
        ks = pl.multiple_of(i * tk, tk)
        for kv in range(n_kv):
            s_ref[kv] = jnp.dot(qs[kv], kt_ref[0, kv * HEAD_DIM:(kv + 1) * HEAD_DIM, pl.ds(ks, tk)],
                                preferred_element_type=F32)

    def update(i, s_ref, carry):
        ks = pl.multiple_of(i * tk, tk)
        out = []
        for kv in range(n_kv):
            m, acc = carry[kv]
            s = s_ref[kv]
            m_new = jnp.maximum(m, jnp.max(s, axis=-1, keepdims=True))
            p = jnp.exp2(s - m_new).astype(v_ref.dtype)
            pv = jnp.dot(p, v_ref[0, pl.ds(ks, tk), kv * LANES:(kv + 1) * LANES], preferred_element_type=F32)
            out.append((m_new, jnp.exp2(m - m_new) * acc + pv))
        return tuple(out)

    carry = tuple((jnp.full((2 * tq, 1), NEG_INF, F32), jnp.zeros((2 * tq, LANES), F32)) for _ in range(n_kv))
    scores(0, s_a)

    def pair(j, carry):
        scores(2 * j + 1, s_b)
        carry = update(2 * j, s_a, carry)
        scores(2 * j + 2, s_a)
        return update(2 * j + 1, s_b, carry)

    n_pairs = (n_chunks - 1) // 2
    unroll = GLOBAL_UNROLL if n_pairs % GLOBAL_UNROLL == 0 else 1

    def pairs(j, carry):
        for u in range(unroll):
            carry = pair(j * unroll + u, carry)
        return carry

    carry = lax.fori_loop(0, n_pairs // unroll, pairs, carry)
    if n_chunks % 2 == 0:
        scores(n_chunks - 1, s_b)
        carry = update(n_chunks - 2, s_a, carry)
        carry = update(n_chunks - 1, s_b, carry)
    else:
        carry = update(n_chunks - 1, s_a, carry)
    outs = []
    for _, acc in carry:
        o = acc[:, :HEAD_DIM] / acc[:, HEAD_DIM:HEAD_DIM + 1]
        outs += [o[:tq], o[tq:]]
    o_ref[0] = jnp.concatenate(outs, axis=-1).astype(o_ref.dtype)


def _global_attn(q, k, v):
    b, n, qw = q.shape
    nk, kw = k.shape[1:]
    n_kv = kw // HEAD_DIM
    tq = ATT_BLOCK
    tk = GLOBAL_KEYS if (nk % GLOBAL_KEYS == 0 and nk > GLOBAL_KEYS) else ATT_BLOCK
    kt = jnp.swapaxes(k, 1, 2)
    return pl.pallas_call(
        functools.partial(_global_attn_body, tk=tk),
        out_shape=jax.ShapeDtypeStruct((b, n, qw), ACT_DTYPE),
        grid=(b, n // tq),
        in_specs=[pl.BlockSpec((1, tq, qw), lambda bi, i: (bi, i, 0)),
                  pl.BlockSpec((1, kw, nk), lambda bi, i: (bi, 0, 0)),
                  pl.BlockSpec((1, nk, n_kv * LANES), lambda bi, i: (bi, 0, 0))],
        out_specs=pl.BlockSpec((1, tq, qw), lambda bi, i: (bi, i, 0)),
        scratch_shapes=[pltpu.VMEM((n_kv, 2 * tq, tk), F32), pltpu.VMEM((n_kv, 2 * tq, tk), F32)],
        compiler_params=_params("parallel", "parallel"),
        name="global_attn",
    )(q, kt, v)


def _merge_body(ya, yb, yc, yd, x_ref, on_ref, wo_ref, g1_ref, nf_ref, sh_ref, sc_ref, wr_ref,
                xo_ref, h_ref, aff_ref):
    parts = []
    for j, r in enumerate((ya, yb, yc, yd)):
        y = r[0].astype(F32)
        parts.append(_rms(y, on_ref[:, j * MIXER_WIDTH:(j + 1) * MIXER_WIDTH]).astype(MXU_DTYPE))
    xn = x_ref[0] + g1_ref[0] * _mm(jnp.concatenate(parts, axis=-1), wo_ref[...])
    xo_ref[0] = xn
    h = _rms(xn, nf_ref[...]) * (1.0 + sc_ref[0]) + sh_ref[0]
    _write_row_tiles(h_ref, (0,), h)
    logits = lax.dot_general(wr_ref[...], h, NT_DIMS, precision=HI, preferred_element_type=F32)
    e = jnp.exp(logits - jnp.max(logits, axis=0, keepdims=True))
    aff_ref[0] = e / jnp.sum(e, axis=0, keepdims=True)


def _merge(ys, x, lw, g1, shift, scale, tm):
    b, t, d = x.shape
    ne = lw["w_router_t"].shape[0]
    row = lambda bi, i: (0, 0)
    per_b = lambda bi, i: (bi, 0, 0)
    tile = lambda bi, i: (bi, i, 0)
    return pl.pallas_call(
        _merge_body,
        out_shape=[jax.ShapeDtypeStruct((b, t, d), F32), jax.ShapeDtypeStruct((b, t * ROW_TILE, LANES), F32),
                   jax.ShapeDtypeStruct((b, ne, t), F32)],
        grid=(b, t // tm),
        in_specs=[pl.BlockSpec((1, tm, MIXER_WIDTH), tile)] * 4 + [
            pl.BlockSpec((1, tm, d), tile),
            pl.BlockSpec((1, d), row),
            pl.BlockSpec((d, d), row),
            pl.BlockSpec((1, 1, d), per_b),
            pl.BlockSpec((1, d), row),
            pl.BlockSpec((1, 1, d), per_b),
            pl.BlockSpec((1, 1, d), per_b),
            pl.BlockSpec((ne, d), row)],
        out_specs=[pl.BlockSpec((1, tm, d), tile), pl.BlockSpec((1, tm * ROW_TILE, LANES), tile),
                   pl.BlockSpec((1, ne, tm), lambda bi, i: (bi, 0, i))],
        compiler_params=_params("parallel", "parallel"),
        name="merge",
    )(*ys, x, lw["out_norm"], lw["w_out"], g1, lw["norm_ffn"], shift, scale, lw["w_router_t"])


def _tri(n, m, mode):
    r = lax.broadcasted_iota(jnp.int32, (n, m), 0)
    c = lax.broadcasted_iota(jnp.int32, (n, m), 1)
    return jnp.where({"lt": r < c, "le": r <= c, "gt": r > c}[mode], 1.0, 0.0).astype(MXU_DTYPE)


def _count(mask, axes):
    out = jnp.where(mask, 1.0, 0.0)
    for ax in sorted(axes, reverse=True):
        out = jnp.sum(out, axis=ax, keepdims=True)
    return out


def _kth_largest_bits(bits, cap, axes):
    shape = tuple(1 if a in axes else s for a, s in enumerate(bits.shape))

    def body(i, t):
        cand = t | lax.shift_left(jnp.int32(1), 30 - i)
        return jnp.where(_count(bits >= cand, axes) >= cap, cand, t)

    return lax.fori_loop(0, 31, body, jnp.zeros(shape, jnp.int32))


def _prefix_tokens(m, exact_rows):
    e, r, l = m.shape
    m2 = m.reshape(e * r, l).astype(MXU_DTYPE)
    local = jnp.dot(m2, _tri(l, l, "lt"), preferred_element_type=F32)
    rowtot = jnp.dot(m2, jnp.ones((l, l), MXU_DTYPE), preferred_element_type=F32).reshape(e, r, l)
    below = _tri(r, r, "gt")
    if exact_rows:
        base = [jnp.dot(below, rowtot[i].astype(MXU_DTYPE), preferred_element_type=F32) for i in range(e)]
    else:
        base = [jnp.dot(below.astype(F32), rowtot[i], precision=HI, preferred_element_type=F32)
                for i in range(e)]
    return local.reshape(e, r, l) + jnp.stack(base, axis=0)


def _select_mask(aff, cap, prefix_fn, axes):
    bits = pltpu.bitcast(aff, jnp.int32)
    thr = _kth_largest_bits(bits, cap, axes)
    gt = bits > thr
    eq = bits == thr
    need = cap - _count(gt, axes)
    eq_rank = prefix_fn(jnp.where(eq, 1.0, 0.0))
    take_eq = jnp.where(eq, jnp.where(eq_rank < need, 1.0, 0.0), 0.0)
    return jnp.where(gt, 1.0, take_eq)


def _select_body(aff_ref, sel_ref, prank_ref, tstart_ref, tend_ref, *, cap):
    aff = aff_ref[0]
    ne = aff.shape[0]
    sel = _select_mask(aff, cap, functools.partial(_prefix_tokens, exact_rows=True), (1, 2))
    sel_ref[0] = sel
    cnt = jnp.sum(sel, axis=0)
    tstart = _prefix_tokens(cnt[None], exact_rows=False)[0]
    tstart_ref[0] = tstart.astype(jnp.int32)
    tend_ref[0] = (tstart + cnt).astype(jnp.int32)
    run = tstart
    for e in range(ne):
        prank_ref[0, e] = run.astype(jnp.int32)
        run = run + sel[e]


def _select(aff4, cap):
    b, ne, r, l = aff4.shape
    blk4 = pl.BlockSpec((1, ne, r, l), lambda bi: (bi, 0, 0, 0))
    blk3 = pl.BlockSpec((1, r, l), lambda bi: (bi, 0, 0))
    return pl.pallas_call(
        functools.partial(_select_body, cap=cap),
        out_shape=[jax.ShapeDtypeStruct((b, ne, r, l), F32), jax.ShapeDtypeStruct((b, ne, r, l), jnp.int32),
                   jax.ShapeDtypeStruct((b, r, l), jnp.int32), jax.ShapeDtypeStruct((b, r, l), jnp.int32)],
        grid=(b,),
        in_specs=[blk4],
        out_specs=[blk4, blk4, blk3, blk3],
        compiler_params=_params("parallel"),
        name="moe_select",
    )(aff4)


def _slots_body(sel_ref, aff_ref, prank_ref, idx_ref, dest_ref, gate_ref):
    m = sel_ref[0, 0]
    r, l = m.shape
    cap = idx_ref.shape[2]
    mb = m.astype(MXU_DTYPE)
    linc = jnp.dot(mb, _tri(l, l, "le"), preferred_element_type=F32)
    rowtot = jnp.dot(mb, jnp.ones((l, l), MXU_DTYPE), preferred_element_type=F32)
    rowtot_lane = lax.dot_general(jnp.ones((8, l), MXU_DTYPE), mb, NT_DIMS, preferred_element_type=F32)
    cumrow = jnp.dot(rowtot_lane.astype(MXU_DTYPE), _tri(r, r, "le"), preferred_element_type=F32)[0:1]
    slot = lax.broadcasted_iota(jnp.int32, (cap, r), 0).astype(F32)
    passed = jnp.where(cumrow <= slot, 1.0, 0.0).astype(MXU_DTYPE)
    row_of = jnp.dot(passed, jnp.ones((r, l), MXU_DTYPE), preferred_element_type=F32)[:, 0:1]
    base_of = jnp.dot(passed, rowtot.astype(MXU_DTYPE), preferred_element_type=F32)[:, 0:1]
    onehot = jnp.where(lax.broadcasted_iota(jnp.int32, (cap, r), 1).astype(F32) == row_of, 1.0, 0.0)
    linc_of = jnp.dot(onehot.astype(MXU_DTYPE), linc.astype(MXU_DTYPE), preferred_element_type=F32)
    k = slot[:, 0:1] - base_of
    col_of = jnp.sum(jnp.where(linc_of <= k, 1.0, 0.0), axis=-1, keepdims=True)
    idx_ref[0, 0] = (row_of * l + col_of).astype(jnp.int32)
    at_col = lax.broadcasted_iota(jnp.int32, (cap, l), 1).astype(F32) == col_of
    hot = onehot.astype(MXU_DTYPE)
    aff_rows, rest = jnp.zeros((cap, l), F32), aff_ref[0, 0]
    for _ in range(3):
        piece = rest.astype(MXU_DTYPE)
        aff_rows = aff_rows + jnp.dot(hot, piece, preferred_element_type=F32)
        rest = rest - piece.astype(F32)
    gate_ref[0, 0] = jnp.sum(jnp.where(at_col, aff_rows, 0.0), axis=-1, keepdims=True)
    prank = prank_ref[0, 0]
    prank_rows = (jnp.dot(hot, (prank >> 8).astype(F32).astype(MXU_DTYPE), preferred_element_type=F32) * 256.0
                  + jnp.dot(hot, (prank & 255).astype(F32).astype(MXU_DTYPE), preferred_element_type=F32))
    dest_ref[0, 0] = jnp.sum(jnp.where(at_col, prank_rows, 0.0), axis=-1, keepdims=True).astype(jnp.int32)


def _slots(sel, aff4, prank, cap):
    b, ne, r, l = sel.shape
    blk = pl.BlockSpec((1, 1, r, l), lambda bi, e: (bi, e, 0, 0))
    oblk = pl.BlockSpec((1, 1, cap, 1), lambda bi, e: (bi, e, 0, 0))
    return pl.pallas_call(
        _slots_body,
        out_shape=[jax.ShapeDtypeStruct((b, ne, cap, 1), jnp.int32), jax.ShapeDtypeStruct((b, ne, cap, 1), jnp.int32),
                   jax.ShapeDtypeStruct((b, ne, cap, 1), F32)],
        grid=(b, ne),
        in_specs=[blk, blk, blk],
        out_specs=[oblk, oblk, oblk],
        compiler_params=_params("parallel", "parallel"),
        name="moe_slots",
    )(sel, aff4, prank)


FFN_CHUNK = 256
DMA_THREADS = 2


def _ffn_body(idx_ref, dest_ref, h_hbm, gate_ref, wg_ref, wu_ref, wd_ref, z_hbm, xa, xb, ya, yb, sems,
              *, n_tok, n_pair):
    n_b, tiles = pl.num_programs(1), pl.num_programs(2)
    step = (pl.program_id(0) * n_b + pl.program_id(1)) * tiles + pl.program_id(2)
    last = pl.num_programs(0) * n_b * tiles - 1
    ts = xa.shape[0] // ROW_TILE
    n_groups = wg_ref.shape[2] // FFN_CHUNK

    def sample_of(k):
        return (k // tiles) % n_b

    def tile_of(ref, r):
        return ref.at[pl.ds(pl.multiple_of(r * ROW_TILE, ROW_TILE), ROW_TILE), :]

    def gather(k, half, buf, sem):
        base, rows = (2 * k + half) * ts, sample_of(k) * n_tok
        return lambda s: pltpu.make_async_copy(
            tile_of(h_hbm, rows + idx_ref[base + s]), tile_of(buf, s), sems.at[sem])

    def scatter(k, half, buf, sem):
        base, rows = (2 * k + half) * ts, sample_of(k) * n_pair
        return lambda s: pltpu.make_async_copy(
            tile_of(buf, s), tile_of(z_hbm, rows + dest_ref[base + s]), sems.at[sem])

    def start_all(copy):
        def body(g, c):
            for u in range(8):
                copy(g * 8 + u).start(priority=u % DMA_THREADS)
            return c
        lax.fori_loop(0, ts // 8, body, 0)

    def wait_rows(buf, sem):
        pltpu.make_async_copy(buf, buf, sems.at[sem]).wait()

    def ffn(xbuf, gate, copies):
        x = _read_row_tiles(xbuf, (), ts).astype(MXU_DTYPE)
        for copy in copies:
            for s in range(ts):
                copy(s).start(priority=s % DMA_THREADS)
        y = None
        for j in range(n_groups):
            cols = slice(j * FFN_CHUNK, (j + 1) * FFN_CHUNK)
            hid = _silu(_mm(x, wg_ref[0, :, cols])) * _mm(x, wu_ref[0, :, cols])
            part = _mm(hid, wd_ref[0, cols, :])
            y = part if y is None else y + part
        return y * gate

    nxt = jnp.minimum(step + 1, last)
    gate_a, gate_b = gate_ref[0, 0, 0:ts], gate_ref[0, 0, ts:2 * ts]
    prefetch = [gather(step, 1, xb, 1), gather(nxt, 0, xa, 0)]

    @pl.when(step == 0)
    def _():
        start_all(gather(step, 0, xa, 0))
        wait_rows(xa, 0)
        _write_row_tiles(ya, (), ffn(xa, gate_a, prefetch))

    @pl.when(step > 0)
    def _():
        wait_rows(xa, 0)
        y_a = ffn(xa, gate_a, [scatter(step - 1, 0, ya, 2), scatter(step - 1, 1, yb, 3)] + prefetch)
        wait_rows(ya, 2)
        _write_row_tiles(ya, (), y_a)

    wait_rows(xb, 1)
    y_b = ffn(xb, gate_b, [])

    @pl.when(step > 0)
    def _():
        wait_rows(yb, 3)

    _write_row_tiles(yb, (), y_b)

    @pl.when(step == last)
    def _():
        start_all(scatter(step, 0, ya, 2))
        start_all(scatter(step, 1, yb, 3))
        wait_rows(xa, 0)
        wait_rows(ya, 2)
        wait_rows(yb, 3)


def _expert_ffn(idx, dest, gate, h, lw, ts):
    b, ne, cap, _ = gate.shape
    n, d = h.shape[1] // ROW_TILE, ROW_TILE * LANES
    f = lw["w_gate"].shape[-1]
    layer = lw["layer"]
    n_pair = ne * cap
    by_expert = lambda a: jnp.swapaxes(a, 0, 1)
    grid_spec = pltpu.PrefetchScalarGridSpec(
        num_scalar_prefetch=2,
        grid=(ne, b, cap // (2 * ts)),
        in_specs=[pl.BlockSpec(memory_space=pl.ANY),
                  pl.BlockSpec((1, 1, 2 * ts, 1), lambda e, bi, t, *_: (e, bi, t, 0)),
                  pl.BlockSpec((None, 1, d, f), lambda e, bi, t, *_: (layer, e, 0, 0)),
                  pl.BlockSpec((None, 1, d, f), lambda e, bi, t, *_: (layer, e, 0, 0)),
                  pl.BlockSpec((None, 1, f, d), lambda e, bi, t, *_: (layer, e, 0, 0))],
        out_specs=pl.BlockSpec(memory_space=pl.ANY),
        scratch_shapes=[pltpu.VMEM((ts * ROW_TILE, LANES), F32)] * 4 + [pltpu.SemaphoreType.DMA((4,))])
    return pl.pallas_call(
        functools.partial(_ffn_body, n_tok=n, n_pair=n_pair),
        out_shape=jax.ShapeDtypeStruct((b * n_pair * ROW_TILE, LANES), F32),
        grid_spec=grid_spec,
        compiler_params=_params("arbitrary", "arbitrary", "arbitrary"),
        name="moe_ffn",
    )(by_expert(idx).reshape(-1), by_expert(dest).reshape(-1), h.reshape(b * n * ROW_TILE, LANES), by_expert(gate),
      lw["w_gate"], lw["w_up"], lw["w_down"])


COMBINE_DEPTH = 4


def _combine_body(tb_ref, x_ref, g2_ref, ts_ref, te_ref, fg_ref, z_hbm, o_ref, zbuf, acc_ref, sem,
                  *, n_pair, final):
    n_tiles = pl.num_programs(1)
    tile = pl.program_id(0) * n_tiles + pl.program_id(1)
    depth, pc = zbuf.shape[0], zbuf.shape[1] // ROW_TILE
    tt = x_ref.shape[1]

    def plan(g):
        b, i = g // n_tiles, g % n_tiles
        lo, hi = tb_ref[b * (n_tiles + 1) + i], tb_ref[b * (n_tiles + 1) + i + 1]
        p0 = (lo // 8) * 8
        return b, p0, (hi - p0 + pc - 1) // pc

    def fetch(b, p0, k, slot):
        cs = pl.multiple_of(jnp.minimum(p0 + k * pc, n_pair - pc), 8)
        rows = pl.ds(pl.multiple_of((b * n_pair + cs) * ROW_TILE, 8 * ROW_TILE), pc * ROW_TILE)
        return cs, pltpu.make_async_copy(z_hbm.at[rows, :], zbuf.at[slot], sem.at[slot])

    b, p0, n_chunks = plan(tile)
    start_row, end_row = ts_ref[0, 0], te_ref[0, 0]
    acc_ref[...] = jnp.zeros_like(acc_ref)

    def start_head(b, p0, n_chunks):
        for k in range(depth - 1):
            @pl.when(k < n_chunks)
            def _():
                fetch(b, p0, k, k)[1].start()

    @pl.when(tile == 0)
    def _():
        start_head(b, p0, n_chunks)

    def chunk(k, c):
        slot = k % depth
        ahead = k + depth - 1

        @pl.when(ahead < n_chunks)
        def _():
            fetch(b, p0, ahead, ahead % depth)[1].start()

        cs, cp = fetch(b, p0, k, slot)
        cp.wait()
        pair = cs + lax.broadcasted_iota(jnp.int32, (pc, tt), 0)
        own = jnp.where(pair >= jnp.maximum(start_row, p0 + k * pc), jnp.where(pair < end_row, 1.0, 0.0), 0.0)
        own = own.astype(MXU_DTYPE)
        z = _read_row_tiles(zbuf, (slot,), pc).astype(MXU_DTYPE)
        acc_ref[...] += lax.dot_general(own, z, TN_DIMS, preferred_element_type=F32)
        return c

    lax.fori_loop(0, n_chunks, chunk, 0)

    @pl.when(tile + 1 < pl.num_programs(0) * n_tiles)
    def _():
        start_head(*plan(tile + 1))

    out = x_ref[0] + g2_ref[0] * acc_ref[...]
    o_ref[0] = _rms(out, fg_ref[...]) if final else out


def _combine(x, g2, tstart, tend, z, n_pair, tt, pc, final_g, final):
    b, n, d = x.shape
    nt = n // tt
    ts4 = tstart.reshape(b, nt, 1, tt)
    te4 = tend.reshape(b, nt, 1, tt)
    bounds = jnp.concatenate([ts4[:, :, 0, 0], jnp.full((b, 1), n_pair, jnp.int32)], axis=1).reshape(-1)
    grid_spec = pltpu.PrefetchScalarGridSpec(
        num_scalar_prefetch=1,
        grid=(b, nt),
        in_specs=[pl.BlockSpec((1, tt, d), lambda bi, i, *_: (bi, i, 0)),
                  pl.BlockSpec((1, 1, d), lambda bi, i, *_: (bi, 0, 0)),
                  pl.BlockSpec((1, 1, 1, tt), lambda bi, i, *_: (bi, i, 0, 0)),
                  pl.BlockSpec((1, 1, 1, tt), lambda bi, i, *_: (bi, i, 0, 0)),
                  pl.BlockSpec((1, d), lambda bi, i, *_: (0, 0)),
                  pl.BlockSpec(memory_space=pl.ANY)],
        out_specs=pl.BlockSpec((1, tt, d), lambda bi, i, *_: (bi, i, 0)),
        scratch_shapes=[pltpu.VMEM((COMBINE_DEPTH, pc * ROW_TILE, LANES), F32), pltpu.VMEM((tt, d), F32),
                        pltpu.SemaphoreType.DMA((COMBINE_DEPTH,))])
    return pl.pallas_call(
        functools.partial(_combine_body, n_pair=n_pair, final=final),
        out_shape=jax.ShapeDtypeStruct((b, n, d), F32),
        grid_spec=grid_spec,
        compiler_params=_params("arbitrary", "arbitrary"),
        name="moe_combine",
    )(bounds, x, g2, ts4, te4, final_g, z)


def _moe_latent(x, h, aff, g2, lw, final_g, final):
    b, n, d = x.shape
    ne = aff.shape[1]
    cap = EC_CAPACITY_FACTOR * n // ne
    aff4 = aff.reshape(b, ne, n // LANES, LANES)
    sel, prank, tstart, tend = _select(aff4, cap)
    idx, dest, gate = _slots(sel, aff4, prank, cap)
    z = _expert_ffn(idx, dest, gate, h, lw, min(cap // 2, FFN_SLOTS))
    assert ne * cap <= 1 << 16
    return _combine(x, g2, tstart, tend, z, ne * cap, COMBINE_TOKENS, COMBINE_PAIRS, final_g, final)


def _ctx_coef_body(aff_ref, coef_ref, slot_ref, *, cap):
    nb, _, nc = aff_ref.shape
    excl = _tri(nc, nc, "lt")
    prefix = lambda mm: jnp.dot(mm.astype(MXU_DTYPE), excl, preferred_element_type=F32)
    coefs, slots = [], []
    for b in range(nb):
        aff = aff_ref[b]
        sel = _select_mask(aff, cap, prefix, (1,))
        coefs.append(sel * aff)
        slots.append(jnp.where(sel > 0.0, prefix(sel) + float(b * cap), -1.0))
    coef_ref[:, 0, :] = jnp.concatenate(coefs, axis=-1)
    slot_ref[:, 0, :] = jnp.concatenate(slots, axis=-1)


def _ctx_coef(aff, cap):
    b, ne, nc = aff.shape
    return pl.pallas_call(
        functools.partial(_ctx_coef_body, cap=cap),
        out_shape=[jax.ShapeDtypeStruct((ne, 1, b * nc), F32)] * 2,
        name="ctx_moe_select",
    )(aff)


def _ctx_ffn_body(h_ref, coef_ref, slot_ref, wg_ref, wu_ref, wd_ref, x_ref, g2_ref, o_ref, *, n_slots):
    e = pl.program_id(0)
    rows = h_ref.shape[0] // ROW_TILE
    h = _read_row_tiles(h_ref, (), rows)
    picked = jnp.where(lax.broadcasted_iota(jnp.int32, (n_slots, rows), 0).astype(F32) == slot_ref[0], 1.0, 0.0)
    gate = jnp.sum(picked * coef_ref[0], axis=-1, keepdims=True)
    x = _mm(picked, h)
    y = _mm(_silu(_mm(x, wg_ref[0])) * _mm(x, wu_ref[0]), wd_ref[0]) * gate

    @pl.when(e == 0)
    def _():
        o_ref[...] = x_ref[...]

    o_ref[...] += g2_ref[...] * lax.dot_general(picked.astype(MXU_DTYPE), y.astype(MXU_DTYPE), TN_DIMS,
                                                 preferred_element_type=F32)


def _ctx_moe(xc, hc, aff, g2, lw):
    b, nc, d = xc.shape
    ne = aff.shape[1]
    f = lw["w_gate"].shape[-1]
    layer = lw["layer"]
    cap = EC_CAPACITY_FACTOR * nc // ne
    coef, slot = _ctx_coef(aff, cap)
    rows = b * nc
    per_e = pl.BlockSpec((1, 1, rows), lambda e: (e, 0, 0))
    out = pl.pallas_call(
        functools.partial(_ctx_ffn_body, n_slots=b * cap),
        out_shape=jax.ShapeDtypeStruct((rows, d), F32),
        grid=(ne,),
        in_specs=[pl.BlockSpec((rows * ROW_TILE, LANES), lambda e: (0, 0)),
                  per_e, per_e,
                  pl.BlockSpec((None, 1, d, f), lambda e: (layer, e, 0, 0)),
                  pl.BlockSpec((None, 1, d, f), lambda e: (layer, e, 0, 0)),
                  pl.BlockSpec((None, 1, f, d), lambda e: (layer, e, 0, 0)),
                  pl.BlockSpec((rows, d), lambda e: (0, 0)),
                  pl.BlockSpec((1, d), lambda e: (0, 0))],
        out_specs=pl.BlockSpec((rows, d), lambda e: (0, 0)),
        compiler_params=_params("arbitrary"),
        name="ctx_moe_ffn",
    )(hc.reshape(rows * ROW_TILE, LANES), coef, slot, lw["w_gate"], lw["w_up"], lw["w_down"],
      xc.reshape(rows, d), g2)
    return out.reshape(b, nc, d)


def _rope_tables(n):
    t = np.arange(n)
    freqs = ROPE_THETA ** (-np.arange(ROPE_FREQS, dtype=np.float32) / ROPE_FREQS)
    ang_r = (t // GRID_W).astype(np.float32)[:, None] * freqs
    ang_c = (t % GRID_W).astype(np.float32)[:, None] * freqs
    cos = np.concatenate([np.cos(ang_r)] * 2 + [np.cos(ang_c)] * 2, axis=1)
    sin = np.concatenate([-np.sin(ang_r), np.sin(ang_r), -np.sin(ang_c), np.sin(ang_c)], axis=1)
    reps = LANES // HEAD_DIM
    return (jnp.asarray(np.tile(cos, (1, reps)), F32), jnp.asarray(np.tile(sin, (1, reps)), F32))


def _block_geometry(n):
    nblk = n // ATT_BLOCK
    a = np.arange(ATT_BLOCK)
    out = []
    for i in (0, 1, nblk - 1):
        j0 = int(np.clip(i - 1, 0, nblk - 3))
        out.append((i * ATT_BLOCK + a, [(j0 + j) * ATT_BLOCK + a for j in range(3)]))
    return out


def _neighbourhood_bias(rpb, n):
    rows = n // GRID_W
    win_r = min(NA_WIN_R, rows)
    blk_rows = ATT_BLOCK // GRID_W
    cols = np.arange(GRID_W)
    col_off = cols[None, :] - cols[:, None] + NA_WIN_C - 1
    col_hot = jnp.asarray(col_off[:, :, None] == np.arange(2 * NA_WIN_C - 1), F32)
    c0 = np.clip(cols - NA_WIN_C // 2, 0, GRID_W - NA_WIN_C)
    col_ok = (cols[None, :] >= c0[:, None]) & (cols[None, :] < c0[:, None] + NA_WIN_C)
    kinds = []
    for q_tok, k_chunks in _block_geometry(n):
        qr = q_tok[::GRID_W] // GRID_W
        r0 = np.clip(qr - win_r // 2, 0, rows - win_r)
        chunks = []
        for k_tok in k_chunks:
            kr = k_tok[::GRID_W] // GRID_W
            row_off = kr[None, :] - qr[:, None] + NA_WIN_R - 1
            row_hot = jnp.asarray(row_off[:, :, None] == np.arange(2 * NA_WIN_R - 1), F32)
            row_ok = (kr[None, :] >= r0[:, None]) & (kr[None, :] < r0[:, None] + win_r)
            ok = (row_ok[:, None, :, None] & col_ok[None, :, None, :]).reshape(ATT_BLOCK, ATT_BLOCK)
            vals = jnp.einsum("qkr,hrc,xyc->hqxky", row_hot, rpb, col_hot, precision=HI)
            vals = vals.reshape(-1, blk_rows * GRID_W, blk_rows * GRID_W) * LOG2E
            chunks.append(jnp.where(ok[None], vals, NEG_INF))
        kinds.append(jnp.stack(chunks, axis=1))
    return jnp.stack(kinds, axis=0)


def _window_mask(n):
    kinds = []
    for q_tok, k_chunks in _block_geometry(n):
        kinds.append(np.stack([np.where(np.abs(k_tok[None] - q_tok[:, None]) <= WINDOW, 0.0, NEG_INF)
                               for k_tok in k_chunks])[None])
    return jnp.asarray(np.stack(kinds), F32)


def kernel(x, c, ctx, c_ctx, w_mod, b_mod, norm_mix, norm_ffn, w_in, rpb, q_norm, k_norm, sink, sgu_norm, w_sgu,
           b_sgu, out_norm, w_out, w_router, w_gate, w_up, w_down, final_norm):
    depth = w_mod.shape[0]
    b, n, d = x.shape
    nc = ctx.shape[1]
    group_w = MIXER_WIDTH // SG_GROUPS

    cvecs = jnp.concatenate([c, c_ctx[None], jnp.zeros((8 - b - 1, d), F32)], axis=0)
    mods = _adaln_all(cvecs, w_mod, b_mod).reshape(depth, 8, 6, d)

    cos, sin = _rope_tables(n)
    cos_c, sin_c = jnp.ones((nc, LANES), F32), jnp.zeros((nc, LANES), F32)
    win_mask = _window_mask(n)
    blk = np.arange(256) // HEAD_DIM
    gsum = jnp.asarray(blk[:, None] == blk[None, :], MXU_DTYPE)
    wg_all, wu_all, wd_all = (w.astype(MXU_DTYPE) for w in (w_gate, w_up, w_down))

    xc = ctx
    for l in range(depth):
        ctx_needed = l < depth - 1
        lat = [mods[l, :b, j][:, None, :] for j in range(6)]
        cm = [jnp.broadcast_to(mods[l, b, j][None, None, :], (b, 1, d)) for j in range(6)]
        lw = {
            "w_in": w_in[l].astype(MXU_DTYPE),
            "qn": jnp.tile(q_norm[l], 4)[None], "kn": jnp.tile(k_norm[l], 2)[None], "gn": sgu_norm[l][None],
            "w_sgu": w_sgu[l].astype(MXU_DTYPE),
            "b_sgu": jnp.repeat(b_sgu[l].T, group_w, axis=1),
            "gsum": gsum,
            "out_norm": out_norm[l][None], "w_out": w_out[l].astype(MXU_DTYPE), "norm_ffn": norm_ffn[l][None],
            "w_router_t": w_router[l].T,
            "layer": l, "w_gate": wg_all, "w_up": wu_all, "w_down": wd_all,
        }
        nm = norm_mix[l][None]
        hp = _inproj(x, nm, lat[0], lat[1], lw, cos, sin, ROW_BLOCK)
        cp = _inproj(xc, nm, cm[0], cm[1], lw, cos_c, sin_c, nc)
        qa, ka, va, qb, kb, vb, qs, ks, vs, yd = hp
        _, ka_c, va_c, _, kb_c, vb_c, _, ks_c, vs_c, _ = cp
        sink_l = sink[l] * LOG2E

        ya = _local_attn(qa, ka, va, ka_c, va_c, _neighbourhood_bias(rpb[l], n))
        yb = _global_attn(qb, jnp.concatenate([kb, kb_c], axis=1), jnp.concatenate([vb, vb_c], axis=1))
        yc = _local_attn(qs, ks, vs, ks_c, vs_c, win_mask, sink_l)
        x_mid, h2, aff = _merge((ya, yb, yc, yd), x, lw, lat[2], lat[3], lat[4], ROW_BLOCK)
        x = _moe_latent(x_mid, h2, aff, lat[5], lw, final_norm[None], l == depth - 1)

        if ctx_needed:
            ys_c = _ctx_attn(sink_l, cp[:9])
            xc_mid, hc2, aff_c = _merge((*ys_c, cp[9]), xc, lw, cm[2], cm[3], cm[4], nc)
            xc = _ctx_moe(xc_mid, hc2, aff_c, mods[l, b, 5][None], lw)
    return x
```

```python
import functools

import numpy as np
import jax
import jax.numpy as jnp
from jax import lax
from jax.experimental import pallas as pl
from jax.experimental.pallas import tpu as pltpu

HEAD_DIM = 64
GRID_W = 64
MIXER_WIDTH = 256
NA_WIN_R = 8
NA_WIN_C = 16
WINDOW = 128
CHUNK = 128
SG_GROUPS = 4
EC_CAPACITY_FACTOR = 2
ROPE_THETA = 10000.0
ROPE_FREQS = HEAD_DIM // 4
EPS = 1e-6
NEG_INF = -1e30
PROJ_SIZES = (256, 256, 256, 256, 128, 128, 256, 128, 128, 256, 256)
PROJ_OFFS = tuple(int(v) for v in np.cumsum((0,) + PROJ_SIZES))
LOG2E = 1.4426950408889634
Q_SCALE = HEAD_DIM ** -0.5 * LOG2E

LANES = 128
VMEM_LIMIT = 56 * 2 ** 20
ATT_BLOCK = 256
ROW_BLOCK = 512
GLOBAL_KEYS = 1280
FFN_SLOTS = 256
COMBINE_TOKENS = 256
COMBINE_PAIRS = 256
MXU_DTYPE = jnp.bfloat16
ACT_DTYPE = jnp.bfloat16
F32 = jnp.float32
HI = lax.Precision.HIGHEST
NT_DIMS = (((1,), (1,)), ((), ()))
TN_DIMS = (((0,), (0,)), ((), ()))


def _params(*sem):
    return pltpu.CompilerParams(dimension_semantics=sem, vmem_limit_bytes=VMEM_LIMIT)


def _mm(a, b):
    return jnp.dot(a.astype(MXU_DTYPE), b.astype(MXU_DTYPE), preferred_element_type=F32)


def _rms(x, g):
    return x * lax.rsqrt(jnp.mean(x * x, axis=-1, keepdims=True) + EPS) * g


def _silu(x):
    return x / (1.0 + jnp.exp(-x))


ROW_TILE = 8


def _read_row_tiles(ref, lead, n_rows):
    return jnp.concatenate([ref[lead + (pl.ds(a, n_rows, stride=ROW_TILE), slice(None))]
                            for a in range(ROW_TILE)], axis=-1)


def _write_row_tiles(ref, lead, val):
    for a in range(ROW_TILE):
        ref[lead + (pl.ds(a, val.shape[0], stride=ROW_TILE), slice(None))] = val[:, a * LANES:(a + 1) * LANES]


def _mod_body(c_ref, w_ref, b_ref, o_ref):
    s = _silu(c_ref[...])
    o_ref[0] = jnp.dot(s, w_ref[0], precision=HI, preferred_element_type=F32) + b_ref[0]


def _adaln_all(cvecs, w_mod, b_mod):
    depth, d, d6 = w_mod.shape
    tn = 1536
    rows = cvecs.shape[0]
    return pl.pallas_call(
        _mod_body,
        out_shape=jax.ShapeDtypeStruct((depth, rows, d6), F32),
        grid=(depth, d6 // tn),
        in_specs=[pl.BlockSpec((rows, d), lambda l, j: (0, 0)),
                  pl.BlockSpec((1, d, tn), lambda l, j: (l, 0, j)),
                  pl.BlockSpec((1, 1, tn), lambda l, j: (l, 0, j))],
        out_specs=pl.BlockSpec((1, rows, tn), lambda l, j: (l, 0, j)),
        compiler_params=_params("parallel", "parallel"),
        name="adaln",
    )(cvecs, w_mod, b_mod.reshape(depth, 1, d6))


def _head_rms(t, g, gsum_ref):
    w = t.shape[-1]
    sq = t * t
    hi = sq.astype(MXU_DTYPE)
    lo = (sq - hi.astype(F32)).astype(MXU_DTYPE)
    gs = gsum_ref[0:w, 0:w]
    ss = jnp.dot(hi, gs, preferred_element_type=F32) + jnp.dot(lo, gs, preferred_element_type=F32)
    return t * lax.rsqrt(ss * (1.0 / HEAD_DIM) + EPS) * g


def _rope(t, cos, sin_signed):
    w = t.shape[-1]
    rep = w // LANES
    if rep > 1:
        cos = jnp.concatenate([cos] * rep, axis=-1)
        sin_signed = jnp.concatenate([sin_signed] * rep, axis=-1)
    lane = lax.broadcasted_iota(jnp.int32, t.shape, 1)
    first_half = (lane % (2 * ROPE_FREQS)) < ROPE_FREQS
    partner = jnp.where(first_half, pltpu.roll(t, w - ROPE_FREQS, 1), pltpu.roll(t, ROPE_FREQS, 1))
    return t * cos + partner * sin_signed


def _with_ones_lane(v):
    lane = lax.broadcasted_iota(jnp.int32, (v.shape[0], LANES - HEAD_DIM), 1)
    pad = jnp.where(lane == 0, 1.0, 0.0).astype(v.dtype)
    parts = []
    for h in range(v.shape[1] // HEAD_DIM):
        parts += [v[:, h * HEAD_DIM:(h + 1) * HEAD_DIM], pad]
    return jnp.concatenate(parts, axis=-1)


def _inproj_body(x_ref, nw_ref, sh_ref, sc_ref, w_ref, cos_ref, sin_ref, qn_ref, kn_ref, gn_ref,
                 ws_ref, bs_ref, gsum_ref,
                 qa_ref, ka_ref, va_ref, qb_ref, kb_ref, vb_ref, qs_ref, ks_ref, vs_ref, yd_ref):
    x = x_ref[0]
    h = _rms(x, nw_ref[...]) * (1.0 + sc_ref[0]) + sh_ref[0]
    p = _mm(h, w_ref[...])
    o = PROJ_OFFS
    cos, sin = cos_ref[...], sin_ref[...]
    dt = qa_ref.dtype
    qa_ref[0] = (p[:, o[0]:o[1]] * Q_SCALE).astype(dt)
    ka_ref[0] = p[:, o[1]:o[2]].astype(dt)
    va_ref[0] = _with_ones_lane(p[:, o[2]:o[3]].astype(dt))
    qb = _rope(_head_rms(p[:, o[3]:o[4]], qn_ref[...], gsum_ref), cos, sin)
    qb_ref[0] = (qb * Q_SCALE).astype(dt)
    kb_ref[0] = _rope(_head_rms(p[:, o[4]:o[5]], kn_ref[...], gsum_ref), cos, sin).astype(dt)
    vb_ref[0] = _with_ones_lane(p[:, o[5]:o[6]].astype(dt))
    qs_ref[0] = (_rope(p[:, o[6]:o[7]], cos, sin) * Q_SCALE).astype(dt)
    ks_ref[0] = _rope(p[:, o[7]:o[8]], cos, sin).astype(dt)
    vs_ref[0] = _with_ones_lane(p[:, o[8]:o[9]].astype(dt))
    u = jax.nn.gelu(p[:, o[9]:o[10]])
    v = _rms(jax.nn.gelu(p[:, o[10]:o[11]]), gn_ref[...]).astype(MXU_DTYPE)
    lane_group = lax.broadcasted_iota(jnp.int32, (CHUNK, MIXER_WIDTH), 1) // (MIXER_WIDTH // SG_GROUPS)
    for c in range(x.shape[0] // CHUNK):
        rows = slice(c * CHUNK, (c + 1) * CHUNK)
        mixed = bs_ref[...]
        for g in range(SG_GROUPS):
            mg = jnp.dot(ws_ref[g], v[rows], preferred_element_type=F32)
            mixed = mixed + jnp.where(lane_group == g, mg, 0.0)
        yd_ref[0, rows, :] = (u[rows] * mixed).astype(dt)


def _inproj(x, nw, shift, scale, lw, cos, sin, tm):
    b, t, d = x.shape
    widths = tuple(w * (LANES // HEAD_DIM if j % 3 == 2 else 1) for j, w in enumerate(PROJ_SIZES[:9]))
    widths += (MIXER_WIDTH,)
    row = lambda bi, i: (0, 0)
    per_b = lambda bi, i: (bi, 0, 0)
    tile = lambda bi, i: (bi, i, 0)
    return pl.pallas_call(
        _inproj_body,
        out_shape=[jax.ShapeDtypeStruct((b, t, w), ACT_DTYPE) for w in widths],
        grid=(b, t // tm),
        in_specs=[pl.BlockSpec((1, tm, d), tile),
                  pl.BlockSpec((1, d), row),
                  pl.BlockSpec((1, 1, d), per_b),
                  pl.BlockSpec((1, 1, d), per_b),
                  pl.BlockSpec(lw["w_in"].shape, row),
                  pl.BlockSpec((tm, LANES), lambda bi, i: (i, 0)),
                  pl.BlockSpec((tm, LANES), lambda bi, i: (i, 0)),
                  pl.BlockSpec((1, 256), row),
                  pl.BlockSpec((1, 128), row),
                  pl.BlockSpec((1, 256), row),
                  pl.BlockSpec((SG_GROUPS, CHUNK, CHUNK), lambda bi, i: (0, 0, 0)),
                  pl.BlockSpec((CHUNK, MIXER_WIDTH), row),
                  pl.BlockSpec((256, 256), row)],
        out_specs=[pl.BlockSpec((1, tm, w), tile) for w in widths],
        compiler_params=_params("parallel", "parallel"),
        name="inproj",
    )(x, nw, shift, scale, lw["w_in"], cos, sin, lw["qn"], lw["kn"], lw["gn"], lw["w_sgu"],
      lw["b_sgu"], lw["gsum"])


def _attend_all(jobs):
    staged = []
    for q, chunks, sink in jobs:
        scores = []
        for k, _, bias in chunks:
            s = lax.dot_general(q, k, NT_DIMS, preferred_element_type=F32)
            scores.append(s if bias is None else s + bias)
        m = jnp.max(functools.reduce(jnp.maximum, scores), axis=-1, keepdims=True)
        staged.append((scores, m if sink is None else jnp.maximum(m, sink)))
    outs = []
    for (q, chunks, sink), (scores, m) in zip(jobs, staged):
        acc = jnp.zeros((q.shape[0], LANES), F32)
        for s, (_, v, _) in zip(scores, chunks):
            acc = acc + jnp.dot(jnp.exp2(s - m).astype(v.dtype), v, preferred_element_type=F32)
        l = acc[:, HEAD_DIM:HEAD_DIM + 1]
        if sink is not None:
            l = l + jnp.exp2(sink - m)
        outs.append(acc[:, :HEAD_DIM] / l)
    return outs


def _head(ref, h, width=HEAD_DIM):
    return ref[0, :, h * width:(h + 1) * width]


def _local_attn_body(*refs, group, has_sink):
    if has_sink:
        sink_ref, refs = refs[0], refs[1:]
    q_ref, k0, k1, k2, v0, v1, v2, kc_ref, vc_ref, bias_ref, o_ref = refs
    jobs = []
    for h in range(q_ref.shape[-1] // HEAD_DIM):
        kv = h // group
        hb = h if bias_ref.shape[1] > 1 else 0
        chunks = [(_head(kr, kv), _head(vr, kv, LANES), bias_ref[0, hb, j])
                  for j, (kr, vr) in enumerate(((k0, v0), (k1, v1), (k2, v2)))]
        chunks.append((_head(kc_ref, kv), _head(vc_ref, kv, LANES), None))
        jobs.append((_head(q_ref, h), chunks, sink_ref[h] if has_sink else None))
    o_ref[0] = jnp.concatenate(_attend_all(jobs), axis=-1).astype(o_ref.dtype)


def _local_attn(q, k, v, kc, vc, bias, sink=None):
    b, n, qw = q.shape
    kw = k.shape[-1]
    nc = kc.shape[1]
    tq = ATT_BLOCK
    nblk = n // tq
    assert nblk >= 4
    group = qw // kw
    hb = bias.shape[1]

    def kmap(j):
        return lambda bi, i: (bi, jnp.clip(i - 1, 0, nblk - 3) + j, 0)

    def bmap(bi, i):
        return (jnp.where(i == 0, 0, jnp.where(i == nblk - 1, 2, 1)), 0, 0, 0, 0)

    in_specs = [pl.BlockSpec((1, tq, qw), lambda bi, i: (bi, i, 0))]
    vw = v.shape[-1]
    assert nc == tq
    in_specs += [pl.BlockSpec((1, tq, kw), kmap(j)) for j in range(3)]
    in_specs += [pl.BlockSpec((1, tq, vw), kmap(j)) for j in range(3)]
    in_specs += [pl.BlockSpec((1, nc, kw), lambda bi, i: (bi, 0, 0)),
                 pl.BlockSpec((1, nc, vw), lambda bi, i: (bi, 0, 0))]
    in_specs += [pl.BlockSpec((1, hb, 3, tq, tq), bmap)]
    args = [q, k, k, k, v, v, v, kc, vc, bias]
    if sink is not None:
        in_specs = [pl.BlockSpec(memory_space=pltpu.SMEM)] + in_specs
        args = [sink] + args
    return pl.pallas_call(
        functools.partial(_local_attn_body, group=group, has_sink=sink is not None),
        out_shape=jax.ShapeDtypeStruct((b, n, qw), ACT_DTYPE),
        grid=(b, nblk),
        in_specs=in_specs,
        out_specs=pl.BlockSpec((1, tq, qw), lambda bi, i: (bi, i, 0)),
        compiler_params=_params("parallel", "parallel"),
        name="local_attn",
    )(*args)


def _ctx_attn_body(sink_ref, qa, ka, va, qb, kb, vb, qs, ks, vs, oa, ob, oc):
    for q_ref, k_ref, v_ref, o_ref, group, use_sink in (
            (qa, ka, va, oa, 1, False), (qb, kb, vb, ob, 2, False), (qs, ks, vs, oc, 2, True)):
        jobs = [(_head(q_ref, h), [(_head(k_ref, h // group), _head(v_ref, h // group, LANES), None)],
                 sink_ref[h] if use_sink else None) for h in range(q_ref.shape[-1] // HEAD_DIM)]
        o_ref[0] = jnp.concatenate(_attend_all(jobs), axis=-1).astype(o_ref.dtype)


def _ctx_attn(sink, qkv):
    b, nc, _ = qkv[0].shape
    spec = lambda a: pl.BlockSpec((1, nc, a.shape[-1]), lambda bi: (bi, 0, 0))
    return pl.pallas_call(
        _ctx_attn_body,
        out_shape=[jax.ShapeDtypeStruct((b, nc, MIXER_WIDTH), ACT_DTYPE)] * 3,
        grid=(b,),
        in_specs=[pl.BlockSpec(memory_space=pltpu.SMEM)] + [spec(a) for a in qkv],
        out_specs=[pl.BlockSpec((1, nc, MIXER_WIDTH), lambda bi: (bi, 0, 0))] * 3,
        compiler_params=_params("parallel"),
        name="ctx_attn",
    )(sink, *qkv)


GLOBAL_UNROLL = 6


def _global_attn_body(q_ref, kt_ref, v_ref, o_ref, s_a, s_b, *, tk):
    tq = q_ref.shape[1]
    n_kv = kt_ref.shape[1] // HEAD_DIM
    n_chunks = kt_ref.shape[2] // tk
    group_w = 2 * HEAD_DIM
    qs = [jnp.concatenate([q_ref[0, :, kv * group_w:kv * group_w + HEAD_DIM],
                           q_ref[0, :, kv * group_w + HEAD_DIM:(kv + 1) * group_w]], axis=0)
          for kv in range(n_kv)]

    def scores(i, s_ref):
        ks = pl.multiple_of(i * tk, tk)
        for kv in range(n_kv):
            s_ref[kv] = jnp.dot(qs[kv], kt_ref[0, kv * HEAD_DIM:(kv + 1) * HEAD_DIM, pl.ds(ks, tk)],
                                preferred_element_type=F32)

    def update(i, s_ref, carry):
        ks = pl.multiple_of(i * tk, tk)
        out = []
        for kv in range(n_kv):
            m, acc = carry[kv]
            s = s_ref[kv]
            m_new = jnp.maximum(m, jnp.max(s, axis=-1, keepdims=True))
            p = jnp.exp2(s - m_new).astype(v_ref.dtype)
            pv = jnp.dot(p, v_ref[0, pl.ds(ks, tk), kv * LANES:(kv + 1) * LANES], preferred_element_type=F32)
            out.append((m_new, jnp.exp2(m - m_new) * acc + pv))
        return tuple(out)

    carry = tuple((jnp.full((2 * tq, 1), NEG_INF, F32), jnp.zeros((2 * tq, LANES), F32)) for _ in range(n_kv))
    scores(0, s_a)

    def pair(j, carry):
        scores(2 * j + 1, s_b)
        carry = update(2 * j, s_a, carry)
        scores(2 * j + 2, s_a)
        return update(2 * j + 1, s_b, carry)

    n_pairs = (n_chunks - 1) // 2
    unroll = GLOBAL_UNROLL if n_pairs % GLOBAL_UNROLL == 0 else 1

    def pairs(j, carry):
        for u in range(unroll):
            carry = pair(j * unroll + u, carry)
        return carry

    carry = lax.fori_loop(0, n_pairs // unroll, pairs, carry)
    if n_chunks % 2 == 0:
        scores(n_chunks - 1, s_b)
        carry = update(n_chunks - 2, s_a, carry)
        carry = update(n_chunks - 1, s_b, carry)
    else:
        carry = update(n_chunks - 1, s_a, carry)
    outs = []
    for _, acc in carry:
        o = acc[:, :HEAD_DIM] / acc[:, HEAD_DIM:HEAD_DIM + 1]
        outs += [o[:tq], o[tq:]]
    o_ref[0] = jnp.concatenate(outs, axis=-1).astype(o_ref.dtype)


def _global_attn(q, k, v):
    b, n, qw = q.shape
    nk, kw = k.shape[1:]
    n_kv = kw // HEAD_DIM
    tq = ATT_BLOCK
    tk = GLOBAL_KEYS if (nk % GLOBAL_KEYS == 0 and nk > GLOBAL_KEYS) else ATT_BLOCK
    kt = jnp.swapaxes(k, 1, 2)
    return pl.pallas_call(
        functools.partial(_global_attn_body, tk=tk),
        out_shape=jax.ShapeDtypeStruct((b, n, qw), ACT_DTYPE),
        grid=(b, n // tq),
        in_specs=[pl.BlockSpec((1, tq, qw), lambda bi, i: (bi, i, 0)),
                  pl.BlockSpec((1, kw, nk), lambda bi, i: (bi, 0, 0)),
                  pl.BlockSpec((1, nk, n_kv * LANES), lambda bi, i: (bi, 0, 0))],
        out_specs=pl.BlockSpec((1, tq, qw), lambda bi, i: (bi, i, 0)),
        scratch_shapes=[pltpu.VMEM((n_kv, 2 * tq, tk), F32), pltpu.VMEM((n_kv, 2 * tq, tk), F32)],
        compiler_params=_params("parallel", "parallel"),
        name="global_attn",
    )(q, kt, v)


def _merge_body(ya, yb, yc, yd, x_ref, on_ref, wo_ref, g1_ref, nf_ref, sh_ref, sc_ref, wr_ref,
                xo_ref, h_ref, aff_ref):
    parts = []
    for j, r in enumerate((ya, yb, yc, yd)):
        y = r[0].astype(F32)
        parts.append(_rms(y, on_ref[:, j * MIXER_WIDTH:(j + 1) * MIXER_WIDTH]).astype(MXU_DTYPE))
    xn = x_ref[0] + g1_ref[0] * _mm(jnp.concatenate(parts, axis=-1), wo_ref[...])
    xo_ref[0] = xn
    h = _rms(xn, nf_ref[...]) * (1.0 + sc_ref[0]) + sh_ref[0]
    _write_row_tiles(h_ref, (0,), h)
    logits = lax.dot_general(wr_ref[...], h, NT_DIMS, precision=HI, preferred_element_type=F32)
    e = jnp.exp(logits - jnp.max(logits, axis=0, keepdims=True))
    aff_ref[0] = e / jnp.sum(e, axis=0, keepdims=True)


def _merge(ys, x, lw, g1, shift, scale, tm):
    b, t, d = x.shape
    ne = lw["w_router_t"].shape[0]
    row = lambda bi, i: (0, 0)
    per_b = lambda bi, i: (bi, 0, 0)
    tile = lambda bi, i: (bi, i, 0)
    return pl.pallas_call(
        _merge_body,
        out_shape=[jax.ShapeDtypeStruct((b, t, d), F32), jax.ShapeDtypeStruct((b, t * ROW_TILE, LANES), F32),
                   jax.ShapeDtypeStruct((b, ne, t), F32)],
        grid=(b, t // tm),
        in_specs=[pl.BlockSpec((1, tm, MIXER_WIDTH), tile)] * 4 + [
            pl.BlockSpec((1, tm, d), tile),
            pl.BlockSpec((1, d), row),
            pl.BlockSpec((d, d), row),
            pl.BlockSpec((1, 1, d), per_b),
            pl.BlockSpec((1, d), row),
            pl.BlockSpec((1, 1, d), per_b),
            pl.BlockSpec((1, 1, d), per_b),
            pl.BlockSpec((ne, d), row)],
        out_specs=[pl.BlockSpec((1, tm, d), tile), pl.BlockSpec((1, tm * ROW_TILE, LANES), tile),
                   pl.BlockSpec((1, ne, tm), lambda bi, i: (bi, 0, i))],
        compiler_params=_params("parallel", "parallel"),
        name="merge",
    )(*ys, x, lw["out_norm"], lw["w_out"], g1, lw["norm_ffn"], shift, scale, lw["w_router_t"])


def _tri(n, m, mode):
    r = lax.broadcasted_iota(jnp.int32, (n, m), 0)
    c = lax.broadcasted_iota(jnp.int32, (n, m), 1)
    return jnp.where({"lt": r < c, "le": r <= c, "gt": r > c}[mode], 1.0, 0.0).astype(MXU_DTYPE)


def _count(mask, axes):
    out = jnp.where(mask, 1.0, 0.0)
    for ax in sorted(axes, reverse=True):
        out = jnp.sum(out, axis=ax, keepdims=True)
    return out


def _kth_largest_bits(bits, cap, axes):
    shape = tuple(1 if a in axes else s for a, s in enumerate(bits.shape))

    def body(i, t):
        cand = t | lax.shift_left(jnp.int32(1), 30 - i)
        return jnp.where(_count(bits >= cand, axes) >= cap, cand, t)

    return lax.fori_loop(0, 31, body, jnp.zeros(shape, jnp.int32))


def _prefix_tokens(m, exact_rows):
    e, r, l = m.shape
    m2 = m.reshape(e * r, l).astype(MXU_DTYPE)
    local = jnp.dot(m2, _tri(l, l, "lt"), preferred_element_type=F32)
    rowtot = jnp.dot(m2, jnp.ones((l, l), MXU_DTYPE), preferred_element_type=F32).reshape(e, r, l)
    below = _tri(r, r, "gt")
    if exact_rows:
        base = [jnp.dot(below, rowtot[i].astype(MXU_DTYPE), preferred_element_type=F32) for i in range(e)]
    else:
        base = [jnp.dot(below.astype(F32), rowtot[i], precision=HI, preferred_element_type=F32)
                for i in range(e)]
    return local.reshape(e, r, l) + jnp.stack(base, axis=0)


def _select_mask(aff, cap, prefix_fn, axes):
    bits = pltpu.bitcast(aff, jnp.int32)
    thr = _kth_largest_bits(bits, cap, axes)
    gt = bits > thr
    eq = bits == thr
    need = cap - _count(gt, axes)
    eq_rank = prefix_fn(jnp.where(eq, 1.0, 0.0))
    take_eq = jnp.where(eq, jnp.where(eq_rank < need, 1.0, 0.0), 0.0)
    return jnp.where(gt, 1.0, take_eq)


def _select_body(aff_ref, sel_ref, prank_ref, tstart_ref, tend_ref, *, cap):
    aff = aff_ref[0]
    ne = aff.shape[0]
    sel = _select_mask(aff, cap, functools.partial(_prefix_tokens, exact_rows=True), (1, 2))
    sel_ref[0] = sel
    cnt = jnp.sum(sel, axis=0)
    tstart = _prefix_tokens(cnt[None], exact_rows=False)[0]
    tstart_ref[0] = tstart.astype(jnp.int32)
    tend_ref[0] = (tstart + cnt).astype(jnp.int32)
    run = tstart
    for e in range(ne):
        prank_ref[0, e] = run.astype(jnp.int32)
        run = run + sel[e]


def _select(aff4, cap):
    b, ne, r, l = aff4.shape
    blk4 = pl.BlockSpec((1, ne, r, l), lambda bi: (bi, 0, 0, 0))
    blk3 = pl.BlockSpec((1, r, l), lambda bi: (bi, 0, 0))
    return pl.pallas_call(
        functools.partial(_select_body, cap=cap),
        out_shape=[jax.ShapeDtypeStruct((b, ne, r, l), F32), jax.ShapeDtypeStruct((b, ne, r, l), jnp.int32),
                   jax.ShapeDtypeStruct((b, r, l), jnp.int32), jax.ShapeDtypeStruct((b, r, l), jnp.int32)],
        grid=(b,),
        in_specs=[blk4],
        out_specs=[blk4, blk4, blk3, blk3],
        compiler_params=_params("parallel"),
        name="moe_select",
    )(aff4)


def _slots_body(sel_ref, aff_ref, prank_ref, idx_ref, dest_ref, gate_ref):
    m = sel_ref[0, 0]
    r, l = m.shape
    cap = idx_ref.shape[2]
    mb = m.astype(MXU_DTYPE)
    linc = jnp.dot(mb, _tri(l, l, "le"), preferred_element_type=F32)
    rowtot = jnp.dot(mb, jnp.ones((l, l), MXU_DTYPE), preferred_element_type=F32)
    rowtot_lane = lax.dot_general(jnp.ones((8, l), MXU_DTYPE), mb, NT_DIMS, preferred_element_type=F32)
    cumrow = jnp.dot(rowtot_lane.astype(MXU_DTYPE), _tri(r, r, "le"), preferred_element_type=F32)[0:1]
    slot = lax.broadcasted_iota(jnp.int32, (cap, r), 0).astype(F32)
    passed = jnp.where(cumrow <= slot, 1.0, 0.0).astype(MXU_DTYPE)
    row_of = jnp.dot(passed, jnp.ones((r, l), MXU_DTYPE), preferred_element_type=F32)[:, 0:1]
    base_of = jnp.dot(passed, rowtot.astype(MXU_DTYPE), preferred_element_type=F32)[:, 0:1]
    onehot = jnp.where(lax.broadcasted_iota(jnp.int32, (cap, r), 1).astype(F32) == row_of, 1.0, 0.0)
    linc_of = jnp.dot(onehot.astype(MXU_DTYPE), linc.astype(MXU_DTYPE), preferred_element_type=F32)
    k = slot[:, 0:1] - base_of
    col_of = jnp.sum(jnp.where(linc_of <= k, 1.0, 0.0), axis=-1, keepdims=True)
    idx_ref[0, 0] = (row_of * l + col_of).astype(jnp.int32)
    at_col = lax.broadcasted_iota(jnp.int32, (cap, l), 1).astype(F32) == col_of
    hot = onehot.astype(MXU_DTYPE)
    aff_rows, rest = jnp.zeros((cap, l), F32), aff_ref[0, 0]
    for _ in range(3):
        piece = rest.astype(MXU_DTYPE)
        aff_rows = aff_rows + jnp.dot(hot, piece, preferred_element_type=F32)
        rest = rest - piece.astype(F32)
    gate_ref[0, 0] = jnp.sum(jnp.where(at_col, aff_rows, 0.0), axis=-1, keepdims=True)
    prank = prank_ref[0, 0]
    prank_rows = (jnp.dot(hot, (prank >> 8).astype(F32).astype(MXU_DTYPE), preferred_element_type=F32) * 256.0
                  + jnp.dot(hot, (prank & 255).astype(F32).astype(MXU_DTYPE), preferred_element_type=F32))
    dest_ref[0, 0] = jnp.sum(jnp.where(at_col, prank_rows, 0.0), axis=-1, keepdims=True).astype(jnp.int32)


def _slots(sel, aff4, prank, cap):
    b, ne, r, l = sel.shape
    blk = pl.BlockSpec((1, 1, r, l), lambda bi, e: (bi, e, 0, 0))
    oblk = pl.BlockSpec((1, 1, cap, 1), lambda bi, e: (bi, e, 0, 0))
    return pl.pallas_call(
        _slots_body,
        out_shape=[jax.ShapeDtypeStruct((b, ne, cap, 1), jnp.int32), jax.ShapeDtypeStruct((b, ne, cap, 1), jnp.int32),
                   jax.ShapeDtypeStruct((b, ne, cap, 1), F32)],
        grid=(b, ne),
        in_specs=[blk, blk, blk],
        out_specs=[oblk, oblk, oblk],
        compiler_params=_params("parallel", "parallel"),
        name="moe_slots",
    )(sel, aff4, prank)


FFN_CHUNK = 256
DMA_THREADS = 1


def _ffn_body(idx_ref, dest_ref, h_hbm, gate_ref, wg_ref, wu_ref, wd_ref, z_hbm, xa, xb, ya, yb, sems,
              *, n_tok, n_pair):
    n_b, tiles = pl.num_programs(1), pl.num_programs(2)
    step = (pl.program_id(0) * n_b + pl.program_id(1)) * tiles + pl.program_id(2)
    last = pl.num_programs(0) * n_b * tiles - 1
    ts = xa.shape[0] // ROW_TILE
    n_groups = wg_ref.shape[2] // FFN_CHUNK

    def sample_of(k):
        return (k // tiles) % n_b

    def tile_of(ref, r):
        return ref.at[pl.ds(pl.multiple_of(r * ROW_TILE, ROW_TILE), ROW_TILE), :]

    def gather(k, half, buf, sem):
        base, rows = (2 * k + half) * ts, sample_of(k) * n_tok
        return lambda s: pltpu.make_async_copy(
            tile_of(h_hbm, rows + idx_ref[base + s]), tile_of(buf, s), sems.at[sem])

    def scatter(k, half, buf, sem):
        base, rows = (2 * k + half) * ts, sample_of(k) * n_pair
        return lambda s: pltpu.make_async_copy(
            tile_of(buf, s), tile_of(z_hbm, rows + dest_ref[base + s]), sems.at[sem])

    def start_all(copy):
        def body(g, c):
            for u in range(8):
                copy(g * 8 + u).start(priority=u % DMA_THREADS)
            return c
        lax.fori_loop(0, ts // 8, body, 0)

    def wait_rows(buf, sem):
        pltpu.make_async_copy(buf, buf, sems.at[sem]).wait()

    def ffn(xbuf, gate, copies):
        x = _read_row_tiles(xbuf, (), ts).astype(MXU_DTYPE)
        for copy in copies:
            for s in range(ts):
                copy(s).start(priority=s % DMA_THREADS)
        y = None
        for j in range(n_groups):
            cols = slice(j * FFN_CHUNK, (j + 1) * FFN_CHUNK)
            hid = _silu(_mm(x, wg_ref[0, :, cols])) * _mm(x, wu_ref[0, :, cols])
            part = _mm(hid, wd_ref[0, cols, :])
            y = part if y is None else y + part
        return y * gate

    nxt = jnp.minimum(step + 1, last)
    gate_a, gate_b = gate_ref[0, 0, 0:ts], gate_ref[0, 0, ts:2 * ts]
    prefetch = [gather(step, 1, xb, 1), gather(nxt, 0, xa, 0)]

    @pl.when(step == 0)
    def _():
        start_all(gather(step, 0, xa, 0))
        wait_rows(xa, 0)
        _write_row_tiles(ya, (), ffn(xa, gate_a, prefetch))

    @pl.when(step > 0)
    def _():
        wait_rows(xa, 0)
        y_a = ffn(xa, gate_a, [scatter(step - 1, 0, ya, 2), scatter(step - 1, 1, yb, 3)] + prefetch)
        wait_rows(ya, 2)
        _write_row_tiles(ya, (), y_a)

    wait_rows(xb, 1)
    y_b = ffn(xb, gate_b, [])

    @pl.when(step > 0)
    def _():
        wait_rows(yb, 3)

    _write_row_tiles(yb, (), y_b)

    @pl.when(step == last)
    def _():
        start_all(scatter(step, 0, ya, 2))
        start_all(scatter(step, 1, yb, 3))
        wait_rows(xa, 0)
        wait_rows(ya, 2)
        wait_rows(yb, 3)


def _expert_ffn(idx, dest, gate, h, lw, ts):
    b, ne, cap, _ = gate.shape
    n, d = h.shape[1] // ROW_TILE, ROW_TILE * LANES
    f = lw["w_gate"].shape[-1]
    layer = lw["layer"]
    n_pair = ne * cap
    by_expert = lambda a: jnp.swapaxes(a, 0, 1)
    grid_spec = pltpu.PrefetchScalarGridSpec(
        num_scalar_prefetch=2,
        grid=(ne, b, cap // (2 * ts)),
        in_specs=[pl.BlockSpec(memory_space=pl.ANY),
                  pl.BlockSpec((1, 1, 2 * ts, 1), lambda e, bi, t, *_: (e, bi, t, 0)),
                  pl.BlockSpec((None, 1, d, f), lambda e, bi, t, *_: (layer, e, 0, 0)),
                  pl.BlockSpec((None, 1, d, f), lambda e, bi, t, *_: (layer, e, 0, 0)),
                  pl.BlockSpec((None, 1, f, d), lambda e, bi, t, *_: (layer, e, 0, 0))],
        out_specs=pl.BlockSpec(memory_space=pl.ANY),
        scratch_shapes=[pltpu.VMEM((ts * ROW_TILE, LANES), F32)] * 4 + [pltpu.SemaphoreType.DMA((4,))])
    return pl.pallas_call(
        functools.partial(_ffn_body, n_tok=n, n_pair=n_pair),
        out_shape=jax.ShapeDtypeStruct((b * n_pair * ROW_TILE, LANES), F32),
        grid_spec=grid_spec,
        compiler_params=_params("arbitrary", "arbitrary", "arbitrary"),
        name="moe_ffn",
    )(by_expert(idx).reshape(-1), by_expert(dest).reshape(-1), h.reshape(b * n * ROW_TILE, LANES), by_expert(gate),
      lw["w_gate"], lw["w_up"], lw["w_down"])


COMBINE_DEPTH = 4


def _combine_body(tb_ref, x_ref, g2_ref, ts_ref, te_ref, fg_ref, z_hbm, o_ref, zbuf, acc_ref, sem,
                  *, n_pair, final):
    n_tiles = pl.num_programs(1)
    tile = pl.program_id(0) * n_tiles + pl.program_id(1)
    depth, pc = zbuf.shape[0], zbuf.shape[1] // ROW_TILE
    tt = x_ref.shape[1]

    def plan(g):
        b, i = g // n_tiles, g % n_tiles
        lo, hi = tb_ref[b * (n_tiles + 1) + i], tb_ref[b * (n_tiles + 1) + i + 1]
        p0 = (lo // 8) * 8
        return b, p0, (hi - p0 + pc - 1) // pc

    def fetch(b, p0, k, slot):
        cs = pl.multiple_of(jnp.minimum(p0 + k * pc, n_pair - pc), 8)
        rows = pl.ds(pl.multiple_of((b * n_pair + cs) * ROW_TILE, 8 * ROW_TILE), pc * ROW_TILE)
        return cs, pltpu.make_async_copy(z_hbm.at[rows, :], zbuf.at[slot], sem.at[slot])

    b, p0, n_chunks = plan(tile)
    start_row, end_row = ts_ref[0, 0], te_ref[0, 0]
    acc_ref[...] = jnp.zeros_like(acc_ref)

    def start_head(b, p0, n_chunks):
        for k in range(depth - 1):
            @pl.when(k < n_chunks)
            def _():
                fetch(b, p0, k, k)[1].start()

    @pl.when(tile == 0)
    def _():
        start_head(b, p0, n_chunks)

    def chunk(k, c):
        slot = k % depth
        ahead = k + depth - 1

        @pl.when(ahead < n_chunks)
        def _():
            fetch(b, p0, ahead, ahead % depth)[1].start()

        cs, cp = fetch(b, p0, k, slot)
        cp.wait()
        pair = cs + lax.broadcasted_iota(jnp.int32, (pc, tt), 0)
        own = jnp.where(pair >= jnp.maximum(start_row, p0 + k * pc), jnp.where(pair < end_row, 1.0, 0.0), 0.0)
        own = own.astype(MXU_DTYPE)
        z = _read_row_tiles(zbuf, (slot,), pc).astype(MXU_DTYPE)
        acc_ref[...] += lax.dot_general(own, z, TN_DIMS, preferred_element_type=F32)
        return c

    lax.fori_loop(0, n_chunks, chunk, 0)

    @pl.when(tile + 1 < pl.num_programs(0) * n_tiles)
    def _():
        start_head(*plan(tile + 1))

    out = x_ref[0] + g2_ref[0] * acc_ref[...]
    o_ref[0] = _rms(out, fg_ref[...]) if final else out


def _combine(x, g2, tstart, tend, z, n_pair, tt, pc, final_g, final):
    b, n, d = x.shape
    nt = n // tt
    ts4 = tstart.reshape(b, nt, 1, tt)
    te4 = tend.reshape(b, nt, 1, tt)
    bounds = jnp.concatenate([ts4[:, :, 0, 0], jnp.full((b, 1), n_pair, jnp.int32)], axis=1).reshape(-1)
    grid_spec = pltpu.PrefetchScalarGridSpec(
        num_scalar_prefetch=1,
        grid=(b, nt),
        in_specs=[pl.BlockSpec((1, tt, d), lambda bi, i, *_: (bi, i, 0)),
                  pl.BlockSpec((1, 1, d), lambda bi, i, *_: (bi, 0, 0)),
                  pl.BlockSpec((1, 1, 1, tt), lambda bi, i, *_: (bi, i, 0, 0)),
                  pl.BlockSpec((1, 1, 1, tt), lambda bi, i, *_: (bi, i, 0, 0)),
                  pl.BlockSpec((1, d), lambda bi, i, *_: (0, 0)),
                  pl.BlockSpec(memory_space=pl.ANY)],
        out_specs=pl.BlockSpec((1, tt, d), lambda bi, i, *_: (bi, i, 0)),
        scratch_shapes=[pltpu.VMEM((COMBINE_DEPTH, pc * ROW_TILE, LANES), F32), pltpu.VMEM((tt, d), F32),
                        pltpu.SemaphoreType.DMA((COMBINE_DEPTH,))])
    return pl.pallas_call(
        functools.partial(_combine_body, n_pair=n_pair, final=final),
        out_shape=jax.ShapeDtypeStruct((b, n, d), F32),
        grid_spec=grid_spec,
        compiler_params=_params("arbitrary", "arbitrary"),
        name="moe_combine",
    )(bounds, x, g2, ts4, te4, final_g, z)


def _moe_latent(x, h, aff, g2, lw, final_g, final):
    b, n, d = x.shape
    ne = aff.shape[1]
    cap = EC_CAPACITY_FACTOR * n // ne
    aff4 = aff.reshape(b, ne, n // LANES, LANES)
    sel, prank, tstart, tend = _select(aff4, cap)
    idx, dest, gate = _slots(sel, aff4, prank, cap)
    z = _expert_ffn(idx, dest, gate, h, lw, min(cap // 2, FFN_SLOTS))
    assert ne * cap <= 1 << 16
    return _combine(x, g2, tstart, tend, z, ne * cap, COMBINE_TOKENS, COMBINE_PAIRS, final_g, final)


def _ctx_coef_body(aff_ref, coef_ref, slot_ref, *, cap):
    nb, _, nc = aff_ref.shape
    excl = _tri(nc, nc, "lt")
    prefix = lambda mm: jnp.dot(mm.astype(MXU_DTYPE), excl, preferred_element_type=F32)
    coefs, slots = [], []
    for b in range(nb):
        aff = aff_ref[b]
        sel = _select_mask(aff, cap, prefix, (1,))
        coefs.append(sel * aff)
        slots.append(jnp.where(sel > 0.0, prefix(sel) + float(b * cap), -1.0))
    coef_ref[:, 0, :] = jnp.concatenate(coefs, axis=-1)
    slot_ref[:, 0, :] = jnp.concatenate(slots, axis=-1)


def _ctx_coef(aff, cap):
    b, ne, nc = aff.shape
    return pl.pallas_call(
        functools.partial(_ctx_coef_body, cap=cap),
        out_shape=[jax.ShapeDtypeStruct((ne, 1, b * nc), F32)] * 2,
        name="ctx_moe_select",
    )(aff)


def _ctx_ffn_body(h_ref, coef_ref, slot_ref, wg_ref, wu_ref, wd_ref, x_ref, g2_ref, o_ref, *, n_slots):
    e = pl.program_id(0)
    rows = h_ref.shape[0] // ROW_TILE
    h = _read_row_tiles(h_ref, (), rows)
    picked = jnp.where(lax.broadcasted_iota(jnp.int32, (n_slots, rows), 0).astype(F32) == slot_ref[0], 1.0, 0.0)
    gate = jnp.sum(picked * coef_ref[0], axis=-1, keepdims=True)
    x = _mm(picked, h)
    y = _mm(_silu(_mm(x, wg_ref[0])) * _mm(x, wu_ref[0]), wd_ref[0]) * gate

    @pl.when(e == 0)
    def _():
        o_ref[...] = x_ref[...]

    o_ref[...] += g2_ref[...] * lax.dot_general(picked.astype(MXU_DTYPE), y.astype(MXU_DTYPE), TN_DIMS,
                                                 preferred_element_type=F32)


def _ctx_moe(xc, hc, aff, g2, lw):
    b, nc, d = xc.shape
    ne = aff.shape[1]
    f = lw["w_gate"].shape[-1]
    layer = lw["layer"]
    cap = EC_CAPACITY_FACTOR * nc // ne
    coef, slot = _ctx_coef(aff, cap)
    rows = b * nc
    per_e = pl.BlockSpec((1, 1, rows), lambda e: (e, 0, 0))
    out = pl.pallas_call(
        functools.partial(_ctx_ffn_body, n_slots=b * cap),
        out_shape=jax.ShapeDtypeStruct((rows, d), F32),
        grid=(ne,),
        in_specs=[pl.BlockSpec((rows * ROW_TILE, LANES), lambda e: (0, 0)),
                  per_e, per_e,
                  pl.BlockSpec((None, 1, d, f), lambda e: (layer, e, 0, 0)),
                  pl.BlockSpec((None, 1, d, f), lambda e: (layer, e, 0, 0)),
                  pl.BlockSpec((None, 1, f, d), lambda e: (layer, e, 0, 0)),
                  pl.BlockSpec((rows, d), lambda e: (0, 0)),
                  pl.BlockSpec((1, d), lambda e: (0, 0))],
        out_specs=pl.BlockSpec((rows, d), lambda e: (0, 0)),
        compiler_params=_params("arbitrary"),
        name="ctx_moe_ffn",
    )(hc.reshape(rows * ROW_TILE, LANES), coef, slot, lw["w_gate"], lw["w_up"], lw["w_down"],
      xc.reshape(rows, d), g2)
    return out.reshape(b, nc, d)


def _rope_tables(n):
    t = np.arange(n)
    freqs = ROPE_THETA ** (-np.arange(ROPE_FREQS, dtype=np.float32) / ROPE_FREQS)
    ang_r = (t // GRID_W).astype(np.float32)[:, None] * freqs
    ang_c = (t % GRID_W).astype(np.float32)[:, None] * freqs
    cos = np.concatenate([np.cos(ang_r)] * 2 + [np.cos(ang_c)] * 2, axis=1)
    sin = np.concatenate([-np.sin(ang_r), np.sin(ang_r), -np.sin(ang_c), np.sin(ang_c)], axis=1)
    reps = LANES // HEAD_DIM
    return (jnp.asarray(np.tile(cos, (1, reps)), F32), jnp.asarray(np.tile(sin, (1, reps)), F32))


def _block_geometry(n):
    nblk = n // ATT_BLOCK
    a = np.arange(ATT_BLOCK)
    out = []
    for i in (0, 1, nblk - 1):
        j0 = int(np.clip(i - 1, 0, nblk - 3))
        out.append((i * ATT_BLOCK + a, [(j0 + j) * ATT_BLOCK + a for j in range(3)]))
    return out


def _neighbourhood_bias(rpb, n):
    rows = n // GRID_W
    win_r = min(NA_WIN_R, rows)
    blk_rows = ATT_BLOCK // GRID_W
    cols = np.arange(GRID_W)
    col_off = cols[None, :] - cols[:, None] + NA_WIN_C - 1
    col_hot = jnp.asarray(col_off[:, :, None] == np.arange(2 * NA_WIN_C - 1), F32)
    c0 = np.clip(cols - NA_WIN_C // 2, 0, GRID_W - NA_WIN_C)
    col_ok = (cols[None, :] >= c0[:, None]) & (cols[None, :] < c0[:, None] + NA_WIN_C)
    kinds = []
    for q_tok, k_chunks in _block_geometry(n):
        qr = q_tok[::GRID_W] // GRID_W
        r0 = np.clip(qr - win_r // 2, 0, rows - win_r)
        chunks = []
        for k_tok in k_chunks:
            kr = k_tok[::GRID_W] // GRID_W
            row_off = kr[None, :] - qr[:, None] + NA_WIN_R - 1
            row_hot = jnp.asarray(row_off[:, :, None] == np.arange(2 * NA_WIN_R - 1), F32)
            row_ok = (kr[None, :] >= r0[:, None]) & (kr[None, :] < r0[:, None] + win_r)
            ok = (row_ok[:, None, :, None] & col_ok[None, :, None, :]).reshape(ATT_BLOCK, ATT_BLOCK)
            vals = jnp.einsum("qkr,hrc,xyc->hqxky", row_hot, rpb, col_hot, precision=HI)
            vals = vals.reshape(-1, blk_rows * GRID_W, blk_rows * GRID_W) * LOG2E
            chunks.append(jnp.where(ok[None], vals, NEG_INF))
        kinds.append(jnp.stack(chunks, axis=1))
    return jnp.stack(kinds, axis=0)


def _window_mask(n):
    kinds = []
    for q_tok, k_chunks in _block_geometry(n):
        kinds.append(np.stack([np.where(np.abs(k_tok[None] - q_tok[:, None]) <= WINDOW, 0.0, NEG_INF)
                               for k_tok in k_chunks])[None])
    return jnp.asarray(np.stack(kinds), F32)


def kernel(x, c, ctx, c_ctx, w_mod, b_mod, norm_mix, norm_ffn, w_in, rpb, q_norm, k_norm, sink, sgu_norm, w_sgu,
           b_sgu, out_norm, w_out, w_router, w_gate, w_up, w_down, final_norm):
    depth = w_mod.shape[0]
    b, n, d = x.shape
    nc = ctx.shape[1]
    group_w = MIXER_WIDTH // SG_GROUPS

    cvecs = jnp.concatenate([c, c_ctx[None], jnp.zeros((8 - b - 1, d), F32)], axis=0)
    mods = _adaln_all(cvecs, w_mod, b_mod).reshape(depth, 8, 6, d)

    cos, sin = _rope_tables(n)
    cos_c, sin_c = jnp.ones((nc, LANES), F32), jnp.zeros((nc, LANES), F32)
    win_mask = _window_mask(n)
    blk = np.arange(256) // HEAD_DIM
    gsum = jnp.asarray(blk[:, None] == blk[None, :], MXU_DTYPE)
    wg_all, wu_all, wd_all = (w.astype(MXU_DTYPE) for w in (w_gate, w_up, w_down))

    xc = ctx
    for l in range(depth):
        ctx_needed = l < depth - 1
        lat = [mods[l, :b, j][:, None, :] for j in range(6)]
        cm = [jnp.broadcast_to(mods[l, b, j][None, None, :], (b, 1, d)) for j in range(6)]
        lw = {
            "w_in": w_in[l].astype(MXU_DTYPE),
            "qn": jnp.tile(q_norm[l], 4)[None], "kn": jnp.tile(k_norm[l], 2)[None], "gn": sgu_norm[l][None],
            "w_sgu": w_sgu[l].astype(MXU_DTYPE),
            "b_sgu": jnp.repeat(b_sgu[l].T, group_w, axis=1),
            "gsum": gsum,
            "out_norm": out_norm[l][None], "w_out": w_out[l].astype(MXU_DTYPE), "norm_ffn": norm_ffn[l][None],
            "w_router_t": w_router[l].T,
            "layer": l, "w_gate": wg_all, "w_up": wu_all, "w_down": wd_all,
        }
        nm = norm_mix[l][None]
        hp = _inproj(x, nm, lat[0], lat[1], lw, cos, sin, ROW_BLOCK)
        cp = _inproj(xc, nm, cm[0], cm[1], lw, cos_c, sin_c, nc)
        qa, ka, va, qb, kb, vb, qs, ks, vs, yd = hp
        _, ka_c, va_c, _, kb_c, vb_c, _, ks_c, vs_c, _ = cp
        sink_l = sink[l] * LOG2E

        ya = _local_attn(qa, ka, va, ka_c, va_c, _neighbourhood_bias(rpb[l], n))
        yb = _global_attn(qb, jnp.concatenate([kb, kb_c], axis=1), jnp.concatenate([vb, vb_c], axis=1))
        yc = _local_attn(qs, ks, vs, ks_c, vs_c, win_mask, sink_l)
        x_mid, h2, aff = _merge((ya, yb, yc, yd), x, lw, lat[2], lat[3], lat[4], ROW_BLOCK)
        x = _moe_latent(x_mid, h2, aff, lat[5], lw, final_norm[None], l == depth - 1)

        if ctx_needed:
            ys_c = _ctx_attn(sink_l, cp[:9])
            xc_mid, hc2, aff_c = _merge((*ys_c, cp[9]), xc, lw, cm[2], cm[3], cm[4], nc)
            xc = _ctx_moe(xc_mid, hc2, aff_c, mods[l, b, 5][None], lw)
    return x
```
